```python
import jax
import jax.numpy as jnp
from jax import lax
import numpy as np

D_MODEL = 2048
BATCH = 2
SEQ = 8192
DEPTH = 4

SB_HEADS = 8
SB_HEAD_DIM = 128
SB_WIDTH = SB_HEADS * SB_HEAD_DIM
GLA_HEADS = 4
GLA_KEY_DIM = 128
GLA_VAL_DIM = 256
GLA_KEY_WIDTH = GLA_HEADS * GLA_KEY_DIM
GLA_VAL_WIDTH = GLA_HEADS * GLA_VAL_DIM
GLA_GATE_RANK = 16
GLA_GATE_NORM = 16.0
GLA_CHUNK = 64
MLA_HEADS = 16
MLA_Q_RANK = 512
MLA_KV_RANK = 512
MLA_NOPE_DIM = 128
MLA_ROPE_DIM = 64
MLA_V_DIM = 128
MLA_WIDTH = MLA_HEADS * MLA_V_DIM
ROPE_THETA = 10000.0
Q_BLOCK = 128
NORM_EPS = 1e-6

EVEN_SIZES = (SB_WIDTH, SB_WIDTH, SB_WIDTH, SB_WIDTH,
              GLA_KEY_WIDTH, GLA_KEY_WIDTH, GLA_VAL_WIDTH, GLA_VAL_WIDTH, GLA_GATE_RANK)
EVEN_IN = sum(EVEN_SIZES)
EVEN_MIX = SB_WIDTH + GLA_VAL_WIDTH
ODD_SIZES = (MLA_Q_RANK, MLA_KV_RANK, MLA_ROPE_DIM, MLA_WIDTH)
ODD_IN = sum(ODD_SIZES)
N_EVEN = (DEPTH + 1) // 2
N_ODD = DEPTH // 2

kernel_name = 'hybrid_stickbreak_gla_mla_trunk'


def _split_points(sizes):
    return [int(v) for v in np.cumsum(sizes)[:-1]]


def rmsnorm(x, g):
    xf = x.astype(jnp.float32)
    y = xf * lax.rsqrt(jnp.mean(xf * xf, axis=-1, keepdims=True) + NORM_EPS)
    return (y * g.astype(jnp.float32)).astype(x.dtype)


def stick_breaking_attention(q, k, v):
    B, S, H, Dh = q.shape
    nb = S // Q_BLOCK
    scale = Dh ** -0.5
    qb = q.reshape(B, nb, Q_BLOCK, H, Dh).transpose(1, 0, 3, 2, 4)
    kh = k.transpose(0, 2, 1, 3)
    vh = v.transpose(0, 2, 1, 3)
    key_pos = jnp.arange(S)

    def block(args):
        q_blk, blk = args
        q_pos = blk * Q_BLOCK + jnp.arange(Q_BLOCK)
        z = jnp.einsum('bhqd,bhkd->bhqk', q_blk, kh).astype(jnp.float32) * scale
        mask = key_pos[None, :] < q_pos[:, None]
        log_stay = jnp.where(mask, jax.nn.log_sigmoid(-z), 0.0)
        later = lax.cumsum(log_stay, axis=3, reverse=True) - log_stay
        w = jnp.where(mask, jnp.exp(jax.nn.log_sigmoid(z) + later), 0.0)
        return jnp.einsum('bhqk,bhkd->bhqd', w.astype(vh.dtype), vh)

    o = lax.map(block, (qb, jnp.arange(nb)))
    return o.transpose(1, 0, 3, 2, 4).reshape(B, S, H * Dh)


def gla_chunked(q, k, v, log_f):
    B, S, H, Dk = q.shape
    Dv = v.shape[-1]
    C = GLA_CHUNK
    nc = S // C

    def to_chunks(t):
        return t.astype(jnp.float32).reshape(B, nc, C, H, t.shape[-1]).transpose(1, 0, 3, 2, 4)

    qc = to_chunks(q) * (Dk ** -0.5)
    kc, vc, gc = to_chunks(k), to_chunks(v), to_chunks(log_f)
    causal = jnp.tril(jnp.ones((C, C), dtype=bool))[None, None, :, :, None]

    def step(state, inp):
        q_c, k_c, v_c, g_c = inp
        b = jnp.cumsum(g_c, axis=2)
        o_inter = jnp.einsum('bhcd,bhde->bhce', q_c * jnp.exp(b), state)
        diff = b[:, :, :, None, :] - b[:, :, None, :, :]
        decay = jnp.where(causal, jnp.exp(jnp.minimum(diff, 0.0)), 0.0)
        att = jnp.einsum('bhijd,bhjd->bhij', q_c[:, :, :, None, :] * decay, k_c)
        o_intra = jnp.einsum('bhij,bhje->bhie', att, v_c)
        b_last = b[:, :, -1:, :]
        new_state = (jnp.exp(b_last[:, :, 0, :])[..., None] * state
                     + jnp.einsum('bhcd,bhce->bhde', k_c * jnp.exp(b_last - b), v_c))
        return new_state, o_inter + o_intra

    state0 = jnp.zeros((B, H, Dk, Dv), jnp.float32)
    _, o = lax.scan(step, state0, (qc, kc, vc, gc))
    return o.transpose(1, 0, 3, 2, 4).reshape(B, S, H, Dv)


def rope_tables(positions, dim):
    half = dim // 2
    inv = ROPE_THETA ** (-jnp.arange(half, dtype=jnp.float32) / half)
    ang = positions.astype(jnp.float32)[..., None] * inv
    return jnp.cos(ang), jnp.sin(ang)


def apply_rope(x, cos, sin):
    half = x.shape[-1] // 2
    x1, x2 = x[..., :half], x[..., half:]
    cos = cos.astype(x.dtype)
    sin = sin.astype(x.dtype)
    return jnp.concatenate([x1 * cos - x2 * sin, x2 * cos + x1 * sin], axis=-1)


def mla_attention(q_nope, q_rope, k_nope, k_rope, v):
    B, S, H, Dn = q_nope.shape
    Dr = q_rope.shape[-1]
    Dv = v.shape[-1]
    nb = S // Q_BLOCK
    scale = (Dn + Dr) ** -0.5
    qn = q_nope.reshape(B, nb, Q_BLOCK, H, Dn).transpose(1, 0, 3, 2, 4)
    qr = q_rope.reshape(B, nb, Q_BLOCK, H, Dr).transpose(1, 0, 3, 2, 4)
    kn = k_nope.transpose(0, 2, 1, 3)
    vh = v.transpose(0, 2, 1, 3)
    key_pos = jnp.arange(S)

    def block(args):
        qn_b, qr_b, blk = args
        q_pos = blk * Q_BLOCK + jnp.arange(Q_BLOCK)
        s = (jnp.einsum('bhqd,bhkd->bhqk', qn_b, kn)
             + jnp.einsum('bhqd,bkd->bhqk', qr_b, k_rope)).astype(jnp.float32) * scale
        s = jnp.where(key_pos[None, :] <= q_pos[:, None], s, -jnp.inf)
        p = jax.nn.softmax(s, axis=-1)
        return jnp.einsum('bhqk,bhkd->bhqd', p.astype(vh.dtype), vh)

    o = lax.map(block, (qn, qr, jnp.arange(nb)))
    return o.transpose(1, 0, 3, 2, 4).reshape(B, S, H * Dv)


def even_layer(x, norm_g, w_in, alpha_up, alpha_bias, gla_norm_g, w_out):
    B, S, _ = x.shape
    h = rmsnorm(x, norm_g)
    z = h @ w_in
    sb_q, sb_k, sb_v, sb_g, gq, gk, gv, gg, ga = jnp.split(z, _split_points(EVEN_SIZES), axis=-1)
    o_a = stick_breaking_attention(sb_q.reshape(B, S, SB_HEADS, SB_HEAD_DIM),
                                   sb_k.reshape(B, S, SB_HEADS, SB_HEAD_DIM),
                                   sb_v.reshape(B, S, SB_HEADS, SB_HEAD_DIM))
    y_a = o_a * jax.nn.silu(sb_g)
    log_f = jax.nn.log_sigmoid((ga @ alpha_up + alpha_bias).astype(jnp.float32)) / GLA_GATE_NORM
    o_b = gla_chunked(gq.reshape(B, S, GLA_HEADS, GLA_KEY_DIM),
                      gk.reshape(B, S, GLA_HEADS, GLA_KEY_DIM),
                      gv.reshape(B, S, GLA_HEADS, GLA_VAL_DIM),
                      log_f.reshape(B, S, GLA_HEADS, GLA_KEY_DIM))
    o_b = rmsnorm(o_b.astype(x.dtype), gla_norm_g).reshape(B, S, GLA_VAL_WIDTH)
    y_b = o_b * jax.nn.silu(gg)
    y = jnp.concatenate([y_a, y_b], axis=-1) @ w_out
    return x + y


def odd_layer(x, positions, norm_g, w_in, q_norm_g, w_q_up, kv_norm_g, w_kv_up, w_out):
    B, S, _ = x.shape
    h = rmsnorm(x, norm_g)
    q_lat, kv_lat, k_rope, gate = jnp.split(h @ w_in, _split_points(ODD_SIZES), axis=-1)
    cos, sin = rope_tables(positions, MLA_ROPE_DIM)
    q = (rmsnorm(q_lat, q_norm_g) @ w_q_up).reshape(B, S, MLA_HEADS, MLA_NOPE_DIM + MLA_ROPE_DIM)
    q_nope = q[..., :MLA_NOPE_DIM]
    q_rope = apply_rope(q[..., MLA_NOPE_DIM:], cos[:, :, None, :], sin[:, :, None, :])
    kv = (rmsnorm(kv_lat, kv_norm_g) @ w_kv_up).reshape(B, S, MLA_HEADS, MLA_NOPE_DIM + MLA_V_DIM)
    k_nope = kv[..., :MLA_NOPE_DIM]
    v = kv[..., MLA_NOPE_DIM:]
    k_rope = apply_rope(k_rope, cos, sin)
    o = mla_attention(q_nope, q_rope, k_nope, k_rope, v)
    y = (o * jax.nn.silu(gate)) @ w_out
    return x + y


def setup_inputs(seed: int = 0) -> dict:
    key = jax.random.key(seed)
    ks = jax.random.split(key, 20)
    f32 = jnp.float32

    def dense(k, shape, fan_in):
        return jax.random.normal(k, shape, f32) * (fan_in ** -0.5)

    def gain(k, shape):
        return 1.0 + 0.02 * jax.random.normal(k, shape, f32)

    x = jax.random.normal(ks[0], (BATCH, SEQ, D_MODEL), f32)
    offset = jax.random.randint(ks[1], (BATCH, 1), 0, 1024, dtype=jnp.int32)
    positions = jnp.arange(SEQ, dtype=jnp.int32)[None, :] + offset
    return {
        'x': x,
        'positions': positions,
        'ln_even': gain(ks[2], (N_EVEN, D_MODEL)),
        'w_in_even': dense(ks[3], (N_EVEN, D_MODEL, EVEN_IN), D_MODEL),
        'gla_alpha_up': dense(ks[4], (N_EVEN, GLA_GATE_RANK, GLA_KEY_WIDTH), GLA_GATE_RANK),
        'gla_alpha_bias': 0.1 * jax.random.normal(ks[5], (N_EVEN, GLA_KEY_WIDTH), f32),
        'gla_norm': gain(ks[6], (N_EVEN, GLA_VAL_DIM)),
        'w_out_even': dense(ks[7], (N_EVEN, EVEN_MIX, D_MODEL), EVEN_MIX),
        'ln_odd': gain(ks[8], (N_ODD, D_MODEL)),
        'w_in_odd': dense(ks[9], (N_ODD, D_MODEL, ODD_IN), D_MODEL),
        'q_norm': gain(ks[10], (N_ODD, MLA_Q_RANK)),
        'w_q_up': dense(ks[11], (N_ODD, MLA_Q_RANK, MLA_HEADS * (MLA_NOPE_DIM + MLA_ROPE_DIM)), MLA_Q_RANK),
        'kv_norm': gain(ks[12], (N_ODD, MLA_KV_RANK)),
        'w_kv_up': dense(ks[13], (N_ODD, MLA_KV_RANK, MLA_HEADS * (MLA_NOPE_DIM + MLA_V_DIM)), MLA_KV_RANK),
        'w_out_odd': dense(ks[14], (N_ODD, MLA_WIDTH, D_MODEL), MLA_WIDTH),
        'final_norm': gain(ks[15], (D_MODEL,)),
    }


def reference(x, positions, ln_even, w_in_even, gla_alpha_up, gla_alpha_bias, gla_norm, w_out_even,
              ln_odd, w_in_odd, q_norm, w_q_up, kv_norm, w_kv_up, w_out_odd, final_norm):
    for layer in range(DEPTH):
        i = layer // 2
        if layer % 2 == 0:
            x = even_layer(x, ln_even[i], w_in_even[i], gla_alpha_up[i], gla_alpha_bias[i],
                           gla_norm[i], w_out_even[i])
        else:
            x = odd_layer(x, positions, ln_odd[i], w_in_odd[i], q_norm[i], w_q_up[i],
                          kv_norm[i], w_kv_up[i], w_out_odd[i])
    return rmsnorm(x, final_norm)
```

```python
import functools

import jax
import jax.numpy as jnp
from jax import lax
from jax.experimental import pallas as pl
from jax.experimental.pallas import tpu as pltpu

F32 = jnp.float32
BF16 = jnp.bfloat16

SB_HEADS = 8
SB_DIM = 128
GLA_HEADS = 4
GLA_DK = 128
GLA_DV = 256
GLA_RANK = 16
GLA_GATE_NORM = 16.0
GLA_CHUNK = 64
GLA_SUB = 16
MLA_HEADS = 16
MLA_QR = 512
MLA_KVR = 512
MLA_NOPE = 128
MLA_ROPE = 64
MLA_V = 128
MLA_QK_PAD = 256
ROPE_THETA = 10000.0
EPS = 1e-6

LANE = 128
VMEM_LIMIT = 48 * 1024 * 1024

SB_W = SB_HEADS * SB_DIM
GLA_KW = GLA_HEADS * GLA_DK
GLA_VW = GLA_HEADS * GLA_DV
EVEN_MAIN = 4 * SB_W + 2 * GLA_KW + 2 * GLA_VW


def _cparams(sem):
    return pltpu.CompilerParams(dimension_semantics=sem, vmem_limit_bytes=VMEM_LIMIT)


def _rms(x, g):
    return x * lax.rsqrt(jnp.mean(x * x, axis=-1, keepdims=True) + EPS) * g


def _silu(g):
    return g * (1.0 / (1.0 + jnp.exp(-g)))


def _dot(a, b):
    return jnp.dot(a, b, preferred_element_type=F32)


def _dot_nt(a, b):
    return lax.dot_general(a, b, (((1,), (1,)), ((), ())), preferred_element_type=F32)


def _dot_tn(a, b):
    return lax.dot_general(a, b, (((0,), (0,)), ((), ())), preferred_element_type=F32)


def _norm_matmul_kernel(x_ref, g_ref, w_ref, o_ref, h_ref):
    @pl.when(pl.program_id(1) == 0)
    def _():
        h_ref[...] = _rms(x_ref[...], g_ref[...]).astype(BF16)

    o_ref[...] = _dot(h_ref[...], w_ref[...]).astype(o_ref.dtype)


def _pick_tile(n, cap):
    best = LANE
    for t in range(LANE, cap + 1, LANE):
        if n % t == 0:
            best = t
    return best


def norm_matmul(x, g, w, out_dtype, tm=512, tn_cap=1024):
    T, K = x.shape
    N = w.shape[1]
    tn = _pick_tile(N, tn_cap)
    tm = min(tm, T)
    return pl.pallas_call(
        _norm_matmul_kernel,
        grid=(T // tm, N // tn),
        in_specs=[
            pl.BlockSpec((tm, K), lambda i, j: (i, 0)),
            pl.BlockSpec((1, K), lambda i, j: (0, 0)),
            pl.BlockSpec((K, tn), lambda i, j: (0, j)),
        ],
        out_specs=pl.BlockSpec((tm, tn), lambda i, j: (i, j)),
        out_shape=jax.ShapeDtypeStruct((T, N), out_dtype),
        scratch_shapes=[pltpu.VMEM((tm, K), BF16)],
        compiler_params=_cparams(("parallel", "arbitrary")),
        name="norm_matmul",
    )(x, g.reshape(1, K), w)


def _sb_kernel(q_ref, k_ref, v_ref, g_ref, o_ref, *, blk, scale):
    qi = pl.program_id(2)
    q = q_ref[...]
    row = lax.broadcasted_iota(jnp.int32, (blk, blk), 0)
    col = lax.broadcasted_iota(jnp.int32, (blk, blk), 1)
    later = (row > col).astype(BF16)
    causal = col < row

    def block(start, carry, mask):
        c, acc = carry
        kb = k_ref[pl.ds(start, blk), :]
        vb = v_ref[pl.ds(start, blk), :]
        z = _dot_nt(q, kb) * scale
        sp = jnp.maximum(z, 0.0) + jnp.log1p(jnp.exp(-jnp.abs(z)))
        log_stay = -sp
        if mask is not None:
            log_stay = jnp.where(mask, log_stay, 0.0)
        hi = log_stay.astype(BF16)
        lo = (log_stay - hi.astype(F32)).astype(BF16)
        within = _dot(hi, later) + _dot(lo, later)
        w = jnp.exp((z - sp) + within + c)
        if mask is not None:
            w = jnp.where(mask, w, 0.0)
        acc = acc + _dot(w.astype(BF16), vb)
        c = c + jnp.sum(log_stay, axis=-1, keepdims=True)
        return c, acc

    carry = (jnp.zeros((blk, 1), F32), jnp.zeros((blk, SB_DIM), F32))
    carry = block(pl.multiple_of(qi * blk, blk), carry, causal)

    def body(i, carry):
        return block(pl.multiple_of((qi - 1 - i) * blk, blk), carry, None)

    _, acc = lax.fori_loop(0, qi, body, carry)
    o_ref[...] = (acc * _silu(g_ref[...].astype(F32))).astype(o_ref.dtype)


def sb_attention(z, batch, seq, blk=256):
    T = z.shape[0]
    blk = min(blk, seq)
    nq = seq // blk
    hw = SB_W // SB_DIM
    return pl.pallas_call(
        functools.partial(_sb_kernel, blk=blk, scale=SB_DIM ** -0.5),
        grid=(batch, SB_HEADS, nq),
        in_specs=[
            pl.BlockSpec((blk, SB_DIM), lambda b, h, i: (b * nq + i, h)),
            pl.BlockSpec((seq, SB_DIM), lambda b, h, i: (b, hw + h)),
            pl.BlockSpec((seq, SB_DIM), lambda b, h, i: (b, 2 * hw + h)),
            pl.BlockSpec((blk, SB_DIM), lambda b, h, i: (b * nq + i, 3 * hw + h)),
        ],
        out_specs=pl.BlockSpec((blk, SB_DIM), lambda b, h, i: (b * nq + i, h)),
        out_shape=jax.ShapeDtypeStruct((T, SB_W), BF16),
        compiler_params=_cparams(("parallel", "parallel", "arbitrary")),
        name="sb_attention",
    )(z, z, z, z)


def _gla_kernel(q_ref, k_ref, v_ref, gg_ref, ga_ref, au_ref, ab_ref, ng_ref, o_ref, st_ref, *, nchunk):
    C, SUB = GLA_CHUNK, GLA_SUB

    @pl.when(pl.program_id(2) == 0)
    def _():
        st_ref[...] = jnp.zeros_like(st_ref)

    ri = lax.broadcasted_iota(jnp.int32, (C, C), 0)
    ci = lax.broadcasted_iota(jnp.int32, (C, C), 1)
    incl = (ci <= ri).astype(F32)
    sub_row = lax.broadcasted_iota(jnp.int32, (SUB, 1), 0)
    hp = lax.Precision.HIGHEST

    st = st_ref[...]
    for c in range(nchunk):
        r0 = c * C
        q = q_ref[r0:r0 + C, :].astype(F32) * (GLA_DK ** -0.5)
        k = k_ref[r0:r0 + C, :].astype(F32)
        v = v_ref[r0:r0 + C, :]
        vf = v.astype(F32)
        pre = jnp.dot(ga_ref[r0:r0 + C, :], au_ref[...], precision=hp, preferred_element_type=F32) + ab_ref[...]
        log_f = -(jnp.maximum(-pre, 0.0) + jnp.log1p(jnp.exp(-jnp.abs(pre)))) / GLA_GATE_NORM
        b = jnp.dot(incl, log_f, precision=hp, preferred_element_type=F32)
        b_last = b[C - 1:C, :]

        o_inter = _dot_nt((q * jnp.exp(b)).astype(BF16), st.astype(BF16))

        pieces = []
        for s in range(C // SUB):
            i0 = s * SUB
            bs = b[i0:i0 + SUB, :]
            qs = q[i0:i0 + SUB, :]
            o_s = o_inter[i0:i0 + SUB, :]
            if s > 0:
                b0 = b[i0:i0 + 1, :]
                q_dec = (qs * jnp.exp(bs - b0)).astype(BF16)
                k_dec = (k[0:i0, :] * jnp.exp(b0 - b[0:i0, :])).astype(BF16)
                att = _dot_nt(q_dec, k_dec)
                o_s = o_s + _dot(att.astype(BF16), v[0:i0, :])
            for j in range(SUB):
                jj = i0 + j
                dec = jnp.exp(jnp.minimum(bs - b[jj:jj + 1, :], 0.0))
                a = jnp.sum(qs * k[jj:jj + 1, :] * dec, axis=-1, keepdims=True)
                a = jnp.where(sub_row >= j, a, 0.0)
                o_s = o_s + a * vf[jj:jj + 1, :]
            pieces.append(o_s)
        o = jnp.concatenate(pieces, axis=0)

        k_dec = (k * jnp.exp(b_last - b)).astype(BF16)
        st = st * jnp.exp(b_last) + _dot_tn(v, k_dec)

        y = _rms(o, ng_ref[...]) * _silu(gg_ref[r0:r0 + C, :].astype(F32))
        o_ref[r0:r0 + C, :] = y.astype(o_ref.dtype)
    st_ref[...] = st


def gla(z, ga, au, ab, ng, batch, seq, tb=256):
    T = z.shape[0]
    tb = min(tb, seq)
    nb = seq // tb
    q0 = 4 * SB_W // GLA_DK
    k0 = q0 + GLA_HEADS
    v0 = (4 * SB_W + 2 * GLA_KW) // GLA_DV
    g0 = v0 + GLA_HEADS
    return pl.pallas_call(
        functools.partial(_gla_kernel, nchunk=tb // GLA_CHUNK),
        grid=(batch, GLA_HEADS, nb),
        in_specs=[
            pl.BlockSpec((tb, GLA_DK), lambda b, h, i: (b * nb + i, q0 + h)),
            pl.BlockSpec((tb, GLA_DK), lambda b, h, i: (b * nb + i, k0 + h)),
            pl.BlockSpec((tb, GLA_DV), lambda b, h, i: (b * nb + i, v0 + h)),
            pl.BlockSpec((tb, GLA_DV), lambda b, h, i: (b * nb + i, g0 + h)),
            pl.BlockSpec((tb, LANE), lambda b, h, i: (b * nb + i, 0)),
            pl.BlockSpec((LANE, GLA_DK), lambda b, h, i: (0, h)),
            pl.BlockSpec((1, GLA_DK), lambda b, h, i: (0, h)),
            pl.BlockSpec((1, GLA_DV), lambda b, h, i: (0, 0)),
        ],
        out_specs=pl.BlockSpec((tb, GLA_DV), lambda b, h, i: (b * nb + i, h)),
        out_shape=jax.ShapeDtypeStruct((T, GLA_VW), BF16),
        scratch_shapes=[pltpu.VMEM((GLA_DV, GLA_DK), F32)],
        compiler_params=_cparams(("parallel", "parallel", "arbitrary")),
        name="gla",
    )(z, z, z, z, ga, au, ab, ng)


def _out_proj_kernel(y1_ref, y2_ref, w1_ref, w2_ref, x_ref, gf_ref, o_ref, *, final):
    acc = x_ref[...] + _dot(y1_ref[...], w1_ref[...]) + _dot(y2_ref[...], w2_ref[...])
    if final:
        acc = _rms(acc, gf_ref[...])
    o_ref[...] = acc


def out_proj(y1, c1, y2, c2, w, x, gf, final, tm=256):
    T, D = x.shape
    kh = w.shape[0] // 2
    tm = min(tm, T)
    return pl.pallas_call(
        functools.partial(_out_proj_kernel, final=final),
        grid=(T // tm,),
        in_specs=[
            pl.BlockSpec((tm, kh), lambda i: (i, c1)),
            pl.BlockSpec((tm, kh), lambda i: (i, c2)),
            pl.BlockSpec((kh, D), lambda i: (0, 0)),
            pl.BlockSpec((kh, D), lambda i: (1, 0)),
            pl.BlockSpec((tm, D), lambda i: (i, 0)),
            pl.BlockSpec((1, D), lambda i: (0, 0)),
        ],
        out_specs=pl.BlockSpec((tm, D), lambda i: (i, 0)),
        out_shape=jax.ShapeDtypeStruct((T, D), F32),
        compiler_params=_cparams(("parallel",)),
        name="out_proj",
    )(y1, y2, w, w, x, gf.reshape(1, D))


def _mla_up_kernel(lat_ref, pos_ref, inv_ref, qg_ref, kvg_ref, wq_ref, wkv_ref, q_ref, k_ref, v_ref, *, scale):
    hq = _rms(lat_ref[:, 0:MLA_QR], qg_ref[...]).astype(BF16)
    hkv = _rms(lat_ref[:, MLA_QR:MLA_QR + MLA_KVR], kvg_ref[...]).astype(BF16)
    ang = pos_ref[...].astype(F32) * inv_ref[...]
    lane = lax.broadcasted_iota(jnp.int32, ang.shape, 1)
    half = MLA_ROPE // 2
    cos2 = jnp.where(lane < MLA_ROPE, jnp.cos(ang), 0.0)
    sin2 = jnp.where(lane < half, -jnp.sin(ang), jnp.where(lane < MLA_ROPE, jnp.sin(ang), 0.0))

    def rope(r):
        return r * cos2 + pltpu.roll(r, MLA_ROPE, 1) * sin2

    kr = rope(lat_ref[:, MLA_QR + MLA_KVR:MLA_QR + MLA_KVR + LANE]).astype(BF16)
    kn = _dot(hkv, wkv_ref[:, 0:MLA_HEADS * MLA_NOPE]).astype(BF16)
    v_ref[...] = _dot(hkv, wkv_ref[:, MLA_HEADS * MLA_NOPE:]).astype(BF16)
    for h in range(MLA_HEADS):
        c0 = h * MLA_QK_PAD
        y = _dot(hq, wq_ref[:, c0:c0 + MLA_QK_PAD])
        q_ref[:, c0:c0 + MLA_NOPE] = (y[:, 0:MLA_NOPE] * scale).astype(BF16)
        q_ref[:, c0 + MLA_NOPE:c0 + MLA_QK_PAD] = (rope(y[:, MLA_NOPE:]) * scale).astype(BF16)
        k_ref[:, c0:c0 + MLA_NOPE] = kn[:, h * MLA_NOPE:(h + 1) * MLA_NOPE]
        k_ref[:, c0 + MLA_NOPE:c0 + MLA_QK_PAD] = kr


def mla_up(lat, pos, inv, qg, kvg, wq, wkv, tm=256):
    T = lat.shape[0]
    tm = min(tm, T)
    qk_w = MLA_HEADS * MLA_QK_PAD
    v_w = MLA_HEADS * MLA_V
    return pl.pallas_call(
        functools.partial(_mla_up_kernel, scale=(MLA_NOPE + MLA_ROPE) ** -0.5),
        grid=(T // tm,),
        in_specs=[
            pl.BlockSpec((tm, lat.shape[1]), lambda i: (i, 0)),
            pl.BlockSpec((tm, 1), lambda i: (i, 0)),
            pl.BlockSpec((1, LANE), lambda i: (0, 0)),
            pl.BlockSpec((1, MLA_QR), lambda i: (0, 0)),
            pl.BlockSpec((1, MLA_KVR), lambda i: (0, 0)),
            pl.BlockSpec(wq.shape, lambda i: (0, 0)),
            pl.BlockSpec(wkv.shape, lambda i: (0, 0)),
        ],
        out_specs=[
            pl.BlockSpec((tm, qk_w), lambda i: (i, 0)),
            pl.BlockSpec((tm, qk_w), lambda i: (i, 0)),
            pl.BlockSpec((tm, v_w), lambda i: (i, 0)),
        ],
        out_shape=[
            jax.ShapeDtypeStruct((T, qk_w), BF16),
            jax.ShapeDtypeStruct((T, qk_w), BF16),
            jax.ShapeDtypeStruct((T, v_w), BF16),
        ],
        compiler_params=_cparams(("parallel",)),
        name="mla_up",
    )(lat, pos, inv, qg.reshape(1, -1), kvg.reshape(1, -1), wq, wkv)


def _mla_attn_kernel(q_ref, k_ref, v_ref, g_ref, o_ref, *, blk):
    qi = pl.program_id(2)
    q = q_ref[...]
    row = lax.broadcasted_iota(jnp.int32, (blk, blk), 0)
    col = lax.broadcasted_iota(jnp.int32, (blk, blk), 1)
    causal = col <= row

    def block(start, carry, mask):
        m, l, acc = carry
        s = _dot_nt(q, k_ref[pl.ds(start, blk), :])
        if mask is not None:
            s = jnp.where(mask, s, -jnp.inf)
        m_new = jnp.maximum(m, jnp.max(s, axis=-1, keepdims=True))
        alpha = jnp.exp(m - m_new)
        p = jnp.exp(s - m_new)
        l = alpha * l + jnp.sum(p, axis=-1, keepdims=True)
        acc = alpha * acc + _dot(p.astype(BF16), v_ref[pl.ds(start, blk), :])
        return m_new, l, acc

    def body(j, carry):
        return block(pl.multiple_of(j * blk, blk), carry, None)

    carry = (jnp.full((blk, 1), -jnp.inf, F32), jnp.zeros((blk, 1), F32), jnp.zeros((blk, MLA_V), F32))
    carry = lax.fori_loop(0, qi, body, carry)
    _, l, acc = block(pl.multiple_of(qi * blk, blk), carry, causal)
    o_ref[...] = (acc / l * _silu(g_ref[...].astype(F32))).astype(o_ref.dtype)


def mla_attention(qf, kf, v, gate, batch, seq, blk=256):
    T = qf.shape[0]
    blk = min(blk, seq)
    nq = seq // blk
    return pl.pallas_call(
        functools.partial(_mla_attn_kernel, blk=blk),
        grid=(batch, MLA_HEADS, nq),
        in_specs=[
            pl.BlockSpec((blk, MLA_QK_PAD), lambda b, h, i: (b * nq + i, h)),
            pl.BlockSpec((seq, MLA_QK_PAD), lambda b, h, i: (b, h)),
            pl.BlockSpec((seq, MLA_V), lambda b, h, i: (b, h)),
            pl.BlockSpec((blk, MLA_V), lambda b, h, i: (b * nq + i, h)),
        ],
        out_specs=pl.BlockSpec((blk, MLA_V), lambda b, h, i: (b * nq + i, h)),
        out_shape=jax.ShapeDtypeStruct((T, MLA_HEADS * MLA_V), BF16),
        compiler_params=_cparams(("parallel", "parallel", "arbitrary")),
        name="mla_attention",
    )(qf, kf, v, gate)


def _swap_halves(w):
    half = w.shape[-1] // 2
    return jnp.concatenate([w[..., half:], w[..., :half]], axis=-1)


def _prep_even(w_in, alpha_up, alpha_bias):
    w_main = w_in[:, :EVEN_MAIN].astype(BF16)
    w_ga = jnp.pad(w_in[:, EVEN_MAIN:], ((0, 0), (0, LANE - GLA_RANK))).astype(BF16)
    au = jnp.pad(alpha_up, ((0, LANE - GLA_RANK), (0, 0)))
    return w_main, w_ga, au, alpha_bias.reshape(1, -1)


def _prep_odd(w_in, w_q_up, w_kv_up):
    d = w_in.shape[0]
    o1 = MLA_QR + MLA_KVR
    w_kr = w_in[:, o1:o1 + MLA_ROPE]
    w_lat = jnp.concatenate([w_in[:, :o1], w_kr, _swap_halves(w_kr)], axis=1).astype(BF16)
    w_gate = w_in[:, o1 + MLA_ROPE:].astype(BF16)
    wq = w_q_up.reshape(MLA_QR, MLA_HEADS, MLA_NOPE + MLA_ROPE)
    wq_r = wq[..., MLA_NOPE:]
    wq = jnp.concatenate([wq[..., :MLA_NOPE], wq_r, _swap_halves(wq_r)], axis=-1)
    wq = wq.reshape(MLA_QR, MLA_HEADS * MLA_QK_PAD).astype(BF16)
    wkv = w_kv_up.reshape(MLA_KVR, MLA_HEADS, MLA_NOPE + MLA_V)
    wkv = jnp.concatenate([wkv[..., :MLA_NOPE].reshape(MLA_KVR, -1), wkv[..., MLA_NOPE:].reshape(MLA_KVR, -1)], axis=1)
    del d
    return w_lat, w_gate, wq, wkv.astype(BF16)


def _even_layer(x, batch, seq, norm_g, w_in, alpha_up, alpha_bias, gla_norm_g, w_out, final_g, final):
    w_main, w_ga, au, ab = _prep_even(w_in, alpha_up, alpha_bias)
    z = norm_matmul(x, norm_g, w_main, BF16)
    ga = norm_matmul(x, norm_g, w_ga, F32)
    y_a = sb_attention(z, batch, seq)
    y_b = gla(z, ga, au, ab, gla_norm_g.reshape(1, -1), batch, seq)
    return out_proj(y_a, 0, y_b, 0, w_out.astype(BF16), x, final_g, final)


def _odd_layer(x, pos, inv, batch, seq, norm_g, w_in, q_norm_g, w_q_up, kv_norm_g, w_kv_up, w_out, final_g, final):
    w_lat, w_gate, wq, wkv = _prep_odd(w_in, w_q_up, w_kv_up)
    lat = norm_matmul(x, norm_g, w_lat, F32)
    gate = norm_matmul(x, norm_g, w_gate, BF16)
    qf, kf, v = mla_up(lat, pos, inv, q_norm_g, kv_norm_g, wq, wkv)
    y = mla_attention(qf, kf, v, gate, batch, seq)
    return out_proj(y, 0, y, 1, w_out.astype(BF16), x, final_g, final)


def kernel(x, positions, ln_even, w_in_even, gla_alpha_up, gla_alpha_bias, gla_norm, w_out_even,
           ln_odd, w_in_odd, q_norm, w_q_up, kv_norm, w_kv_up, w_out_odd, final_norm):
    batch, seq, d = x.shape
    depth = ln_even.shape[0] + ln_odd.shape[0]
    h = x.reshape(batch * seq, d)
    pos = positions.reshape(batch * seq, 1)
    half = MLA_ROPE // 2
    inv = ROPE_THETA ** (-jnp.arange(half, dtype=F32) / half)
    inv = jnp.concatenate([inv, inv, jnp.zeros((LANE - MLA_ROPE,), F32)]).reshape(1, LANE)
    for layer in range(depth):
        i = layer // 2
        final = layer == depth - 1
        if layer % 2 == 0:
            h = _even_layer(h, batch, seq, ln_even[i], w_in_even[i], gla_alpha_up[i], gla_alpha_bias[i],
                            gla_norm[i], w_out_even[i], final_norm, final)
        else:
            h = _odd_layer(h, pos, inv, batch, seq, ln_odd[i], w_in_odd[i], q_norm[i], w_q_up[i],
                           kv_norm[i], w_kv_up[i], w_out_odd[i], final_norm, final)
    return h.reshape(batch, seq, d)
```

```python
import functools

import jax
import jax.numpy as jnp
from jax import lax
from jax.experimental import pallas as pl
from jax.experimental.pallas import tpu as pltpu

F32 = jnp.float32
BF16 = jnp.bfloat16

SB_HEADS = 8
SB_DIM = 128
GLA_HEADS = 4
GLA_DK = 128
GLA_DV = 256
GLA_RANK = 16
GLA_GATE_NORM = 16.0
GLA_CHUNK = 64
GLA_SUB = 16
MLA_HEADS = 16
MLA_QR = 512
MLA_KVR = 512
MLA_NOPE = 128
MLA_ROPE = 64
MLA_V = 128
MLA_QK_PAD = 256
ROPE_THETA = 10000.0
EPS = 1e-6
LOG2E = 1.4426950408889634

LANE = 128
VMEM_LIMIT = 48 * 1024 * 1024

SB_W = SB_HEADS * SB_DIM
GLA_KW = GLA_HEADS * GLA_DK
GLA_VW = GLA_HEADS * GLA_DV
EVEN_MAIN = 4 * SB_W + 2 * GLA_KW + 2 * GLA_VW


def _cparams(sem):
    return pltpu.CompilerParams(dimension_semantics=sem, vmem_limit_bytes=VMEM_LIMIT)


def _rms(x, g):
    return x * lax.rsqrt(jnp.mean(x * x, axis=-1, keepdims=True) + EPS) * g


def _silu(g):
    return g * (1.0 / (1.0 + jnp.exp(-g)))


def _dot(a, b):
    return jnp.dot(a, b, preferred_element_type=F32)


def _dot_nt(a, b):
    return lax.dot_general(a, b, (((1,), (1,)), ((), ())), preferred_element_type=F32)


def _dot_tn(a, b):
    return lax.dot_general(a, b, (((0,), (0,)), ((), ())), preferred_element_type=F32)


def _norm_matmul_kernel(x_ref, g_ref, w_ref, o_ref, h_ref):
    @pl.when(pl.program_id(1) == 0)
    def _():
        h_ref[...] = _rms(x_ref[...], g_ref[...]).astype(BF16)

    o_ref[...] = _dot(h_ref[...], w_ref[...]).astype(o_ref.dtype)


def _pick_tile(n, cap):
    best = LANE
    for t in range(LANE, cap + 1, LANE):
        if n % t == 0:
            best = t
    return best


def norm_matmul(x, g, w, out_dtype, tm=512, tn_cap=1024):
    T, K = x.shape
    N = w.shape[1]
    tn = _pick_tile(N, tn_cap)
    tm = min(tm, T)
    return pl.pallas_call(
        _norm_matmul_kernel,
        grid=(T // tm, N // tn),
        in_specs=[
            pl.BlockSpec((tm, K), lambda i, j: (i, 0)),
            pl.BlockSpec((1, K), lambda i, j: (0, 0)),
            pl.BlockSpec((K, tn), lambda i, j: (0, j)),
        ],
        out_specs=pl.BlockSpec((tm, tn), lambda i, j: (i, j)),
        out_shape=jax.ShapeDtypeStruct((T, N), out_dtype),
        scratch_shapes=[pltpu.VMEM((tm, K), BF16)],
        compiler_params=_cparams(("parallel", "arbitrary")),
        name="norm_matmul",
    )(x, g.reshape(1, K), w)


def _sb_kernel(q_ref, k_ref, v_ref, g_ref, o_ref, c_ref, acc_ref, *, tq, tk):
    qi = pl.program_id(2)
    row = lax.broadcasted_iota(jnp.int32, (tk, tk), 0)
    col = lax.broadcasted_iota(jnp.int32, (tk, tk), 1)
    later = (row > col).astype(BF16)
    c_ref[...] = jnp.zeros_like(c_ref)
    acc_ref[...] = jnp.zeros_like(acc_ref)

    def block(start, r0, masked):
        nz = _dot_nt(q_ref[r0:, :], k_ref[pl.ds(start, tk), :])
        neg_abs = lax.bitcast_convert_type(lax.bitcast_convert_type(nz, jnp.uint32) | jnp.uint32(0x80000000), F32)
        log_stay = jnp.minimum(nz, 0.0) - jnp.log(1.0 + jnp.exp2(neg_abs)) * LOG2E
        log_beta = log_stay - nz
        if masked:
            keep = lax.broadcasted_iota(jnp.int32, nz.shape, 1) < lax.broadcasted_iota(jnp.int32, nz.shape, 0)
            log_stay = jnp.where(keep, log_stay, 0.0)
        hi = log_stay.astype(BF16)
        lo = (log_stay - hi.astype(F32)).astype(BF16)
        within = _dot(hi, later) + _dot(lo, later)
        w = jnp.exp2(log_beta + within + c_ref[r0:, :])
        if masked:
            w = jnp.where(keep, w, 0.0)
        acc_ref[r0:, :] += _dot(w.astype(BF16), v_ref[pl.ds(start, tk), :])
        c_ref[r0:, :] += jnp.sum(log_stay, axis=-1, keepdims=True)

    nd = tq // tk
    base = qi * tq
    for m in reversed(range(nd)):
        block(pl.multiple_of(base + m * tk, tk), m * tk, True)

    def body(i, carry):
        block(pl.multiple_of(base - (i + 1) * tk, tk), 0, False)
        return carry

    lax.fori_loop(0, qi * nd, body, 0)
    o_ref[...] = (acc_ref[...] * _silu(g_ref[...].astype(F32))).astype(o_ref.dtype)


def sb_attention(z, batch, seq, tq=1024, tk=256):
    T = z.shape[0]
    tq = min(tq, seq)
    tk = min(tk, tq)
    nq = seq // tq
    hw = SB_W // SB_DIM
    return pl.pallas_call(
        functools.partial(_sb_kernel, tq=tq, tk=tk),
        grid=(batch, SB_HEADS, nq),
        in_specs=[
            pl.BlockSpec((tq, SB_DIM), lambda b, h, i: (b * nq + i, h)),
            pl.BlockSpec((seq, SB_DIM), lambda b, h, i: (b, hw + h)),
            pl.BlockSpec((seq, SB_DIM), lambda b, h, i: (b, 2 * hw + h)),
            pl.BlockSpec((tq, SB_DIM), lambda b, h, i: (b * nq + i, 3 * hw + h)),
        ],
        out_specs=pl.BlockSpec((tq, SB_DIM), lambda b, h, i: (b * nq + i, h)),
        out_shape=jax.ShapeDtypeStruct((T, SB_W), BF16),
        scratch_shapes=[pltpu.VMEM((tq, 1), F32), pltpu.VMEM((tq, SB_DIM), F32)],
        compiler_params=_cparams(("parallel", "parallel", "arbitrary")),
        name="sb_attention",
    )(z, z, z, z)


def _gla_kernel(q_ref, k_ref, v_ref, gg_ref, ga_ref, au_ref, ab_ref, ng_ref, o_ref, st_ref, *, nchunk):
    C, SUB = GLA_CHUNK, GLA_SUB

    @pl.when(pl.program_id(2) == 0)
    def _():
        st_ref[...] = jnp.zeros_like(st_ref)

    ri = lax.broadcasted_iota(jnp.int32, (C, C), 0)
    ci = lax.broadcasted_iota(jnp.int32, (C, C), 1)
    incl = (ci <= ri).astype(F32)
    sub_row = lax.broadcasted_iota(jnp.int32, (SUB, 1), 0)
    hp = lax.Precision.HIGHEST

    st = st_ref[...]
    for c in range(nchunk):
        r0 = c * C
        q = q_ref[r0:r0 + C, :].astype(F32) * (GLA_DK ** -0.5)
        k = k_ref[r0:r0 + C, :].astype(F32)
        v = v_ref[r0:r0 + C, :]
        vf = v.astype(F32)
        pre = jnp.dot(ga_ref[r0:r0 + C, :], au_ref[...], precision=hp, preferred_element_type=F32) + ab_ref[...]
        log_f = -(jnp.maximum(-pre, 0.0) + jnp.log1p(jnp.exp(-jnp.abs(pre)))) / GLA_GATE_NORM
        b = jnp.dot(incl, log_f, precision=hp, preferred_element_type=F32)
        b_last = b[C - 1:C, :]

        o_inter = _dot_nt((q * jnp.exp(b)).astype(BF16), st.astype(BF16))

        pieces = []
        for s in range(C // SUB):
            i0 = s * SUB
            bs = b[i0:i0 + SUB, :]
            qs = q[i0:i0 + SUB, :]
            o_s = o_inter[i0:i0 + SUB, :]
            if s > 0:
                b0 = b[i0:i0 + 1, :]
                q_dec = (qs * jnp.exp(bs - b0)).astype(BF16)
                k_dec = (k[0:i0, :] * jnp.exp(b0 - b[0:i0, :])).astype(BF16)
                att = _dot_nt(q_dec, k_dec)
                o_s = o_s + _dot(att.astype(BF16), v[0:i0, :])
            for j in range(SUB):
                jj = i0 + j
                dec = jnp.exp(jnp.minimum(bs - b[jj:jj + 1, :], 0.0))
                a = jnp.sum(qs * k[jj:jj + 1, :] * dec, axis=-1, keepdims=True)
                a = jnp.where(sub_row >= j, a, 0.0)
                o_s = o_s + a * vf[jj:jj + 1, :]
            pieces.append(o_s)
        o = jnp.concatenate(pieces, axis=0)

        k_dec = (k * jnp.exp(b_last - b)).astype(BF16)
        st = st * jnp.exp(b_last) + _dot_tn(v, k_dec)

        y = _rms(o, ng_ref[...]) * _silu(gg_ref[r0:r0 + C, :].astype(F32))
        o_ref[r0:r0 + C, :] = y.astype(o_ref.dtype)
    st_ref[...] = st


def gla(z, ga, au, ab, ng, batch, seq, tb=256):
    T = z.shape[0]
    tb = min(tb, seq)
    nb = seq // tb
    q0 = 4 * SB_W // GLA_DK
    k0 = q0 + GLA_HEADS
    v0 = (4 * SB_W + 2 * GLA_KW) // GLA_DV
    g0 = v0 + GLA_HEADS
    return pl.pallas_call(
        functools.partial(_gla_kernel, nchunk=tb // GLA_CHUNK),
        grid=(batch, GLA_HEADS, nb),
        in_specs=[
            pl.BlockSpec((tb, GLA_DK), lambda b, h, i: (b * nb + i, q0 + h)),
            pl.BlockSpec((tb, GLA_DK), lambda b, h, i: (b * nb + i, k0 + h)),
            pl.BlockSpec((tb, GLA_DV), lambda b, h, i: (b * nb + i, v0 + h)),
            pl.BlockSpec((tb, GLA_DV), lambda b, h, i: (b * nb + i, g0 + h)),
            pl.BlockSpec((tb, LANE), lambda b, h, i: (b * nb + i, 0)),
            pl.BlockSpec((LANE, GLA_DK), lambda b, h, i: (0, h)),
            pl.BlockSpec((1, GLA_DK), lambda b, h, i: (0, h)),
            pl.BlockSpec((1, GLA_DV), lambda b, h, i: (0, 0)),
        ],
        out_specs=pl.BlockSpec((tb, GLA_DV), lambda b, h, i: (b * nb + i, h)),
        out_shape=jax.ShapeDtypeStruct((T, GLA_VW), BF16),
        scratch_shapes=[pltpu.VMEM((GLA_DV, GLA_DK), F32)],
        compiler_params=_cparams(("parallel", "parallel", "arbitrary")),
        name="gla",
    )(z, z, z, z, ga, au, ab, ng)


def _out_proj_kernel(y1_ref, y2_ref, w1_ref, w2_ref, x_ref, gf_ref, o_ref, *, final):
    acc = x_ref[...] + _dot(y1_ref[...], w1_ref[...]) + _dot(y2_ref[...], w2_ref[...])
    if final:
        acc = _rms(acc, gf_ref[...])
    o_ref[...] = acc


def out_proj(y1, c1, y2, c2, w, x, gf, final, tm=256):
    T, D = x.shape
    kh = w.shape[0] // 2
    tm = min(tm, T)
    return pl.pallas_call(
        functools.partial(_out_proj_kernel, final=final),
        grid=(T // tm,),
        in_specs=[
            pl.BlockSpec((tm, kh), lambda i: (i, c1)),
            pl.BlockSpec((tm, kh), lambda i: (i, c2)),
            pl.BlockSpec((kh, D), lambda i: (0, 0)),
            pl.BlockSpec((kh, D), lambda i: (1, 0)),
            pl.BlockSpec((tm, D), lambda i: (i, 0)),
            pl.BlockSpec((1, D), lambda i: (0, 0)),
        ],
        out_specs=pl.BlockSpec((tm, D), lambda i: (i, 0)),
        out_shape=jax.ShapeDtypeStruct((T, D), F32),
        compiler_params=_cparams(("parallel",)),
        name="out_proj",
    )(y1, y2, w, w, x, gf.reshape(1, D))


def _mla_up_kernel(lat_ref, pos_ref, inv_ref, qg_ref, kvg_ref, wq_ref, wkv_ref, q_ref, k_ref, v_ref, *, scale):
    hq = _rms(lat_ref[:, 0:MLA_QR], qg_ref[...]).astype(BF16)
    hkv = _rms(lat_ref[:, MLA_QR:MLA_QR + MLA_KVR], kvg_ref[...]).astype(BF16)
    ang = pos_ref[...].astype(F32) * inv_ref[...]
    lane = lax.broadcasted_iota(jnp.int32, ang.shape, 1)
    half = MLA_ROPE // 2
    cos2 = jnp.where(lane < MLA_ROPE, jnp.cos(ang), 0.0)
    sin2 = jnp.where(lane < half, -jnp.sin(ang), jnp.where(lane < MLA_ROPE, jnp.sin(ang), 0.0))

    def rope(r):
        return r * cos2 + pltpu.roll(r, MLA_ROPE, 1) * sin2

    kr = rope(lat_ref[:, MLA_QR + MLA_KVR:MLA_QR + MLA_KVR + LANE]).astype(BF16)
    kn = _dot(hkv, wkv_ref[:, 0:MLA_HEADS * MLA_NOPE]).astype(BF16)
    v_ref[...] = _dot(hkv, wkv_ref[:, MLA_HEADS * MLA_NOPE:]).astype(BF16)
    for h in range(MLA_HEADS):
        c0 = h * MLA_QK_PAD
        y = _dot(hq, wq_ref[:, c0:c0 + MLA_QK_PAD])
        q_ref[:, c0:c0 + MLA_NOPE] = (y[:, 0:MLA_NOPE] * scale).astype(BF16)
        q_ref[:, c0 + MLA_NOPE:c0 + MLA_QK_PAD] = (rope(y[:, MLA_NOPE:]) * scale).astype(BF16)
        k_ref[:, c0:c0 + MLA_NOPE] = kn[:, h * MLA_NOPE:(h + 1) * MLA_NOPE]
        k_ref[:, c0 + MLA_NOPE:c0 + MLA_QK_PAD] = kr


def mla_up(lat, pos, inv, qg, kvg, wq, wkv, tm=256):
    T = lat.shape[0]
    tm = min(tm, T)
    qk_w = MLA_HEADS * MLA_QK_PAD
    v_w = MLA_HEADS * MLA_V
    return pl.pallas_call(
        functools.partial(_mla_up_kernel, scale=(MLA_NOPE + MLA_ROPE) ** -0.5 * LOG2E),
        grid=(T // tm,),
        in_specs=[
            pl.BlockSpec((tm, lat.shape[1]), lambda i: (i, 0)),
            pl.BlockSpec((tm, 1), lambda i: (i, 0)),
            pl.BlockSpec((1, LANE), lambda i: (0, 0)),
            pl.BlockSpec((1, MLA_QR), lambda i: (0, 0)),
            pl.BlockSpec((1, MLA_KVR), lambda i: (0, 0)),
            pl.BlockSpec(wq.shape, lambda i: (0, 0)),
            pl.BlockSpec(wkv.shape, lambda i: (0, 0)),
        ],
        out_specs=[
            pl.BlockSpec((tm, qk_w), lambda i: (i, 0)),
            pl.BlockSpec((tm, qk_w), lambda i: (i, 0)),
            pl.BlockSpec((tm, v_w), lambda i: (i, 0)),
        ],
        out_shape=[
            jax.ShapeDtypeStruct((T, qk_w), BF16),
            jax.ShapeDtypeStruct((T, qk_w), BF16),
            jax.ShapeDtypeStruct((T, v_w), BF16),
        ],
        compiler_params=_cparams(("parallel",)),
        name="mla_up",
    )(lat, pos, inv, qg.reshape(1, -1), kvg.reshape(1, -1), wq, wkv)


def _mla_attn_kernel(q_ref, k_ref, v_ref, g_ref, o_ref, s0_ref, s1_ref, m_ref, l_ref, acc_ref, *, tq, tk, rt):
    qi = pl.program_id(2)
    nd = tq // tk
    nf = qi * nd
    base = qi * tq
    reps = tk // LANE
    s_refs = (s0_ref, s1_ref)
    m_ref[...] = jnp.full_like(m_ref, -jnp.inf)
    l_ref[...] = jnp.zeros_like(l_ref)
    acc_ref[...] = jnp.zeros_like(acc_ref)

    def scores(start, r0, slot):
        s_refs[slot][r0:, :] = _dot_nt(q_ref[r0:, :], k_ref[pl.ds(start, tk), :])

    def update(start, r0, slot, masked):
        vb = v_ref[pl.ds(start, tk), :]
        for t0 in range(r0, tq, rt):
            rows = slice(t0, t0 + rt)
            s = s_refs[slot][rows, :]
            if masked and t0 - r0 < tk:
                col = lax.broadcasted_iota(jnp.int32, s.shape, 1)
                row = lax.broadcasted_iota(jnp.int32, s.shape, 0)
                s = jnp.where(col <= row + (t0 - r0), s, -jnp.inf)
            m_old = m_ref[rows, :]
            m_new = jnp.maximum(m_old, jnp.max(s, axis=-1, keepdims=True))
            alpha = jnp.exp2(m_old - m_new)
            p = jnp.exp2(s - jnp.tile(m_new, (1, reps)))
            l_ref[rows, :] = alpha * l_ref[rows, :] + jnp.sum(p, axis=-1, keepdims=True)
            acc_ref[rows, :] = alpha * acc_ref[rows, :] + _dot(p.astype(BF16), vb)
            m_ref[rows, :] = m_new

    scores(0, 0, 0)

    def body(i, carry):
        for u in range(nd):
            j = i * nd + u
            scores(pl.multiple_of((j + 1) * tk, tk), 0, (u + 1) % 2)
            update(pl.multiple_of(j * tk, tk), 0, u % 2, False)
        return carry

    lax.fori_loop(0, qi, body, 0)
    for m in range(nd):
        if m + 1 < nd:
            scores(pl.multiple_of(base + (m + 1) * tk, tk), (m + 1) * tk, (m + 1) % 2)
        update(pl.multiple_of(base + m * tk, tk), m * tk, m % 2, True)
    o_ref[...] = (acc_ref[...] / l_ref[...] * _silu(g_ref[...].astype(F32))).astype(o_ref.dtype)


def mla_attention(qf, kf, v, gate, batch, seq, tq=1024, tk=512, rt=256):
    T = qf.shape[0]
    tq = min(tq, seq)
    tk = min(tk, tq)
    rt = min(rt, tk)
    nq = seq // tq
    assert (tq // tk) % 2 == 0 or nq == 1, "score-buffer parity is static only for an even block count per tile"
    return pl.pallas_call(
        functools.partial(_mla_attn_kernel, tq=tq, tk=tk, rt=rt),
        grid=(batch, MLA_HEADS, nq),
        in_specs=[
            pl.BlockSpec((tq, MLA_QK_PAD), lambda b, h, i: (b * nq + i, h)),
            pl.BlockSpec((seq, MLA_QK_PAD), lambda b, h, i: (b, h)),
            pl.BlockSpec((seq, MLA_V), lambda b, h, i: (b, h)),
            pl.BlockSpec((tq, MLA_V), lambda b, h, i: (b * nq + i, h)),
        ],
        out_specs=pl.BlockSpec((tq, MLA_V), lambda b, h, i: (b * nq + i, h)),
        out_shape=jax.ShapeDtypeStruct((T, MLA_HEADS * MLA_V), BF16),
        scratch_shapes=[pltpu.VMEM((tq, tk), F32), pltpu.VMEM((tq, tk), F32), pltpu.VMEM((tq, LANE), F32),
                        pltpu.VMEM((tq, LANE), F32), pltpu.VMEM((tq, MLA_V), F32)],
        compiler_params=_cparams(("parallel", "parallel", "arbitrary")),
        name="mla_attention",
    )(qf, kf, v, gate)


def _swap_halves(w):
    half = w.shape[-1] // 2
    return jnp.concatenate([w[..., half:], w[..., :half]], axis=-1)


def _prep_even(w_in, alpha_up, alpha_bias):
    col_scale = jnp.concatenate([jnp.full((SB_W,), -(SB_DIM ** -0.5) * LOG2E, F32), jnp.ones((EVEN_MAIN - SB_W,), F32)])
    w_main = (w_in[:, :EVEN_MAIN] * col_scale).astype(BF16)
    w_ga = jnp.pad(w_in[:, EVEN_MAIN:], ((0, 0), (0, LANE - GLA_RANK))).astype(BF16)
    au = jnp.pad(alpha_up, ((0, LANE - GLA_RANK), (0, 0)))
    return w_main, w_ga, au, alpha_bias.reshape(1, -1)


def _prep_odd(w_in, w_q_up, w_kv_up):
    d = w_in.shape[0]
    o1 = MLA_QR + MLA_KVR
    w_kr = w_in[:, o1:o1 + MLA_ROPE]
    w_lat = jnp.concatenate([w_in[:, :o1], w_kr, _swap_halves(w_kr)], axis=1).astype(BF16)
    w_gate = w_in[:, o1 + MLA_ROPE:].astype(BF16)
    wq = w_q_up.reshape(MLA_QR, MLA_HEADS, MLA_NOPE + MLA_ROPE)
    wq_r = wq[..., MLA_NOPE:]
    wq = jnp.concatenate([wq[..., :MLA_NOPE], wq_r, _swap_halves(wq_r)], axis=-1)
    wq = wq.reshape(MLA_QR, MLA_HEADS * MLA_QK_PAD).astype(BF16)
    wkv = w_kv_up.reshape(MLA_KVR, MLA_HEADS, MLA_NOPE + MLA_V)
    wkv = jnp.concatenate([wkv[..., :MLA_NOPE].reshape(MLA_KVR, -1), wkv[..., MLA_NOPE:].reshape(MLA_KVR, -1)], axis=1)
    del d
    return w_lat, w_gate, wq, wkv.astype(BF16)


def _even_layer(x, batch, seq, norm_g, w_in, alpha_up, alpha_bias, gla_norm_g, w_out, final_g, final):
    w_main, w_ga, au, ab = _prep_even(w_in, alpha_up, alpha_bias)
    z = norm_matmul(x, norm_g, w_main, BF16)
    ga = norm_matmul(x, norm_g, w_ga, F32)
    y_a = sb_attention(z, batch, seq)
    y_b = gla(z, ga, au, ab, gla_norm_g.reshape(1, -1), batch, seq)
    return out_proj(y_a, 0, y_b, 0, w_out.astype(BF16), x, final_g, final)


def _odd_layer(x, pos, inv, batch, seq, norm_g, w_in, q_norm_g, w_q_up, kv_norm_g, w_kv_up, w_out, final_g, final):
    w_lat, w_gate, wq, wkv = _prep_odd(w_in, w_q_up, w_kv_up)
    lat = norm_matmul(x, norm_g, w_lat, F32)
    gate = norm_matmul(x, norm_g, w_gate, BF16)
    qf, kf, v = mla_up(lat, pos, inv, q_norm_g, kv_norm_g, wq, wkv)
    y = mla_attention(qf, kf, v, gate, batch, seq)
    return out_proj(y, 0, y, 1, w_out.astype(BF16), x, final_g, final)


def kernel(x, positions, ln_even, w_in_even, gla_alpha_up, gla_alpha_bias, gla_norm, w_out_even,
           ln_odd, w_in_odd, q_norm, w_q_up, kv_norm, w_kv_up, w_out_odd, final_norm):
    batch, seq, d = x.shape
    depth = ln_even.shape[0] + ln_odd.shape[0]
    h = x.reshape(batch * seq, d)
    pos = positions.reshape(batch * seq, 1)
    half = MLA_ROPE // 2
    inv = ROPE_THETA ** (-jnp.arange(half, dtype=F32) / half)
    inv = jnp.concatenate([inv, inv, jnp.zeros((LANE - MLA_ROPE,), F32)]).reshape(1, LANE)
    for layer in range(depth):
        i = layer // 2
        final = layer == depth - 1
        if layer % 2 == 0:
            h = _even_layer(h, batch, seq, ln_even[i], w_in_even[i], gla_alpha_up[i], gla_alpha_bias[i],
                            gla_norm[i], w_out_even[i], final_norm, final)
        else:
            h = _odd_layer(h, pos, inv, batch, seq, ln_odd[i], w_in_odd[i], q_norm[i], w_q_up[i],
                           kv_norm[i], w_kv_up[i], w_out_odd[i], final_norm, final)
    return h.reshape(batch, seq, d)
```

```python
import functools

import jax
import jax.numpy as jnp
from jax import lax
from jax.experimental import pallas as pl
from jax.experimental.pallas import tpu as pltpu

F32 = jnp.float32
BF16 = jnp.bfloat16

SB_HEADS = 8
SB_DIM = 128
GLA_HEADS = 4
GLA_DK = 128
GLA_DV = 256
GLA_RANK = 16
GLA_GATE_NORM = 16.0
GLA_CHUNK = 64
GLA_SUB = 16
MLA_HEADS = 16
MLA_QR = 512
MLA_KVR = 512
MLA_NOPE = 128
MLA_ROPE = 64
MLA_V = 128
MLA_QK_PAD = 256
ROPE_THETA = 10000.0
EPS = 1e-6
LOG2E = 1.4426950408889634

LANE = 128
VMEM_LIMIT = 48 * 1024 * 1024

SB_W = SB_HEADS * SB_DIM
GLA_KW = GLA_HEADS * GLA_DK
GLA_VW = GLA_HEADS * GLA_DV
EVEN_MAIN = 4 * SB_W + 2 * GLA_KW + 2 * GLA_VW


def _cparams(sem):
    return pltpu.CompilerParams(dimension_semantics=sem, vmem_limit_bytes=VMEM_LIMIT)


def _rms(x, g):
    return x * lax.rsqrt(jnp.mean(x * x, axis=-1, keepdims=True) + EPS) * g


def _silu(g):
    return g * (1.0 / (1.0 + jnp.exp(-g)))


def _dot(a, b):
    return jnp.dot(a, b, preferred_element_type=F32)


def _dot_nt(a, b):
    return lax.dot_general(a, b, (((1,), (1,)), ((), ())), preferred_element_type=F32)


def _dot_tn(a, b):
    return lax.dot_general(a, b, (((0,), (0,)), ((), ())), preferred_element_type=F32)


def _norm_matmul_kernel(x_ref, g_ref, w_ref, o_ref, h_ref):
    @pl.when(pl.program_id(1) == 0)
    def _():
        h_ref[...] = _rms(x_ref[...], g_ref[...]).astype(BF16)

    o_ref[...] = _dot(h_ref[...], w_ref[...]).astype(o_ref.dtype)


def _pick_tile(n, cap):
    best = LANE
    for t in range(LANE, cap + 1, LANE):
        if n % t == 0:
            best = t
    return best


def norm_matmul(x, g, w, out_dtype, tm=512, tn_cap=1024):
    T, K = x.shape
    N = w.shape[1]
    tn = _pick_tile(N, tn_cap)
    tm = min(tm, T)
    return pl.pallas_call(
        _norm_matmul_kernel,
        grid=(T // tm, N // tn),
        in_specs=[
            pl.BlockSpec((tm, K), lambda i, j: (i, 0)),
            pl.BlockSpec((1, K), lambda i, j: (0, 0)),
            pl.BlockSpec((K, tn), lambda i, j: (0, j)),
        ],
        out_specs=pl.BlockSpec((tm, tn), lambda i, j: (i, j)),
        out_shape=jax.ShapeDtypeStruct((T, N), out_dtype),
        scratch_shapes=[pltpu.VMEM((tm, K), BF16)],
        compiler_params=_cparams(("parallel", "arbitrary")),
        name="norm_matmul",
    )(x, g.reshape(1, K), w)


def _sb_kernel(q_ref, k_ref, v_ref, g_ref, o_ref, *scratch, tq, tk, rt):
    qi = pl.program_id(2)
    nd = tq // tk
    nf = qi * nd
    nz_b, ls_b, lb_b, tot_b = (scratch[i * nd:(i + 1) * nd] for i in range(4))
    acc_ref = scratch[4 * nd]
    krow = lax.broadcasted_iota(jnp.int32, (tk, tk), 0)
    kcol = lax.broadcasted_iota(jnp.int32, (tk, tk), 1)
    later = (krow > kcol).astype(BF16)
    acc_ref[...] = jnp.zeros_like(acc_ref)
    nz_b[nd - 1][...] = jnp.full((tq, tk), jnp.inf, F32)
    ls_b[nd - 2][...] = jnp.zeros((tq, tk), BF16)
    lb_b[nd - 2][...] = jnp.full((tq, tk), -jnp.inf, F32)
    tot_b[nd - 2][...] = jnp.zeros((tq, LANE), F32)

    def stage_a(start, r0, slot):
        nz_b[slot][r0:, :] = _dot_nt(q_ref[r0:, :], k_ref[pl.ds(start, tk), :])

    def stage_b(r0, slot, masked):
        for t0 in range(r0, tq, rt):
            rows = slice(t0, t0 + rt)
            nz = nz_b[slot][rows, :]
            neg_abs = lax.bitcast_convert_type(lax.bitcast_convert_type(nz, jnp.uint32) | jnp.uint32(0x80000000), F32)
            log_stay = jnp.minimum(nz, 0.0) - jnp.log(1.0 + jnp.exp2(neg_abs)) * LOG2E
            log_beta = log_stay - nz
            if masked and t0 - r0 < tk:
                col = lax.broadcasted_iota(jnp.int32, nz.shape, 1)
                row = lax.broadcasted_iota(jnp.int32, nz.shape, 0)
                keep = col < row + (t0 - r0)
                log_stay = jnp.where(keep, log_stay, 0.0)
                log_beta = jnp.where(keep, log_beta, -jnp.inf)
            ls_b[slot][rows, :] = log_stay.astype(BF16)
            lb_b[slot][rows, :] = log_beta
            tot_b[slot][rows, :] = jnp.broadcast_to(jnp.sum(log_stay, axis=-1, keepdims=True), (rt, LANE))

    def stage_c(start, r0, slot):
        vb = v_ref[pl.ds(start, tk), :]
        for t0 in range(r0, tq, rt):
            rows = slice(t0, t0 + rt)
            within = _dot(ls_b[slot][rows, :], later)
            w = jnp.exp2(lb_b[slot][rows, :] + within)
            acc_ref[rows, :] = acc_ref[rows, :] * jnp.exp2(tot_b[slot][rows, :]) + _dot(w.astype(BF16), vb)

    def body(t, carry):
        for u in range(nd):
            g = t * nd + u
            stage_a(pl.multiple_of(g * tk, tk), 0, u)
            stage_b(0, (u - 1) % nd, False)
            stage_c(pl.multiple_of(jnp.maximum(g - 2, 0) * tk, tk), 0, (u - 2) % nd)
        return carry

    lax.fori_loop(0, qi, body, 0)
    for e in range(nd + 2):
        if e < nd:
            stage_a(pl.multiple_of((nf + e) * tk, tk), e * tk, e)
        if 1 <= e <= nd:
            stage_b((e - 1) * tk, e - 1, True)
        elif e == 0:
            stage_b(0, nd - 1, False)
        if e >= 2:
            stage_c(pl.multiple_of((nf + e - 2) * tk, tk), (e - 2) * tk, e - 2)
        else:
            stage_c(pl.multiple_of(jnp.maximum(nf + e - 2, 0) * tk, tk), 0, (e - 2) % nd)
    o_ref[...] = (acc_ref[...] * _silu(g_ref[...].astype(F32))).astype(o_ref.dtype)


def sb_attention(z, batch, seq, tq=1024, tk=256, rt=256):
    T = z.shape[0]
    tq = min(tq, seq)
    tk = min(tk, tq)
    rt = min(rt, tk)
    nq = seq // tq
    nd = tq // tk
    assert nd >= 2
    hw = SB_W // SB_DIM
    scratch = ([pltpu.VMEM((tq, tk), F32)] * nd + [pltpu.VMEM((tq, tk), BF16)] * nd + [pltpu.VMEM((tq, tk), F32)] * nd
               + [pltpu.VMEM((tq, LANE), F32)] * nd + [pltpu.VMEM((tq, SB_DIM), F32)])
    return pl.pallas_call(
        functools.partial(_sb_kernel, tq=tq, tk=tk, rt=rt),
        grid=(batch, SB_HEADS, nq),
        in_specs=[
            pl.BlockSpec((tq, SB_DIM), lambda b, h, i: (b * nq + i, h)),
            pl.BlockSpec((seq, SB_DIM), lambda b, h, i: (b, hw + h)),
            pl.BlockSpec((seq, SB_DIM), lambda b, h, i: (b, 2 * hw + h)),
            pl.BlockSpec((tq, SB_DIM), lambda b, h, i: (b * nq + i, 3 * hw + h)),
        ],
        out_specs=pl.BlockSpec((tq, SB_DIM), lambda b, h, i: (b * nq + i, h)),
        out_shape=jax.ShapeDtypeStruct((T, SB_W), BF16),
        scratch_shapes=scratch,
        compiler_params=_cparams(("parallel", "parallel", "arbitrary")),
        name="sb_attention",
    )(z, z, z, z)


def _gla_kernel(q_ref, k_ref, v_ref, gg_ref, ga_ref, au_ref, ab_ref, ng_ref, o_ref, st_ref, *, nchunk):
    C, SUB = GLA_CHUNK, GLA_SUB

    @pl.when(pl.program_id(2) == 0)
    def _():
        st_ref[...] = jnp.zeros_like(st_ref)

    ri = lax.broadcasted_iota(jnp.int32, (C, C), 0)
    ci = lax.broadcasted_iota(jnp.int32, (C, C), 1)
    incl = (ci <= ri).astype(F32)
    sub_row = lax.broadcasted_iota(jnp.int32, (SUB, 1), 0)
    hp = lax.Precision.HIGHEST

    st = st_ref[...]
    for c in range(nchunk):
        r0 = c * C
        q = q_ref[r0:r0 + C, :].astype(F32) * (GLA_DK ** -0.5)
        k = k_ref[r0:r0 + C, :].astype(F32)
        v = v_ref[r0:r0 + C, :]
        vf = v.astype(F32)
        pre = jnp.dot(ga_ref[r0:r0 + C, :], au_ref[...], precision=hp, preferred_element_type=F32) + ab_ref[...]
        log_f = -(jnp.maximum(-pre, 0.0) + jnp.log1p(jnp.exp(-jnp.abs(pre)))) / GLA_GATE_NORM
        b = jnp.dot(incl, log_f, precision=hp, preferred_element_type=F32)
        b_last = b[C - 1:C, :]

        o_inter = _dot_nt((q * jnp.exp(b)).astype(BF16), st.astype(BF16))

        pieces = []
        for s in range(C // SUB):
            i0 = s * SUB
            bs = b[i0:i0 + SUB, :]
            qs = q[i0:i0 + SUB, :]
            o_s = o_inter[i0:i0 + SUB, :]
            if s > 0:
                b0 = b[i0:i0 + 1, :]
                q_dec = (qs * jnp.exp(bs - b0)).astype(BF16)
                k_dec = (k[0:i0, :] * jnp.exp(b0 - b[0:i0, :])).astype(BF16)
                att = _dot_nt(q_dec, k_dec)
                o_s = o_s + _dot(att.astype(BF16), v[0:i0, :])
            for j in range(SUB):
                jj = i0 + j
                dec = jnp.exp(jnp.minimum(bs - b[jj:jj + 1, :], 0.0))
                a = jnp.sum(qs * k[jj:jj + 1, :] * dec, axis=-1, keepdims=True)
                a = jnp.where(sub_row >= j, a, 0.0)
                o_s = o_s + a * vf[jj:jj + 1, :]
            pieces.append(o_s)
        o = jnp.concatenate(pieces, axis=0)

        k_dec = (k * jnp.exp(b_last - b)).astype(BF16)
        st = st * jnp.exp(b_last) + _dot_tn(v, k_dec)

        y = _rms(o, ng_ref[...]) * _silu(gg_ref[r0:r0 + C, :].astype(F32))
        o_ref[r0:r0 + C, :] = y.astype(o_ref.dtype)
    st_ref[...] = st


def gla(z, ga, au, ab, ng, batch, seq, tb=256):
    T = z.shape[0]
    tb = min(tb, seq)
    nb = seq // tb
    q0 = 4 * SB_W // GLA_DK
    k0 = q0 + GLA_HEADS
    v0 = (4 * SB_W + 2 * GLA_KW) // GLA_DV
    g0 = v0 + GLA_HEADS
    return pl.pallas_call(
        functools.partial(_gla_kernel, nchunk=tb // GLA_CHUNK),
        grid=(batch, GLA_HEADS, nb),
        in_specs=[
            pl.BlockSpec((tb, GLA_DK), lambda b, h, i: (b * nb + i, q0 + h)),
            pl.BlockSpec((tb, GLA_DK), lambda b, h, i: (b * nb + i, k0 + h)),
            pl.BlockSpec((tb, GLA_DV), lambda b, h, i: (b * nb + i, v0 + h)),
            pl.BlockSpec((tb, GLA_DV), lambda b, h, i: (b * nb + i, g0 + h)),
            pl.BlockSpec((tb, LANE), lambda b, h, i: (b * nb + i, 0)),
            pl.BlockSpec((LANE, GLA_DK), lambda b, h, i: (0, h)),
            pl.BlockSpec((1, GLA_DK), lambda b, h, i: (0, h)),
            pl.BlockSpec((1, GLA_DV), lambda b, h, i: (0, 0)),
        ],
        out_specs=pl.BlockSpec((tb, GLA_DV), lambda b, h, i: (b * nb + i, h)),
        out_shape=jax.ShapeDtypeStruct((T, GLA_VW), BF16),
        scratch_shapes=[pltpu.VMEM((GLA_DV, GLA_DK), F32)],
        compiler_params=_cparams(("parallel", "parallel", "arbitrary")),
        name="gla",
    )(z, z, z, z, ga, au, ab, ng)


def _out_proj_kernel(y1_ref, y2_ref, w1_ref, w2_ref, x_ref, gf_ref, o_ref, *, final):
    acc = x_ref[...] + _dot(y1_ref[...], w1_ref[...]) + _dot(y2_ref[...], w2_ref[...])
    if final:
        acc = _rms(acc, gf_ref[...])
    o_ref[...] = acc


def out_proj(y1, c1, y2, c2, w, x, gf, final, tm=256):
    T, D = x.shape
    kh = w.shape[0] // 2
    tm = min(tm, T)
    return pl.pallas_call(
        functools.partial(_out_proj_kernel, final=final),
        grid=(T // tm,),
        in_specs=[
            pl.BlockSpec((tm, kh), lambda i: (i, c1)),
            pl.BlockSpec((tm, kh), lambda i: (i, c2)),
            pl.BlockSpec((kh, D), lambda i: (0, 0)),
            pl.BlockSpec((kh, D), lambda i: (1, 0)),
            pl.BlockSpec((tm, D), lambda i: (i, 0)),
            pl.BlockSpec((1, D), lambda i: (0, 0)),
        ],
        out_specs=pl.BlockSpec((tm, D), lambda i: (i, 0)),
        out_shape=jax.ShapeDtypeStruct((T, D), F32),
        compiler_params=_cparams(("parallel",)),
        name="out_proj",
    )(y1, y2, w, w, x, gf.reshape(1, D))


def _mla_up_kernel(lat_ref, pos_ref, inv_ref, qg_ref, kvg_ref, wq_ref, wkv_ref, q_ref, k_ref, v_ref, *, scale):
    hq = _rms(lat_ref[:, 0:MLA_QR], qg_ref[...]).astype(BF16)
    hkv = _rms(lat_ref[:, MLA_QR:MLA_QR + MLA_KVR], kvg_ref[...]).astype(BF16)
    ang = pos_ref[...].astype(F32) * inv_ref[...]
    lane = lax.broadcasted_iota(jnp.int32, ang.shape, 1)
    half = MLA_ROPE // 2
    cos2 = jnp.where(lane < MLA_ROPE, jnp.cos(ang), 0.0)
    sin2 = jnp.where(lane < half, -jnp.sin(ang), jnp.where(lane < MLA_ROPE, jnp.sin(ang), 0.0))

    def rope(r):
        return r * cos2 + pltpu.roll(r, MLA_ROPE, 1) * sin2

    kr = rope(lat_ref[:, MLA_QR + MLA_KVR:MLA_QR + MLA_KVR + LANE]).astype(BF16)
    kn = _dot(hkv, wkv_ref[:, 0:MLA_HEADS * MLA_NOPE]).astype(BF16)
    v_ref[...] = _dot(hkv, wkv_ref[:, MLA_HEADS * MLA_NOPE:]).astype(BF16)
    for h in range(MLA_HEADS):
        c0 = h * MLA_QK_PAD
        y = _dot(hq, wq_ref[:, c0:c0 + MLA_QK_PAD])
        q_ref[:, c0:c0 + MLA_NOPE] = (y[:, 0:MLA_NOPE] * scale).astype(BF16)
        q_ref[:, c0 + MLA_NOPE:c0 + MLA_QK_PAD] = (rope(y[:, MLA_NOPE:]) * scale).astype(BF16)
        k_ref[:, c0:c0 + MLA_NOPE] = kn[:, h * MLA_NOPE:(h + 1) * MLA_NOPE]
        k_ref[:, c0 + MLA_NOPE:c0 + MLA_QK_PAD] = kr


def mla_up(lat, pos, inv, qg, kvg, wq, wkv, tm=256):
    T = lat.shape[0]
    tm = min(tm, T)
    qk_w = MLA_HEADS * MLA_QK_PAD
    v_w = MLA_HEADS * MLA_V
    return pl.pallas_call(
        functools.partial(_mla_up_kernel, scale=(MLA_NOPE + MLA_ROPE) ** -0.5 * LOG2E),
        grid=(T // tm,),
        in_specs=[
            pl.BlockSpec((tm, lat.shape[1]), lambda i: (i, 0)),
            pl.BlockSpec((tm, 1), lambda i: (i, 0)),
            pl.BlockSpec((1, LANE), lambda i: (0, 0)),
            pl.BlockSpec((1, MLA_QR), lambda i: (0, 0)),
            pl.BlockSpec((1, MLA_KVR), lambda i: (0, 0)),
            pl.BlockSpec(wq.shape, lambda i: (0, 0)),
            pl.BlockSpec(wkv.shape, lambda i: (0, 0)),
        ],
        out_specs=[
            pl.BlockSpec((tm, qk_w), lambda i: (i, 0)),
            pl.BlockSpec((tm, qk_w), lambda i: (i, 0)),
            pl.BlockSpec((tm, v_w), lambda i: (i, 0)),
        ],
        out_shape=[
            jax.ShapeDtypeStruct((T, qk_w), BF16),
            jax.ShapeDtypeStruct((T, qk_w), BF16),
            jax.ShapeDtypeStruct((T, v_w), BF16),
        ],
        compiler_params=_cparams(("parallel",)),
        name="mla_up",
    )(lat, pos, inv, qg.reshape(1, -1), kvg.reshape(1, -1), wq, wkv)


def _mla_attn_kernel(q_ref, k_ref, v_ref, g_ref, o_ref, s0_ref, s1_ref, m_ref, l_ref, acc_ref, *, tq, tk, rt):
    qi = pl.program_id(2)
    nd = tq // tk
    nf = qi * nd
    base = qi * tq
    reps = tk // LANE
    s_refs = (s0_ref, s1_ref)
    m_ref[...] = jnp.full_like(m_ref, -jnp.inf)
    l_ref[...] = jnp.zeros_like(l_ref)
    acc_ref[...] = jnp.zeros_like(acc_ref)

    def scores(start, r0, slot):
        s_refs[slot][r0:, :] = _dot_nt(q_ref[r0:, :], k_ref[pl.ds(start, tk), :])

    def update(start, r0, slot, masked):
        vb = v_ref[pl.ds(start, tk), :]
        for t0 in range(r0, tq, rt):
            rows = slice(t0, t0 + rt)
            s = s_refs[slot][rows, :]
            if masked and t0 - r0 < tk:
                col = lax.broadcasted_iota(jnp.int32, s.shape, 1)
                row = lax.broadcasted_iota(jnp.int32, s.shape, 0)
                s = jnp.where(col <= row + (t0 - r0), s, -jnp.inf)
            m_old = m_ref[rows, :]
            m_new = jnp.maximum(m_old, jnp.max(s, axis=-1, keepdims=True))
            alpha = jnp.exp2(m_old - m_new)
            p = jnp.exp2(s - jnp.tile(m_new, (1, reps)))
            l_ref[rows, :] = alpha * l_ref[rows, :] + jnp.sum(p, axis=-1, keepdims=True)
            acc_ref[rows, :] = alpha * acc_ref[rows, :] + _dot(p.astype(BF16), vb)
            m_ref[rows, :] = m_new

    scores(0, 0, 0)

    def body(i, carry):
        for u in range(nd):
            j = i * nd + u
            scores(pl.multiple_of((j + 1) * tk, tk), 0, (u + 1) % 2)
            update(pl.multiple_of(j * tk, tk), 0, u % 2, False)
        return carry

    lax.fori_loop(0, qi, body, 0)
    for m in range(nd):
        if m + 1 < nd:
            scores(pl.multiple_of(base + (m + 1) * tk, tk), (m + 1) * tk, (m + 1) % 2)
        update(pl.multiple_of(base + m * tk, tk), m * tk, m % 2, True)
    o_ref[...] = (acc_ref[...] / l_ref[...] * _silu(g_ref[...].astype(F32))).astype(o_ref.dtype)


def mla_attention(qf, kf, v, gate, batch, seq, tq=1024, tk=512, rt=256):
    T = qf.shape[0]
    tq = min(tq, seq)
    tk = min(tk, tq)
    rt = min(rt, tk)
    nq = seq // tq
    assert (tq // tk) % 2 == 0 or nq == 1, "score-buffer parity is static only for an even block count per tile"
    return pl.pallas_call(
        functools.partial(_mla_attn_kernel, tq=tq, tk=tk, rt=rt),
        grid=(batch, MLA_HEADS, nq),
        in_specs=[
            pl.BlockSpec((tq, MLA_QK_PAD), lambda b, h, i: (b * nq + i, h)),
            pl.BlockSpec((seq, MLA_QK_PAD), lambda b, h, i: (b, h)),
            pl.BlockSpec((seq, MLA_V), lambda b, h, i: (b, h)),
            pl.BlockSpec((tq, MLA_V), lambda b, h, i: (b * nq + i, h)),
        ],
        out_specs=pl.BlockSpec((tq, MLA_V), lambda b, h, i: (b * nq + i, h)),
        out_shape=jax.ShapeDtypeStruct((T, MLA_HEADS * MLA_V), BF16),
        scratch_shapes=[pltpu.VMEM((tq, tk), F32), pltpu.VMEM((tq, tk), F32), pltpu.VMEM((tq, LANE), F32),
                        pltpu.VMEM((tq, LANE), F32), pltpu.VMEM((tq, MLA_V), F32)],
        compiler_params=_cparams(("parallel", "parallel", "arbitrary")),
        name="mla_attention",
    )(qf, kf, v, gate)


def _swap_halves(w):
    half = w.shape[-1] // 2
    return jnp.concatenate([w[..., half:], w[..., :half]], axis=-1)


def _prep_even(w_in, alpha_up, alpha_bias):
    col_scale = jnp.concatenate([jnp.full((SB_W,), -(SB_DIM ** -0.5) * LOG2E, F32), jnp.ones((EVEN_MAIN - SB_W,), F32)])
    w_main = (w_in[:, :EVEN_MAIN] * col_scale).astype(BF16)
    w_ga = jnp.pad(w_in[:, EVEN_MAIN:], ((0, 0), (0, LANE - GLA_RANK))).astype(BF16)
    au = jnp.pad(alpha_up, ((0, LANE - GLA_RANK), (0, 0)))
    return w_main, w_ga, au, alpha_bias.reshape(1, -1)


def _prep_odd(w_in, w_q_up, w_kv_up):
    d = w_in.shape[0]
    o1 = MLA_QR + MLA_KVR
    w_kr = w_in[:, o1:o1 + MLA_ROPE]
    w_lat = jnp.concatenate([w_in[:, :o1], w_kr, _swap_halves(w_kr)], axis=1).astype(BF16)
    w_gate = w_in[:, o1 + MLA_ROPE:].astype(BF16)
    wq = w_q_up.reshape(MLA_QR, MLA_HEADS, MLA_NOPE + MLA_ROPE)
    wq_r = wq[..., MLA_NOPE:]
    wq = jnp.concatenate([wq[..., :MLA_NOPE], wq_r, _swap_halves(wq_r)], axis=-1)
    wq = wq.reshape(MLA_QR, MLA_HEADS * MLA_QK_PAD).astype(BF16)
    wkv = w_kv_up.reshape(MLA_KVR, MLA_HEADS, MLA_NOPE + MLA_V)
    wkv = jnp.concatenate([wkv[..., :MLA_NOPE].reshape(MLA_KVR, -1), wkv[..., MLA_NOPE:].reshape(MLA_KVR, -1)], axis=1)
    del d
    return w_lat, w_gate, wq, wkv.astype(BF16)


def _even_layer(x, batch, seq, norm_g, w_in, alpha_up, alpha_bias, gla_norm_g, w_out, final_g, final):
    w_main, w_ga, au, ab = _prep_even(w_in, alpha_up, alpha_bias)
    z = norm_matmul(x, norm_g, w_main, BF16)
    ga = norm_matmul(x, norm_g, w_ga, F32)
    y_a = sb_attention(z, batch, seq)
    y_b = gla(z, ga, au, ab, gla_norm_g.reshape(1, -1), batch, seq)
    return out_proj(y_a, 0, y_b, 0, w_out.astype(BF16), x, final_g, final)


def _odd_layer(x, pos, inv, batch, seq, norm_g, w_in, q_norm_g, w_q_up, kv_norm_g, w_kv_up, w_out, final_g, final):
    w_lat, w_gate, wq, wkv = _prep_odd(w_in, w_q_up, w_kv_up)
    lat = norm_matmul(x, norm_g, w_lat, F32)
    gate = norm_matmul(x, norm_g, w_gate, BF16)
    qf, kf, v = mla_up(lat, pos, inv, q_norm_g, kv_norm_g, wq, wkv)
    y = mla_attention(qf, kf, v, gate, batch, seq)
    return out_proj(y, 0, y, 1, w_out.astype(BF16), x, final_g, final)


def kernel(x, positions, ln_even, w_in_even, gla_alpha_up, gla_alpha_bias, gla_norm, w_out_even,
           ln_odd, w_in_odd, q_norm, w_q_up, kv_norm, w_kv_up, w_out_odd, final_norm):
    batch, seq, d = x.shape
    depth = ln_even.shape[0] + ln_odd.shape[0]
    h = x.reshape(batch * seq, d)
    pos = positions.reshape(batch * seq, 1)
    half = MLA_ROPE // 2
    inv = ROPE_THETA ** (-jnp.arange(half, dtype=F32) / half)
    inv = jnp.concatenate([inv, inv, jnp.zeros((LANE - MLA_ROPE,), F32)]).reshape(1, LANE)
    for layer in range(depth):
        i = layer // 2
        final = layer == depth - 1
        if layer % 2 == 0:
            h = _even_layer(h, batch, seq, ln_even[i], w_in_even[i], gla_alpha_up[i], gla_alpha_bias[i],
                            gla_norm[i], w_out_even[i], final_norm, final)
        else:
            h = _odd_layer(h, pos, inv, batch, seq, ln_odd[i], w_in_odd[i], q_norm[i], w_q_up[i],
                           kv_norm[i], w_kv_up[i], w_out_odd[i], final_norm, final)
    return h.reshape(batch, seq, d)
```

```python
import functools

import jax
import jax.numpy as jnp
from jax import lax
from jax.experimental import pallas as pl
from jax.experimental.pallas import tpu as pltpu

F32 = jnp.float32
BF16 = jnp.bfloat16

SB_HEADS = 8
SB_DIM = 128
GLA_HEADS = 4
GLA_DK = 128
GLA_DV = 256
GLA_RANK = 16
GLA_GATE_NORM = 16.0
GLA_CHUNK = 64
GLA_SUB = 16
SB_SLOTS = 4
MLA_HEADS = 16
MLA_QR = 512
MLA_KVR = 512
MLA_NOPE = 128
MLA_ROPE = 64
MLA_V = 128
MLA_QK_PAD = 256
ROPE_THETA = 10000.0
EPS = 1e-6
LOG2E = 1.4426950408889634

LANE = 128
SUBLANE = 8
VMEM_LIMIT = 48 * 1024 * 1024

SB_W = SB_HEADS * SB_DIM
GLA_KW = GLA_HEADS * GLA_DK
GLA_VW = GLA_HEADS * GLA_DV
EVEN_MAIN = 4 * SB_W + 2 * GLA_KW + 2 * GLA_VW


def _cparams(sem):
    return pltpu.CompilerParams(dimension_semantics=sem, vmem_limit_bytes=VMEM_LIMIT)


def _rms(x, g):
    return x * lax.rsqrt(jnp.mean(x * x, axis=-1, keepdims=True) + EPS) * g


def _silu(g):
    return g * (1.0 / (1.0 + jnp.exp(-g)))


def _dot(a, b):
    return jnp.dot(a, b, preferred_element_type=F32)


def _hi_lo(x):
    hi = x.astype(BF16)
    return hi, (x - hi.astype(F32)).astype(BF16)


def _dot_nt(a, b):
    return lax.dot_general(a, b, (((1,), (1,)), ((), ())), preferred_element_type=F32)


def _dot_tn(a, b):
    return lax.dot_general(a, b, (((0,), (0,)), ((), ())), preferred_element_type=F32)


def _norm_matmul_kernel(x_ref, g_ref, w_ref, o_ref, h_ref):
    @pl.when(pl.program_id(1) == 0)
    def _():
        h_ref[...] = _rms(x_ref[...], g_ref[...]).astype(BF16)

    o_ref[...] = _dot(h_ref[...], w_ref[...]).astype(o_ref.dtype)


def _pick_tile(n, cap):
    best = LANE
    for t in range(LANE, cap + 1, LANE):
        if n % t == 0:
            best = t
    return best


def norm_matmul(x, g, w, out_dtype, tm=512, tn_cap=1024):
    T, K = x.shape
    N = w.shape[1]
    tn = _pick_tile(N, tn_cap)
    tm = min(tm, T)
    return pl.pallas_call(
        _norm_matmul_kernel,
        grid=(T // tm, N // tn),
        in_specs=[
            pl.BlockSpec((tm, K), lambda i, j: (i, 0)),
            pl.BlockSpec((1, K), lambda i, j: (0, 0)),
            pl.BlockSpec((K, tn), lambda i, j: (0, j)),
        ],
        out_specs=pl.BlockSpec((tm, tn), lambda i, j: (i, j)),
        out_shape=jax.ShapeDtypeStruct((T, N), out_dtype),
        scratch_shapes=[pltpu.VMEM((tm, K), BF16)],
        compiler_params=_cparams(("parallel", "arbitrary")),
        name="norm_matmul",
    )(x, g.reshape(1, K), w)


def _sb_kernel(q_ref, k_ref, v_ref, g_ref, o_ref, *scratch, tq, tk, rt):
    qi = pl.program_id(2)
    nd = tq // tk
    ns = SB_SLOTS
    nf = qi * nd
    nz_b, ls_b, lb_b, tot_b = (scratch[i * ns:(i + 1) * ns] for i in range(4))
    acc_ref = scratch[4 * ns]
    krow = lax.broadcasted_iota(jnp.int32, (tk, tk), 0)
    kcol = lax.broadcasted_iota(jnp.int32, (tk, tk), 1)
    later = (krow > kcol).astype(BF16)
    acc_ref[...] = jnp.zeros_like(acc_ref)
    nz_b[ns - 1][...] = jnp.full((tq, tk), jnp.inf, F32)
    ls_b[ns - 2][...] = jnp.zeros((tq, tk), BF16)
    lb_b[ns - 2][...] = jnp.full((tq, tk), -jnp.inf, F32)
    tot_b[ns - 2][...] = jnp.zeros((tq, LANE), F32)

    def stage_a(start, r0, slot):
        nz_b[slot][r0:, :] = _dot_nt(q_ref[r0:, :], k_ref[pl.ds(start, tk), :])

    def stage_b(r0, slot, masked):
        for t0 in range(r0, tq, rt):
            rows = slice(t0, t0 + rt)
            nz = nz_b[slot][rows, :]
            neg_abs = lax.bitcast_convert_type(lax.bitcast_convert_type(nz, jnp.uint32) | jnp.uint32(0x80000000), F32)
            log_stay = jnp.minimum(nz, 0.0) - jnp.log(1.0 + jnp.exp2(neg_abs)) * LOG2E
            log_beta = log_stay - nz
            if masked and t0 - r0 < tk:
                col = lax.broadcasted_iota(jnp.int32, nz.shape, 1)
                row = lax.broadcasted_iota(jnp.int32, nz.shape, 0)
                keep = col < row + (t0 - r0)
                log_stay = jnp.where(keep, log_stay, 0.0)
                log_beta = jnp.where(keep, log_beta, -jnp.inf)
            ls_b[slot][rows, :] = log_stay.astype(BF16)
            lb_b[slot][rows, :] = log_beta
            tot_b[slot][rows, :] = jnp.broadcast_to(jnp.sum(log_stay, axis=-1, keepdims=True), (rt, LANE))

    def stage_c(start, r0, slot):
        vb = v_ref[pl.ds(start, tk), :]
        for t0 in range(r0, tq, rt):
            rows = slice(t0, t0 + rt)
            within = _dot(ls_b[slot][rows, :], later)
            w = jnp.exp2(lb_b[slot][rows, :] + within)
            acc_ref[rows, :] = acc_ref[rows, :] * jnp.exp2(tot_b[slot][rows, :]) + _dot(w.astype(BF16), vb)

    def body(t, carry):
        for u in range(nd):
            g = t * nd + u
            stage_a(pl.multiple_of(g * tk, tk), 0, u % ns)
            stage_b(0, (u - 1) % ns, False)
            stage_c(pl.multiple_of(jnp.maximum(g - 2, 0) * tk, tk), 0, (u - 2) % ns)
        return carry

    lax.fori_loop(0, qi, body, 0)
    for e in range(nd + 2):
        if e < nd:
            stage_a(pl.multiple_of((nf + e) * tk, tk), e * tk, e % ns)
        if 1 <= e <= nd:
            stage_b((e - 1) * tk, (e - 1) % ns, True)
        elif e == 0:
            stage_b(0, ns - 1, False)
        if e >= 2:
            stage_c(pl.multiple_of((nf + e - 2) * tk, tk), (e - 2) * tk, (e - 2) % ns)
        else:
            stage_c(pl.multiple_of(jnp.maximum(nf + e - 2, 0) * tk, tk), 0, (e - 2) % ns)
    o_ref[...] = (acc_ref[...] * _silu(g_ref[...].astype(F32))).astype(o_ref.dtype)


def sb_attention(z, batch, seq, tq=1024, tk=256, rt=256):
    T = z.shape[0]
    tq = min(tq, seq)
    tk = min(tk, tq)
    rt = min(rt, tk)
    nq = seq // tq
    ns = SB_SLOTS
    assert (tq // tk) % ns == 0, "slot of a block must not depend on the loop trip"
    hw = SB_W // SB_DIM
    scratch = ([pltpu.VMEM((tq, tk), F32)] * ns + [pltpu.VMEM((tq, tk), BF16)] * ns + [pltpu.VMEM((tq, tk), F32)] * ns
               + [pltpu.VMEM((tq, LANE), F32)] * ns + [pltpu.VMEM((tq, SB_DIM), F32)])
    return pl.pallas_call(
        functools.partial(_sb_kernel, tq=tq, tk=tk, rt=rt),
        grid=(batch, SB_HEADS, nq),
        in_specs=[
            pl.BlockSpec((tq, SB_DIM), lambda b, h, i: (b * nq + i, h)),
            pl.BlockSpec((seq, SB_DIM), lambda b, h, i: (b, hw + h)),
            pl.BlockSpec((seq, SB_DIM), lambda b, h, i: (b, 2 * hw + h)),
            pl.BlockSpec((tq, SB_DIM), lambda b, h, i: (b * nq + i, 3 * hw + h)),
        ],
        out_specs=pl.BlockSpec((tq, SB_DIM), lambda b, h, i: (b * nq + i, h)),
        out_shape=jax.ShapeDtypeStruct((T, SB_W), BF16),
        scratch_shapes=scratch,
        compiler_params=_cparams(("parallel", "parallel", "arbitrary")),
        name="sb_attention",
    )(z, z, z, z)


def _gla_kernel(q_ref, k_ref, v_ref, gg_ref, ga_ref, au_ref, ab_ref, ng_ref, o_ref, st_ref, incl_ref, *, nchunk):
    C, SUB, G = GLA_CHUNK, GLA_SUB, SUBLANE
    tb = nchunk * C

    @pl.when(pl.program_id(2) == 0)
    def _():
        st_ref[...] = jnp.zeros_like(st_ref)
        ri = lax.broadcasted_iota(jnp.int32, (tb, tb), 0)
        ci = lax.broadcasted_iota(jnp.int32, (tb, tb), 1)
        incl_ref[...] = jnp.where(ci <= ri, jnp.where(ci >= (ri & ~(C - 1)), 1.0, 0.0), 0.0).astype(BF16)

    grp_ri = lax.broadcasted_iota(jnp.int32, (G, C), 0)
    grp_ci = lax.broadcasted_iota(jnp.int32, (G, C), 1)

    g_hi, g_lo = _hi_lo(ga_ref[...])
    a_hi, a_lo = _hi_lo(au_ref[...])
    pre = _dot(g_hi, a_hi) + _dot(g_hi, a_lo) + _dot(g_lo, a_hi) + ab_ref[...]
    log_f = -(jnp.maximum(-pre, 0.0) + jnp.log1p(jnp.exp(-jnp.abs(pre)))) / GLA_GATE_NORM
    f_hi = log_f.astype(BF16)
    f_mid, f_lo = _hi_lo(log_f - f_hi.astype(F32))
    incl = incl_ref[...]
    b_all = (_dot(incl, f_hi) + _dot(incl, f_mid) + _dot(incl, f_lo)) * LOG2E

    st = st_ref[...]
    for c in range(nchunk):
        r0 = c * C
        q = q_ref[r0:r0 + C, :].astype(F32) * (GLA_DK ** -0.5)
        k = k_ref[r0:r0 + C, :].astype(F32)
        v = v_ref[r0:r0 + C, :]
        b = b_all[r0:r0 + C, :]
        b_last = b[C - 1:C, :]

        o_inter = _dot_nt((q * jnp.exp2(b)).astype(BF16), st.astype(BF16))

        att_rows = []
        for s in range(C // SUB):
            i0 = s * SUB
            if s > 0:
                b0 = b[i0:i0 + 1, :]
                q_dec = (q[i0:i0 + SUB, :] * jnp.exp2(b[i0:i0 + SUB, :] - b0)).astype(BF16)
                k_dec = (k * jnp.exp2(jnp.minimum(b0 - b, 0.0))).astype(BF16)
                att_far = _dot_nt(q_dec, k_dec)
            for g0 in range(i0, i0 + SUB, G):
                bg = b[g0:g0 + G, :]
                qg = q[g0:g0 + G, :]
                att = jnp.where(grp_ci < i0, att_far[g0 - i0:g0 - i0 + G, :], 0.0) if s > 0 else jnp.zeros((G, C), F32)
                for jj in range(i0, g0 + G):
                    a = jnp.sum(qg * k[jj:jj + 1, :] * jnp.exp2(bg - b[jj:jj + 1, :]), axis=-1, keepdims=True)
                    att = jnp.where(grp_ci == jj, a, att)
                att_rows.append(jnp.where(grp_ci <= grp_ri + g0, att, 0.0))
        att = jnp.concatenate(att_rows, axis=0)
        o = o_inter + _dot(att.astype(BF16), v)

        k_dec = (k * jnp.exp2(b_last - b)).astype(BF16)
        st = st * jnp.exp2(b_last) + _dot_tn(v, k_dec)

        y = _rms(o, ng_ref[...]) * _silu(gg_ref[r0:r0 + C, :].astype(F32))
        o_ref[r0:r0 + C, :] = y.astype(o_ref.dtype)
    st_ref[...] = st


def gla(z, ga, au, ab, ng, batch, seq, tb=512):
    T = z.shape[0]
    tb = min(tb, seq)
    nb = seq // tb
    q0 = 4 * SB_W // GLA_DK
    k0 = q0 + GLA_HEADS
    v0 = (4 * SB_W + 2 * GLA_KW) // GLA_DV
    g0 = v0 + GLA_HEADS
    return pl.pallas_call(
        functools.partial(_gla_kernel, nchunk=tb // GLA_CHUNK),
        grid=(batch, GLA_HEADS, nb),
        in_specs=[
            pl.BlockSpec((tb, GLA_DK), lambda b, h, i: (b * nb + i, q0 + h)),
            pl.BlockSpec((tb, GLA_DK), lambda b, h, i: (b * nb + i, k0 + h)),
            pl.BlockSpec((tb, GLA_DV), lambda b, h, i: (b * nb + i, v0 + h)),
            pl.BlockSpec((tb, GLA_DV), lambda b, h, i: (b * nb + i, g0 + h)),
            pl.BlockSpec((tb, LANE), lambda b, h, i: (b * nb + i, 0)),
            pl.BlockSpec((LANE, GLA_DK), lambda b, h, i: (0, h)),
            pl.BlockSpec((1, GLA_DK), lambda b, h, i: (0, h)),
            pl.BlockSpec((1, GLA_DV), lambda b, h, i: (0, 0)),
        ],
        out_specs=pl.BlockSpec((tb, GLA_DV), lambda b, h, i: (b * nb + i, h)),
        out_shape=jax.ShapeDtypeStruct((T, GLA_VW), BF16),
        scratch_shapes=[pltpu.VMEM((GLA_DV, GLA_DK), F32), pltpu.VMEM((tb, tb), BF16)],
        compiler_params=_cparams(("parallel", "parallel", "arbitrary")),
        name="gla",
    )(z, z, z, z, ga, au, ab, ng)


def _out_proj_kernel(y1_ref, y2_ref, w1_ref, w2_ref, x_ref, gf_ref, o_ref, *, final):
    acc = x_ref[...] + _dot(y1_ref[...], w1_ref[...]) + _dot(y2_ref[...], w2_ref[...])
    if final:
        acc = _rms(acc, gf_ref[...])
    o_ref[...] = acc


def out_proj(y1, c1, y2, c2, w, x, gf, final, tm=256):
    T, D = x.shape
    kh = w.shape[0] // 2
    tm = min(tm, T)
    return pl.pallas_call(
        functools.partial(_out_proj_kernel, final=final),
        grid=(T // tm,),
        in_specs=[
            pl.BlockSpec((tm, kh), lambda i: (i, c1)),
            pl.BlockSpec((tm, kh), lambda i: (i, c2)),
            pl.BlockSpec((kh, D), lambda i: (0, 0)),
            pl.BlockSpec((kh, D), lambda i: (1, 0)),
            pl.BlockSpec((tm, D), lambda i: (i, 0)),
            pl.BlockSpec((1, D), lambda i: (0, 0)),
        ],
        out_specs=pl.BlockSpec((tm, D), lambda i: (i, 0)),
        out_shape=jax.ShapeDtypeStruct((T, D), F32),
        compiler_params=_cparams(("parallel",)),
        name="out_proj",
    )(y1, y2, w, w, x, gf.reshape(1, D))


def _mla_up_kernel(lat_ref, pos_ref, inv_ref, qg_ref, kvg_ref, wq_ref, wkv_ref, q_ref, k_ref, v_ref, *, scale):
    hq = _rms(lat_ref[:, 0:MLA_QR], qg_ref[...]).astype(BF16)
    hkv = _rms(lat_ref[:, MLA_QR:MLA_QR + MLA_KVR], kvg_ref[...]).astype(BF16)
    ang = pos_ref[...].astype(F32) * inv_ref[...]
    lane = lax.broadcasted_iota(jnp.int32, ang.shape, 1)
    half = MLA_ROPE // 2
    cos2 = jnp.where(lane < MLA_ROPE, jnp.cos(ang), 0.0)
    sin2 = jnp.where(lane < half, -jnp.sin(ang), jnp.where(lane < MLA_ROPE, jnp.sin(ang), 0.0))

    def rope(r):
        return r * cos2 + pltpu.roll(r, MLA_ROPE, 1) * sin2

    kr = rope(lat_ref[:, MLA_QR + MLA_KVR:MLA_QR + MLA_KVR + LANE]).astype(BF16)
    kn = _dot(hkv, wkv_ref[:, 0:MLA_HEADS * MLA_NOPE]).astype(BF16)
    v_ref[...] = _dot(hkv, wkv_ref[:, MLA_HEADS * MLA_NOPE:]).astype(BF16)
    for h in range(MLA_HEADS):
        c0 = h * MLA_QK_PAD
        y = _dot(hq, wq_ref[:, c0:c0 + MLA_QK_PAD])
        q_ref[:, c0:c0 + MLA_NOPE] = (y[:, 0:MLA_NOPE] * scale).astype(BF16)
        q_ref[:, c0 + MLA_NOPE:c0 + MLA_QK_PAD] = (rope(y[:, MLA_NOPE:]) * scale).astype(BF16)
        k_ref[:, c0:c0 + MLA_NOPE] = kn[:, h * MLA_NOPE:(h + 1) * MLA_NOPE]
        k_ref[:, c0 + MLA_NOPE:c0 + MLA_QK_PAD] = kr


def mla_up(lat, pos, inv, qg, kvg, wq, wkv, tm=256):
    T = lat.shape[0]
    tm = min(tm, T)
    qk_w = MLA_HEADS * MLA_QK_PAD
    v_w = MLA_HEADS * MLA_V
    return pl.pallas_call(
        functools.partial(_mla_up_kernel, scale=(MLA_NOPE + MLA_ROPE) ** -0.5 * LOG2E),
        grid=(T // tm,),
        in_specs=[
            pl.BlockSpec((tm, lat.shape[1]), lambda i: (i, 0)),
            pl.BlockSpec((tm, 1), lambda i: (i, 0)),
            pl.BlockSpec((1, LANE), lambda i: (0, 0)),
            pl.BlockSpec((1, MLA_QR), lambda i: (0, 0)),
            pl.BlockSpec((1, MLA_KVR), lambda i: (0, 0)),
            pl.BlockSpec(wq.shape, lambda i: (0, 0)),
            pl.BlockSpec(wkv.shape, lambda i: (0, 0)),
        ],
        out_specs=[
            pl.BlockSpec((tm, qk_w), lambda i: (i, 0)),
            pl.BlockSpec((tm, qk_w), lambda i: (i, 0)),
            pl.BlockSpec((tm, v_w), lambda i: (i, 0)),
        ],
        out_shape=[
            jax.ShapeDtypeStruct((T, qk_w), BF16),
            jax.ShapeDtypeStruct((T, qk_w), BF16),
            jax.ShapeDtypeStruct((T, v_w), BF16),
        ],
        compiler_params=_cparams(("parallel",)),
        name="mla_up",
    )(lat, pos, inv, qg.reshape(1, -1), kvg.reshape(1, -1), wq, wkv)


def _mla_attn_kernel(q_ref, k_ref, v_ref, g_ref, o_ref, s0_ref, s1_ref, m_ref, l_ref, acc_ref, *, tq, tk, rt):
    qi = pl.program_id(2)
    nd = tq // tk
    nf = qi * nd
    base = qi * tq
    reps = tk // LANE
    s_refs = (s0_ref, s1_ref)
    m_ref[...] = jnp.full_like(m_ref, -jnp.inf)
    l_ref[...] = jnp.zeros_like(l_ref)
    acc_ref[...] = jnp.zeros_like(acc_ref)

    def scores(start, r0, slot):
        s_refs[slot][r0:, :] = _dot_nt(q_ref[r0:, :], k_ref[pl.ds(start, tk), :])

    def update(start, r0, slot, masked):
        vb = v_ref[pl.ds(start, tk), :]
        for t0 in range(r0, tq, rt):
            rows = slice(t0, t0 + rt)
            s = s_refs[slot][rows, :]
            if masked and t0 - r0 < tk:
                col = lax.broadcasted_iota(jnp.int32, s.shape, 1)
                row = lax.broadcasted_iota(jnp.int32, s.shape, 0)
                s = jnp.where(col <= row + (t0 - r0), s, -jnp.inf)
            m_old = m_ref[rows, :]
            m_new = jnp.maximum(m_old, jnp.max(s, axis=-1, keepdims=True))
            alpha = jnp.exp2(m_old - m_new)
            p = jnp.exp2(s - jnp.tile(m_new, (1, reps)))
            l_ref[rows, :] = alpha * l_ref[rows, :] + jnp.sum(p, axis=-1, keepdims=True)
            acc_ref[rows, :] = alpha * acc_ref[rows, :] + _dot(p.astype(BF16), vb)
            m_ref[rows, :] = m_new

    scores(0, 0, 0)

    def body(i, carry):
        for u in range(nd):
            j = i * nd + u
            scores(pl.multiple_of((j + 1) * tk, tk), 0, (u + 1) % 2)
            update(pl.multiple_of(j * tk, tk), 0, u % 2, False)
        return carry

    lax.fori_loop(0, qi, body, 0)
    for m in range(nd):
        if m + 1 < nd:
            scores(pl.multiple_of(base + (m + 1) * tk, tk), (m + 1) * tk, (m + 1) % 2)
        update(pl.multiple_of(base + m * tk, tk), m * tk, m % 2, True)
    o_ref[...] = (acc_ref[...] / l_ref[...] * _silu(g_ref[...].astype(F32))).astype(o_ref.dtype)


def mla_attention(qf, kf, v, gate, batch, seq, tq=2048, tk=512, rt=256):
    T = qf.shape[0]
    tq = min(tq, seq)
    tk = min(tk, tq)
    rt = min(rt, tk)
    nq = seq // tq
    assert (tq // tk) % 2 == 0 or nq == 1, "score-buffer parity is static only for an even block count per tile"
    return pl.pallas_call(
        functools.partial(_mla_attn_kernel, tq=tq, tk=tk, rt=rt),
        grid=(batch, MLA_HEADS, nq),
        in_specs=[
            pl.BlockSpec((tq, MLA_QK_PAD), lambda b, h, i: (b * nq + i, h)),
            pl.BlockSpec((seq, MLA_QK_PAD), lambda b, h, i: (b, h)),
            pl.BlockSpec((seq, MLA_V), lambda b, h, i: (b, h)),
            pl.BlockSpec((tq, MLA_V), lambda b, h, i: (b * nq + i, h)),
        ],
        out_specs=pl.BlockSpec((tq, MLA_V), lambda b, h, i: (b * nq + i, h)),
        out_shape=jax.ShapeDtypeStruct((T, MLA_HEADS * MLA_V), BF16),
        scratch_shapes=[pltpu.VMEM((tq, tk), F32), pltpu.VMEM((tq, tk), F32), pltpu.VMEM((tq, LANE), F32),
                        pltpu.VMEM((tq, LANE), F32), pltpu.VMEM((tq, MLA_V), F32)],
        compiler_params=_cparams(("parallel", "parallel", "arbitrary")),
        name="mla_attention",
    )(qf, kf, v, gate)


def _swap_halves(w):
    half = w.shape[-1] // 2
    return jnp.concatenate([w[..., half:], w[..., :half]], axis=-1)


def _prep_even(w_in, alpha_up, alpha_bias):
    col_scale = jnp.concatenate([jnp.full((SB_W,), -(SB_DIM ** -0.5) * LOG2E, F32), jnp.ones((EVEN_MAIN - SB_W,), F32)])
    w_main = (w_in[:, :EVEN_MAIN] * col_scale).astype(BF16)
    w_ga = jnp.pad(w_in[:, EVEN_MAIN:], ((0, 0), (0, LANE - GLA_RANK))).astype(BF16)
    au = jnp.pad(alpha_up, ((0, LANE - GLA_RANK), (0, 0)))
    return w_main, w_ga, au, alpha_bias.reshape(1, -1)


def _prep_odd(w_in, w_q_up, w_kv_up):
    d = w_in.shape[0]
    o1 = MLA_QR + MLA_KVR
    w_kr = w_in[:, o1:o1 + MLA_ROPE]
    w_lat = jnp.concatenate([w_in[:, :o1], w_kr, _swap_halves(w_kr)], axis=1).astype(BF16)
    w_gate = w_in[:, o1 + MLA_ROPE:].astype(BF16)
    wq = w_q_up.reshape(MLA_QR, MLA_HEADS, MLA_NOPE + MLA_ROPE)
    wq_r = wq[..., MLA_NOPE:]
    wq = jnp.concatenate([wq[..., :MLA_NOPE], wq_r, _swap_halves(wq_r)], axis=-1)
    wq = wq.reshape(MLA_QR, MLA_HEADS * MLA_QK_PAD).astype(BF16)
    wkv = w_kv_up.reshape(MLA_KVR, MLA_HEADS, MLA_NOPE + MLA_V)
    wkv = jnp.concatenate([wkv[..., :MLA_NOPE].reshape(MLA_KVR, -1), wkv[..., MLA_NOPE:].reshape(MLA_KVR, -1)], axis=1)
    del d
    return w_lat, w_gate, wq, wkv.astype(BF16)


def _even_layer(x, batch, seq, norm_g, w_in, alpha_up, alpha_bias, gla_norm_g, w_out, final_g, final):
    w_main, w_ga, au, ab = _prep_even(w_in, alpha_up, alpha_bias)
    z = norm_matmul(x, norm_g, w_main, BF16)
    ga = norm_matmul(x, norm_g, w_ga, F32)
    y_a = sb_attention(z, batch, seq)
    y_b = gla(z, ga, au, ab, gla_norm_g.reshape(1, -1), batch, seq)
    return out_proj(y_a, 0, y_b, 0, w_out.astype(BF16), x, final_g, final)


def _odd_layer(x, pos, inv, batch, seq, norm_g, w_in, q_norm_g, w_q_up, kv_norm_g, w_kv_up, w_out, final_g, final):
    w_lat, w_gate, wq, wkv = _prep_odd(w_in, w_q_up, w_kv_up)
    lat = norm_matmul(x, norm_g, w_lat, F32)
    gate = norm_matmul(x, norm_g, w_gate, BF16)
    qf, kf, v = mla_up(lat, pos, inv, q_norm_g, kv_norm_g, wq, wkv)
    y = mla_attention(qf, kf, v, gate, batch, seq)
    return out_proj(y, 0, y, 1, w_out.astype(BF16), x, final_g, final)


def kernel(x, positions, ln_even, w_in_even, gla_alpha_up, gla_alpha_bias, gla_norm, w_out_even,
           ln_odd, w_in_odd, q_norm, w_q_up, kv_norm, w_kv_up, w_out_odd, final_norm):
    batch, seq, d = x.shape
    depth = ln_even.shape[0] + ln_odd.shape[0]
    h = x.reshape(batch * seq, d)
    pos = positions.reshape(batch * seq, 1)
    half = MLA_ROPE // 2
    inv = ROPE_THETA ** (-jnp.arange(half, dtype=F32) / half)
    inv = jnp.concatenate([inv, inv, jnp.zeros((LANE - MLA_ROPE,), F32)]).reshape(1, LANE)
    for layer in range(depth):
        i = layer // 2
        final = layer == depth - 1
        if layer % 2 == 0:
            h = _even_layer(h, batch, seq, ln_even[i], w_in_even[i], gla_alpha_up[i], gla_alpha_bias[i],
                            gla_norm[i], w_out_even[i], final_norm, final)
        else:
            h = _odd_layer(h, pos, inv, batch, seq, ln_odd[i], w_in_odd[i], q_norm[i], w_q_up[i],
                           kv_norm[i], w_kv_up[i], w_out_odd[i], final_norm, final)
    return h.reshape(batch, seq, d)
```

```python
import functools

import jax
import jax.numpy as jnp
from jax import lax
from jax.experimental import pallas as pl
from jax.experimental.pallas import tpu as pltpu

F32 = jnp.float32
BF16 = jnp.bfloat16

SB_HEADS = 8
SB_DIM = 128
GLA_HEADS = 4
GLA_DK = 128
GLA_DV = 256
GLA_RANK = 16
GLA_GATE_NORM = 16.0
GLA_CHUNK = 64
GLA_SUB = 16
SB_SLOTS = 4
MLA_HEADS = 16
MLA_QR = 512
MLA_KVR = 512
MLA_NOPE = 128
MLA_ROPE = 64
MLA_V = 128
MLA_QK_PAD = 256
ROPE_THETA = 10000.0
EPS = 1e-6
LOG2E = 1.4426950408889634

LANE = 128
SUBLANE = 8
VMEM_LIMIT = 48 * 1024 * 1024

SB_W = SB_HEADS * SB_DIM
GLA_KW = GLA_HEADS * GLA_DK
GLA_VW = GLA_HEADS * GLA_DV
EVEN_MAIN = 4 * SB_W + 2 * GLA_KW + 2 * GLA_VW


def _cparams(sem):
    return pltpu.CompilerParams(dimension_semantics=sem, vmem_limit_bytes=VMEM_LIMIT)


def _rms(x, g):
    return x * lax.rsqrt(jnp.mean(x * x, axis=-1, keepdims=True) + EPS) * g


def _silu(g):
    return g * (1.0 / (1.0 + jnp.exp(-g)))


def _dot(a, b):
    return jnp.dot(a, b, preferred_element_type=F32)


def _hi_lo(x):
    hi = x.astype(BF16)
    return hi, (x - hi.astype(F32)).astype(BF16)


def _dot_nt(a, b):
    return lax.dot_general(a, b, (((1,), (1,)), ((), ())), preferred_element_type=F32)


def _dot_tn(a, b):
    return lax.dot_general(a, b, (((0,), (0,)), ((), ())), preferred_element_type=F32)


def _norm_matmul_kernel(x_ref, g_ref, w_ref, o_ref, h_ref):
    @pl.when(pl.program_id(1) == 0)
    def _():
        h_ref[...] = _rms(x_ref[...], g_ref[...]).astype(BF16)

    o_ref[...] = _dot(h_ref[...], w_ref[...]).astype(o_ref.dtype)


def _pick_tile(n, cap):
    best = LANE
    for t in range(LANE, cap + 1, LANE):
        if n % t == 0:
            best = t
    return best


def norm_matmul(x, g, w, out_dtype, tm=512, tn_cap=1024):
    T, K = x.shape
    N = w.shape[1]
    tn = _pick_tile(N, tn_cap)
    tm = min(tm, T)
    return pl.pallas_call(
        _norm_matmul_kernel,
        grid=(T // tm, N // tn),
        in_specs=[
            pl.BlockSpec((tm, K), lambda i, j: (i, 0)),
            pl.BlockSpec((1, K), lambda i, j: (0, 0)),
            pl.BlockSpec((K, tn), lambda i, j: (0, j)),
        ],
        out_specs=pl.BlockSpec((tm, tn), lambda i, j: (i, j)),
        out_shape=jax.ShapeDtypeStruct((T, N), out_dtype),
        scratch_shapes=[pltpu.VMEM((tm, K), BF16)],
        compiler_params=_cparams(("parallel", "arbitrary")),
        name="norm_matmul",
    )(x, g.reshape(1, K), w)


def _sb_kernel(q_ref, k_ref, v_ref, g_ref, o_ref, *scratch, tq, tk, rt):
    qi = pl.program_id(2)
    nd = tq // tk
    ns = SB_SLOTS
    nf = qi * nd
    nzw_b = scratch[0:ns // 2]
    ls_b, lb_b, tot_b = (scratch[ns // 2 + i * ns:ns // 2 + (i + 1) * ns] for i in range(3))
    acc_ref = scratch[ns // 2 + 3 * ns]
    krow = lax.broadcasted_iota(jnp.int32, (tk, tk), 0)
    kcol = lax.broadcasted_iota(jnp.int32, (tk, tk), 1)
    later = (krow > kcol).astype(BF16)
    acc_ref[...] = jnp.zeros_like(acc_ref)
    nzw_b[ns // 2 - 1][:, tk:2 * tk] = jnp.full((tq, tk), jnp.inf, F32)
    ls_b[ns - 2][...] = jnp.zeros((tq, tk), BF16)
    lb_b[ns - 2][...] = jnp.full((tq, tk), -jnp.inf, F32)
    tot_b[ns - 2][...] = jnp.zeros((tq, LANE), F32)

    def stage_a(start, r0, slot):
        nzw_b[slot // 2][r0:, :] = _dot_nt(q_ref[r0:, :], k_ref[pl.ds(start, 2 * tk), :])

    def stage_b(r0, slot, masked):
        for t0 in range(r0, tq, rt):
            rows = slice(t0, t0 + rt)
            nz = nzw_b[slot // 2][rows, (slot % 2) * tk:(slot % 2 + 1) * tk]
            neg_abs = lax.bitcast_convert_type(lax.bitcast_convert_type(nz, jnp.uint32) | jnp.uint32(0x80000000), F32)
            log_stay = jnp.minimum(nz, 0.0) - jnp.log(1.0 + jnp.exp2(neg_abs)) * LOG2E
            log_beta = log_stay - nz
            if masked and t0 - r0 < tk:
                col = lax.broadcasted_iota(jnp.int32, nz.shape, 1)
                row = lax.broadcasted_iota(jnp.int32, nz.shape, 0)
                keep = col < row + (t0 - r0)
                log_stay = jnp.where(keep, log_stay, 0.0)
                log_beta = jnp.where(keep, log_beta, -jnp.inf)
            ls_b[slot][rows, :] = log_stay.astype(BF16)
            lb_b[slot][rows, :] = log_beta
            tot_b[slot][rows, :] = jnp.broadcast_to(jnp.sum(log_stay, axis=-1, keepdims=True), (rt, LANE))

    def stage_c(start, r0, slot):
        vb = v_ref[pl.ds(start, tk), :]
        for t0 in range(r0, tq, rt):
            rows = slice(t0, t0 + rt)
            within = _dot(ls_b[slot][rows, :], later)
            w = jnp.exp2(lb_b[slot][rows, :] + within)
            acc_ref[rows, :] = acc_ref[rows, :] * jnp.exp2(tot_b[slot][rows, :]) + _dot(w.astype(BF16), vb)

    def body(t, carry):
        for u in range(nd):
            g = t * nd + u
            if u % 2 == 0:
                stage_a(pl.multiple_of(g * tk, tk), 0, u % ns)
            stage_b(0, (u - 1) % ns, False)
            stage_c(pl.multiple_of(jnp.maximum(g - 2, 0) * tk, tk), 0, (u - 2) % ns)
        return carry

    lax.fori_loop(0, qi, body, 0)
    for e in range(nd + 2):
        if e < nd and e % 2 == 0:
            stage_a(pl.multiple_of((nf + e) * tk, tk), e * tk, e % ns)
        if 1 <= e <= nd:
            stage_b((e - 1) * tk, (e - 1) % ns, True)
        elif e == 0:
            stage_b(0, ns - 1, False)
        if e >= 2:
            stage_c(pl.multiple_of((nf + e - 2) * tk, tk), (e - 2) * tk, (e - 2) % ns)
        else:
            stage_c(pl.multiple_of(jnp.maximum(nf + e - 2, 0) * tk, tk), 0, (e - 2) % ns)
    o_ref[...] = (acc_ref[...] * _silu(g_ref[...].astype(F32))).astype(o_ref.dtype)


def sb_attention(z, batch, seq, tq=1024, tk=256, rt=256):
    T = z.shape[0]
    tq = min(tq, seq)
    tk = min(tk, tq)
    rt = min(rt, tk)
    nq = seq // tq
    ns = SB_SLOTS
    assert (tq // tk) % ns == 0, "slot of a block must not depend on the loop trip"
    hw = SB_W // SB_DIM
    scratch = ([pltpu.VMEM((tq, 2 * tk), F32)] * (ns // 2) + [pltpu.VMEM((tq, tk), BF16)] * ns + [pltpu.VMEM((tq, tk), F32)] * ns
               + [pltpu.VMEM((tq, LANE), F32)] * ns + [pltpu.VMEM((tq, SB_DIM), F32)])
    return pl.pallas_call(
        functools.partial(_sb_kernel, tq=tq, tk=tk, rt=rt),
        grid=(batch, SB_HEADS, nq),
        in_specs=[
            pl.BlockSpec((tq, SB_DIM), lambda b, h, i: (b * nq + i, h)),
            pl.BlockSpec((seq, SB_DIM), lambda b, h, i: (b, hw + h)),
            pl.BlockSpec((seq, SB_DIM), lambda b, h, i: (b, 2 * hw + h)),
            pl.BlockSpec((tq, SB_DIM), lambda b, h, i: (b * nq + i, 3 * hw + h)),
        ],
        out_specs=pl.BlockSpec((tq, SB_DIM), lambda b, h, i: (b * nq + i, h)),
        out_shape=jax.ShapeDtypeStruct((T, SB_W), BF16),
        scratch_shapes=scratch,
        compiler_params=_cparams(("parallel", "parallel", "arbitrary")),
        name="sb_attention",
    )(z, z, z, z)


def _gla_kernel(q_ref, k_ref, v_ref, gg_ref, ga_ref, au_ref, ab_ref, ng_ref, o_ref, st_ref, incl_ref, *, nchunk):
    C, SUB, G = GLA_CHUNK, GLA_SUB, SUBLANE
    tb = nchunk * C

    @pl.when(pl.program_id(2) == 0)
    def _():
        st_ref[...] = jnp.zeros_like(st_ref)
        ri = lax.broadcasted_iota(jnp.int32, (tb, tb), 0)
        ci = lax.broadcasted_iota(jnp.int32, (tb, tb), 1)
        incl_ref[...] = jnp.where(ci <= ri, jnp.where(ci >= (ri & ~(C - 1)), 1.0, 0.0), 0.0).astype(BF16)

    grp_ri = lax.broadcasted_iota(jnp.int32, (G, C), 0)
    grp_ci = lax.broadcasted_iota(jnp.int32, (G, C), 1)

    g_hi, g_lo = _hi_lo(ga_ref[...])
    a_hi, a_lo = _hi_lo(au_ref[...])
    pre = _dot(g_hi, a_hi) + _dot(g_hi, a_lo) + _dot(g_lo, a_hi) + ab_ref[...]
    log_f = -(jnp.maximum(-pre, 0.0) + jnp.log1p(jnp.exp(-jnp.abs(pre)))) / GLA_GATE_NORM
    f_hi = log_f.astype(BF16)
    f_mid, f_lo = _hi_lo(log_f - f_hi.astype(F32))
    incl = incl_ref[...]
    b_all = (_dot(incl, f_hi) + _dot(incl, f_mid) + _dot(incl, f_lo)) * LOG2E

    st = st_ref[...]
    for c in range(nchunk):
        r0 = c * C
        q = q_ref[r0:r0 + C, :].astype(F32) * (GLA_DK ** -0.5)
        k = k_ref[r0:r0 + C, :].astype(F32)
        v = v_ref[r0:r0 + C, :]
        b = b_all[r0:r0 + C, :]
        b_last = b[C - 1:C, :]

        o_inter = _dot_nt((q * jnp.exp2(b)).astype(BF16), st.astype(BF16))

        att_rows = []
        for s in range(C // SUB):
            i0 = s * SUB
            if s > 0:
                b0 = b[i0:i0 + 1, :]
                q_dec = (q[i0:i0 + SUB, :] * jnp.exp2(b[i0:i0 + SUB, :] - b0)).astype(BF16)
                k_dec = (k * jnp.exp2(jnp.minimum(b0 - b, 0.0))).astype(BF16)
                att_far = _dot_nt(q_dec, k_dec)
            for g0 in range(i0, i0 + SUB, G):
                bg = b[g0:g0 + G, :]
                qg = q[g0:g0 + G, :]
                att = jnp.where(grp_ci < i0, att_far[g0 - i0:g0 - i0 + G, :], 0.0) if s > 0 else jnp.zeros((G, C), F32)
                for jj in range(i0, g0 + G):
                    a = jnp.sum(qg * k[jj:jj + 1, :] * jnp.exp2(bg - b[jj:jj + 1, :]), axis=-1, keepdims=True)
                    att = jnp.where(grp_ci == jj, a, att)
                att_rows.append(jnp.where(grp_ci <= grp_ri + g0, att, 0.0))
        att = jnp.concatenate(att_rows, axis=0)
        o = o_inter + _dot(att.astype(BF16), v)

        k_dec = (k * jnp.exp2(b_last - b)).astype(BF16)
        st = st * jnp.exp2(b_last) + _dot_tn(v, k_dec)

        y = _rms(o, ng_ref[...]) * _silu(gg_ref[r0:r0 + C, :].astype(F32))
        o_ref[r0:r0 + C, :] = y.astype(o_ref.dtype)
    st_ref[...] = st


def gla(z, ga, au, ab, ng, batch, seq, tb=512):
    T = z.shape[0]
    tb = min(tb, seq)
    nb = seq // tb
    q0 = 4 * SB_W // GLA_DK
    k0 = q0 + GLA_HEADS
    v0 = (4 * SB_W + 2 * GLA_KW) // GLA_DV
    g0 = v0 + GLA_HEADS
    return pl.pallas_call(
        functools.partial(_gla_kernel, nchunk=tb // GLA_CHUNK),
        grid=(batch, GLA_HEADS, nb),
        in_specs=[
            pl.BlockSpec((tb, GLA_DK), lambda b, h, i: (b * nb + i, q0 + h)),
            pl.BlockSpec((tb, GLA_DK), lambda b, h, i: (b * nb + i, k0 + h)),
            pl.BlockSpec((tb, GLA_DV), lambda b, h, i: (b * nb + i, v0 + h)),
            pl.BlockSpec((tb, GLA_DV), lambda b, h, i: (b * nb + i, g0 + h)),
            pl.BlockSpec((tb, LANE), lambda b, h, i: (b * nb + i, 0)),
            pl.BlockSpec((LANE, GLA_DK), lambda b, h, i: (0, h)),
            pl.BlockSpec((1, GLA_DK), lambda b, h, i: (0, h)),
            pl.BlockSpec((1, GLA_DV), lambda b, h, i: (0, 0)),
        ],
        out_specs=pl.BlockSpec((tb, GLA_DV), lambda b, h, i: (b * nb + i, h)),
        out_shape=jax.ShapeDtypeStruct((T, GLA_VW), BF16),
        scratch_shapes=[pltpu.VMEM((GLA_DV, GLA_DK), F32), pltpu.VMEM((tb, tb), BF16)],
        compiler_params=_cparams(("parallel", "parallel", "arbitrary")),
        name="gla",
    )(z, z, z, z, ga, au, ab, ng)


def _out_proj_kernel(y1_ref, y2_ref, w1_ref, w2_ref, x_ref, gf_ref, o_ref, *, final):
    acc = x_ref[...] + _dot(y1_ref[...], w1_ref[...]) + _dot(y2_ref[...], w2_ref[...])
    if final:
        acc = _rms(acc, gf_ref[...])
    o_ref[...] = acc


def out_proj(y1, c1, y2, c2, w, x, gf, final, tm=256):
    T, D = x.shape
    kh = w.shape[0] // 2
    tm = min(tm, T)
    return pl.pallas_call(
        functools.partial(_out_proj_kernel, final=final),
        grid=(T // tm,),
        in_specs=[
            pl.BlockSpec((tm, kh), lambda i: (i, c1)),
            pl.BlockSpec((tm, kh), lambda i: (i, c2)),
            pl.BlockSpec((kh, D), lambda i: (0, 0)),
            pl.BlockSpec((kh, D), lambda i: (1, 0)),
            pl.BlockSpec((tm, D), lambda i: (i, 0)),
            pl.BlockSpec((1, D), lambda i: (0, 0)),
        ],
        out_specs=pl.BlockSpec((tm, D), lambda i: (i, 0)),
        out_shape=jax.ShapeDtypeStruct((T, D), F32),
        compiler_params=_cparams(("parallel",)),
        name="out_proj",
    )(y1, y2, w, w, x, gf.reshape(1, D))


def _mla_up_kernel(lat_ref, pos_ref, inv_ref, qg_ref, kvg_ref, wq_ref, wkv_ref, q_ref, k_ref, v_ref, *, scale):
    hq = _rms(lat_ref[:, 0:MLA_QR], qg_ref[...]).astype(BF16)
    hkv = _rms(lat_ref[:, MLA_QR:MLA_QR + MLA_KVR], kvg_ref[...]).astype(BF16)
    ang = pos_ref[...].astype(F32) * inv_ref[...]
    lane = lax.broadcasted_iota(jnp.int32, ang.shape, 1)
    half = MLA_ROPE // 2
    cos2 = jnp.where(lane < MLA_ROPE, jnp.cos(ang), 0.0)
    sin2 = jnp.where(lane < half, -jnp.sin(ang), jnp.where(lane < MLA_ROPE, jnp.sin(ang), 0.0))

    def rope(r):
        return r * cos2 + pltpu.roll(r, MLA_ROPE, 1) * sin2

    kr = rope(lat_ref[:, MLA_QR + MLA_KVR:MLA_QR + MLA_KVR + LANE]).astype(BF16)
    kn = _dot(hkv, wkv_ref[:, 0:MLA_HEADS * MLA_NOPE]).astype(BF16)
    v_ref[...] = _dot(hkv, wkv_ref[:, MLA_HEADS * MLA_NOPE:]).astype(BF16)
    for h in range(MLA_HEADS):
        c0 = h * MLA_QK_PAD
        y = _dot(hq, wq_ref[:, c0:c0 + MLA_QK_PAD])
        q_ref[:, c0:c0 + MLA_NOPE] = (y[:, 0:MLA_NOPE] * scale).astype(BF16)
        q_ref[:, c0 + MLA_NOPE:c0 + MLA_QK_PAD] = (rope(y[:, MLA_NOPE:]) * scale).astype(BF16)
        k_ref[:, c0:c0 + MLA_NOPE] = kn[:, h * MLA_NOPE:(h + 1) * MLA_NOPE]
        k_ref[:, c0 + MLA_NOPE:c0 + MLA_QK_PAD] = kr


def mla_up(lat, pos, inv, qg, kvg, wq, wkv, tm=256):
    T = lat.shape[0]
    tm = min(tm, T)
    qk_w = MLA_HEADS * MLA_QK_PAD
    v_w = MLA_HEADS * MLA_V
    return pl.pallas_call(
        functools.partial(_mla_up_kernel, scale=(MLA_NOPE + MLA_ROPE) ** -0.5 * LOG2E),
        grid=(T // tm,),
        in_specs=[
            pl.BlockSpec((tm, lat.shape[1]), lambda i: (i, 0)),
            pl.BlockSpec((tm, 1), lambda i: (i, 0)),
            pl.BlockSpec((1, LANE), lambda i: (0, 0)),
            pl.BlockSpec((1, MLA_QR), lambda i: (0, 0)),
            pl.BlockSpec((1, MLA_KVR), lambda i: (0, 0)),
            pl.BlockSpec(wq.shape, lambda i: (0, 0)),
            pl.BlockSpec(wkv.shape, lambda i: (0, 0)),
        ],
        out_specs=[
            pl.BlockSpec((tm, qk_w), lambda i: (i, 0)),
            pl.BlockSpec((tm, qk_w), lambda i: (i, 0)),
            pl.BlockSpec((tm, v_w), lambda i: (i, 0)),
        ],
        out_shape=[
            jax.ShapeDtypeStruct((T, qk_w), BF16),
            jax.ShapeDtypeStruct((T, qk_w), BF16),
            jax.ShapeDtypeStruct((T, v_w), BF16),
        ],
        compiler_params=_cparams(("parallel",)),
        name="mla_up",
    )(lat, pos, inv, qg.reshape(1, -1), kvg.reshape(1, -1), wq, wkv)


def _mla_attn_kernel(q_ref, k_ref, v_ref, g_ref, o_ref, s0_ref, s1_ref, m_ref, l_ref, acc_ref, *, tq, tk, rt):
    qi = pl.program_id(2)
    nd = tq // tk
    nf = qi * nd
    base = qi * tq
    reps = tk // LANE
    s_refs = (s0_ref, s1_ref)
    m_ref[...] = jnp.full_like(m_ref, -jnp.inf)
    l_ref[...] = jnp.zeros_like(l_ref)
    acc_ref[...] = jnp.zeros_like(acc_ref)

    def scores(start, r0, slot, r1=tq):
        s_refs[slot][r0:r1, :] = _dot_nt(q_ref[r0:r1, :], k_ref[pl.ds(start, tk), :])

    def update(start, r0, slot, masked):
        vb = v_ref[pl.ds(start, tk), :]
        for t0 in range(r0, tq, rt):
            rows = slice(t0, t0 + rt)
            s = s_refs[slot][rows, :]
            if masked and t0 - r0 < tk:
                col = lax.broadcasted_iota(jnp.int32, s.shape, 1)
                row = lax.broadcasted_iota(jnp.int32, s.shape, 0)
                s = jnp.where(col <= row + (t0 - r0), s, -jnp.inf)
            m_old = m_ref[rows, :]
            m_new = jnp.maximum(m_old, jnp.max(s, axis=-1, keepdims=True))
            alpha = jnp.exp2(m_old - m_new)
            p = jnp.exp2(s - jnp.tile(m_new, (1, reps)))
            l_ref[rows, :] = alpha * l_ref[rows, :] + jnp.sum(p, axis=-1, keepdims=True)
            acc_ref[rows, :] = alpha * acc_ref[rows, :] + _dot(p.astype(BF16), vb)
            m_ref[rows, :] = m_new

    scores(0, 0, 0, rt)

    def body(i, carry):
        for u in range(nd):
            j = i * nd + u
            if u == 0:
                scores(pl.multiple_of(j * tk, tk), rt, 0)
            scores(pl.multiple_of((j + 1) * tk, tk), 0, (u + 1) % 2, tq if u + 1 < nd else rt)
            update(pl.multiple_of(j * tk, tk), 0, u % 2, False)
        return carry

    lax.fori_loop(0, qi, body, 0)
    scores(pl.multiple_of(base, tk), rt, 0)
    for m in range(nd):
        if m + 1 < nd:
            scores(pl.multiple_of(base + (m + 1) * tk, tk), (m + 1) * tk, (m + 1) % 2)
        update(pl.multiple_of(base + m * tk, tk), m * tk, m % 2, True)
    o_ref[...] = (acc_ref[...] / l_ref[...] * _silu(g_ref[...].astype(F32))).astype(o_ref.dtype)


def mla_attention(qf, kf, v, gate, batch, seq, tq=2048, tk=512, rt=256):
    T = qf.shape[0]
    tq = min(tq, seq)
    tk = min(tk, tq)
    rt = min(rt, tk)
    nq = seq // tq
    assert (tq // tk) % 2 == 0 or nq == 1, "score-buffer parity is static only for an even block count per tile"
    return pl.pallas_call(
        functools.partial(_mla_attn_kernel, tq=tq, tk=tk, rt=rt),
        grid=(batch, MLA_HEADS, nq),
        in_specs=[
            pl.BlockSpec((tq, MLA_QK_PAD), lambda b, h, i: (b * nq + i, h)),
            pl.BlockSpec((seq, MLA_QK_PAD), lambda b, h, i: (b, h)),
            pl.BlockSpec((seq, MLA_V), lambda b, h, i: (b, h)),
            pl.BlockSpec((tq, MLA_V), lambda b, h, i: (b * nq + i, h)),
        ],
        out_specs=pl.BlockSpec((tq, MLA_V), lambda b, h, i: (b * nq + i, h)),
        out_shape=jax.ShapeDtypeStruct((T, MLA_HEADS * MLA_V), BF16),
        scratch_shapes=[pltpu.VMEM((tq, tk), F32), pltpu.VMEM((tq, tk), F32), pltpu.VMEM((tq, LANE), F32),
                        pltpu.VMEM((tq, LANE), F32), pltpu.VMEM((tq, MLA_V), F32)],
        compiler_params=_cparams(("parallel", "parallel", "arbitrary")),
        name="mla_attention",
    )(qf, kf, v, gate)


def _swap_halves(w):
    half = w.shape[-1] // 2
    return jnp.concatenate([w[..., half:], w[..., :half]], axis=-1)


def _prep_even(w_in, alpha_up, alpha_bias):
    col_scale = jnp.concatenate([jnp.full((SB_W,), -(SB_DIM ** -0.5) * LOG2E, F32), jnp.ones((EVEN_MAIN - SB_W,), F32)])
    w_main = (w_in[:, :EVEN_MAIN] * col_scale).astype(BF16)
    w_ga = jnp.pad(w_in[:, EVEN_MAIN:], ((0, 0), (0, LANE - GLA_RANK))).astype(BF16)
    au = jnp.pad(alpha_up, ((0, LANE - GLA_RANK), (0, 0)))
    return w_main, w_ga, au, alpha_bias.reshape(1, -1)


def _prep_odd(w_in, w_q_up, w_kv_up):
    d = w_in.shape[0]
    o1 = MLA_QR + MLA_KVR
    w_kr = w_in[:, o1:o1 + MLA_ROPE]
    w_lat = jnp.concatenate([w_in[:, :o1], w_kr, _swap_halves(w_kr)], axis=1).astype(BF16)
    w_gate = w_in[:, o1 + MLA_ROPE:].astype(BF16)
    wq = w_q_up.reshape(MLA_QR, MLA_HEADS, MLA_NOPE + MLA_ROPE)
    wq_r = wq[..., MLA_NOPE:]
    wq = jnp.concatenate([wq[..., :MLA_NOPE], wq_r, _swap_halves(wq_r)], axis=-1)
    wq = wq.reshape(MLA_QR, MLA_HEADS * MLA_QK_PAD).astype(BF16)
    wkv = w_kv_up.reshape(MLA_KVR, MLA_HEADS, MLA_NOPE + MLA_V)
    wkv = jnp.concatenate([wkv[..., :MLA_NOPE].reshape(MLA_KVR, -1), wkv[..., MLA_NOPE:].reshape(MLA_KVR, -1)], axis=1)
    del d
    return w_lat, w_gate, wq, wkv.astype(BF16)


def _even_layer(x, batch, seq, norm_g, w_in, alpha_up, alpha_bias, gla_norm_g, w_out, final_g, final):
    w_main, w_ga, au, ab = _prep_even(w_in, alpha_up, alpha_bias)
    z = norm_matmul(x, norm_g, w_main, BF16)
    ga = norm_matmul(x, norm_g, w_ga, F32)
    y_a = sb_attention(z, batch, seq)
    y_b = gla(z, ga, au, ab, gla_norm_g.reshape(1, -1), batch, seq)
    return out_proj(y_a, 0, y_b, 0, w_out.astype(BF16), x, final_g, final)


def _odd_layer(x, pos, inv, batch, seq, norm_g, w_in, q_norm_g, w_q_up, kv_norm_g, w_kv_up, w_out, final_g, final):
    w_lat, w_gate, wq, wkv = _prep_odd(w_in, w_q_up, w_kv_up)
    lat = norm_matmul(x, norm_g, w_lat, F32)
    gate = norm_matmul(x, norm_g, w_gate, BF16)
    qf, kf, v = mla_up(lat, pos, inv, q_norm_g, kv_norm_g, wq, wkv)
    y = mla_attention(qf, kf, v, gate, batch, seq)
    return out_proj(y, 0, y, 1, w_out.astype(BF16), x, final_g, final)


def kernel(x, positions, ln_even, w_in_even, gla_alpha_up, gla_alpha_bias, gla_norm, w_out_even,
           ln_odd, w_in_odd, q_norm, w_q_up, kv_norm, w_kv_up, w_out_odd, final_norm):
    batch, seq, d = x.shape
    depth = ln_even.shape[0] + ln_odd.shape[0]
    h = x.reshape(batch * seq, d)
    pos = positions.reshape(batch * seq, 1)
    half = MLA_ROPE // 2
    inv = ROPE_THETA ** (-jnp.arange(half, dtype=F32) / half)
    inv = jnp.concatenate([inv, inv, jnp.zeros((LANE - MLA_ROPE,), F32)]).reshape(1, LANE)
    for layer in range(depth):
        i = layer // 2
        final = layer == depth - 1
        if layer % 2 == 0:
            h = _even_layer(h, batch, seq, ln_even[i], w_in_even[i], gla_alpha_up[i], gla_alpha_bias[i],
                            gla_norm[i], w_out_even[i], final_norm, final)
        else:
            h = _odd_layer(h, pos, inv, batch, seq, ln_odd[i], w_in_odd[i], q_norm[i], w_q_up[i],
                           kv_norm[i], w_kv_up[i], w_out_odd[i], final_norm, final)
    return h.reshape(batch, seq, d)
```

```python
import functools

import jax
import jax.numpy as jnp
from jax import lax
from jax.experimental import pallas as pl
from jax.experimental.pallas import tpu as pltpu

F32 = jnp.float32
BF16 = jnp.bfloat16

SB_HEADS = 8
SB_DIM = 128
GLA_HEADS = 4
GLA_DK = 128
GLA_DV = 256
GLA_RANK = 16
GLA_GATE_NORM = 16.0
GLA_CHUNK = 64
GLA_SUB = 16
SB_SLOTS = 4
MLA_HEADS = 16
MLA_QR = 512
MLA_KVR = 512
MLA_NOPE = 128
MLA_ROPE = 64
MLA_V = 128
MLA_QK_PAD = 256
ROPE_THETA = 10000.0
EPS = 1e-6
LOG2E = 1.4426950408889634

LANE = 128
SUBLANE = 8
VMEM_LIMIT = 48 * 1024 * 1024

SB_W = SB_HEADS * SB_DIM
GLA_KW = GLA_HEADS * GLA_DK
GLA_VW = GLA_HEADS * GLA_DV
EVEN_MAIN = 4 * SB_W + 2 * GLA_KW + 2 * GLA_VW


def _cparams(sem):
    return pltpu.CompilerParams(dimension_semantics=sem, vmem_limit_bytes=VMEM_LIMIT)


def _rms(x, g):
    return x * lax.rsqrt(jnp.mean(x * x, axis=-1, keepdims=True) + EPS) * g


def _silu(g):
    return g * (1.0 / (1.0 + jnp.exp(-g)))


def _dot(a, b):
    return jnp.dot(a, b, preferred_element_type=F32)


def _hi_lo(x):
    hi = x.astype(BF16)
    return hi, (x - hi.astype(F32)).astype(BF16)


def _dot_nt(a, b):
    return lax.dot_general(a, b, (((1,), (1,)), ((), ())), preferred_element_type=F32)


def _dot_tn(a, b):
    return lax.dot_general(a, b, (((0,), (0,)), ((), ())), preferred_element_type=F32)


def _norm_matmul_kernel(x_ref, g_ref, w_ref, ws_ref, o_ref, os_ref, h_ref):
    @pl.when(pl.program_id(1) == 0)
    def _():
        h = _rms(x_ref[...], g_ref[...]).astype(BF16)
        h_ref[...] = h
        os_ref[...] = _dot(h, ws_ref[...]).astype(os_ref.dtype)

    o_ref[...] = _dot(h_ref[...], w_ref[...]).astype(o_ref.dtype)


def _pick_tile(n, cap):
    best = LANE
    for t in range(LANE, cap + 1, LANE):
        if n % t == 0:
            best = t
    return best


def norm_matmul(x, g, w, out_dtype, w_side, side_dtype, tm, tn_cap=1024):
    T, K = x.shape
    N = w.shape[1]
    ns = w_side.shape[1]
    tn = _pick_tile(N, tn_cap)
    tm = min(tm, T)
    return pl.pallas_call(
        _norm_matmul_kernel,
        grid=(T // tm, N // tn),
        in_specs=[
            pl.BlockSpec((tm, K), lambda i, j: (i, 0)),
            pl.BlockSpec((1, K), lambda i, j: (0, 0)),
            pl.BlockSpec((K, tn), lambda i, j: (0, j)),
            pl.BlockSpec((K, ns), lambda i, j: (0, 0)),
        ],
        out_specs=[
            pl.BlockSpec((tm, tn), lambda i, j: (i, j)),
            pl.BlockSpec((tm, ns), lambda i, j: (i, 0)),
        ],
        out_shape=[jax.ShapeDtypeStruct((T, N), out_dtype), jax.ShapeDtypeStruct((T, ns), side_dtype)],
        scratch_shapes=[pltpu.VMEM((tm, K), BF16)],
        compiler_params=_cparams(("parallel", "arbitrary")),
        name="norm_matmul",
    )(x, g.reshape(1, K), w, w_side)


def _sb_kernel(q_ref, k_ref, v_ref, g_ref, o_ref, *scratch, tq, tk, rt):
    qi = pl.program_id(2)
    nd = tq // tk
    ns = SB_SLOTS
    nf = qi * nd
    nzw_b = scratch[0:ns // 2]
    ls_b, tot_b = (scratch[ns // 2 + i * ns:ns // 2 + (i + 1) * ns] for i in range(2))
    acc_ref = scratch[ns // 2 + 2 * ns]
    krow = lax.broadcasted_iota(jnp.int32, (tk, tk), 0)
    kcol = lax.broadcasted_iota(jnp.int32, (tk, tk), 1)
    from_s = (krow >= kcol).astype(BF16)
    acc_ref[...] = jnp.zeros_like(acc_ref)
    nzw_b[ns // 2 - 1][...] = jnp.full((tq, 2 * tk), jnp.inf, F32)
    ls_b[ns - 2][...] = jnp.zeros((tq, tk), BF16)
    tot_b[ns - 2][...] = jnp.zeros((tq, LANE), F32)

    def nz_view(slot, rows):
        return nzw_b[slot // 2].at[rows, (slot % 2) * tk:(slot % 2 + 1) * tk]

    def stage_a(start, r0, slot):
        nzw_b[slot // 2][r0:, :] = _dot_nt(q_ref[r0:, :], k_ref[pl.ds(start, 2 * tk), :])

    def stage_b(r0, slot, masked):
        for t0 in range(r0, tq, rt):
            rows = slice(t0, t0 + rt)
            nz = nz_view(slot, rows)[...]
            neg_abs = lax.bitcast_convert_type(lax.bitcast_convert_type(nz, jnp.uint32) | jnp.uint32(0x80000000), F32)
            log_stay = jnp.minimum(nz, 0.0) - jnp.log(1.0 + jnp.exp2(neg_abs)) * LOG2E
            if masked and t0 - r0 < tk:
                col = lax.broadcasted_iota(jnp.int32, nz.shape, 1)
                row = lax.broadcasted_iota(jnp.int32, nz.shape, 0)
                keep = col < row + (t0 - r0)
                log_stay = jnp.where(keep, log_stay, 0.0)
                nz_view(slot, rows)[...] = jnp.where(keep, nz, jnp.inf)
            ls_b[slot][rows, :] = log_stay.astype(BF16)
            tot_b[slot][rows, :] = jnp.broadcast_to(jnp.sum(log_stay, axis=-1, keepdims=True), (rt, LANE))

    def stage_c(start, r0, slot):
        vb = v_ref[pl.ds(start, tk), :]
        for t0 in range(r0, tq, rt):
            rows = slice(t0, t0 + rt)
            upto = _dot(ls_b[slot][rows, :], from_s)
            w = jnp.exp2(upto - nz_view(slot, rows)[...])
            acc_ref[rows, :] = acc_ref[rows, :] * jnp.exp2(tot_b[slot][rows, :]) + _dot(w.astype(BF16), vb)

    def body(t, carry):
        for u in range(nd):
            g = t * nd + u
            if u % 2 == 0:
                stage_a(pl.multiple_of(g * tk, tk), 0, u % ns)
            stage_b(0, (u - 1) % ns, False)
            stage_c(pl.multiple_of(jnp.maximum(g - 2, 0) * tk, tk), 0, (u - 2) % ns)
        return carry

    lax.fori_loop(0, qi, body, 0)
    for e in range(nd + 2):
        if e < nd and e % 2 == 0:
            stage_a(pl.multiple_of((nf + e) * tk, tk), e * tk, e % ns)
        if 1 <= e <= nd:
            stage_b((e - 1) * tk, (e - 1) % ns, True)
        elif e == 0:
            stage_b(0, ns - 1, False)
        if e >= 2:
            stage_c(pl.multiple_of((nf + e - 2) * tk, tk), (e - 2) * tk, (e - 2) % ns)
        else:
            stage_c(pl.multiple_of(jnp.maximum(nf + e - 2, 0) * tk, tk), 0, (e - 2) % ns)
    o_ref[...] = (acc_ref[...] * _silu(g_ref[...].astype(F32))).astype(o_ref.dtype)


def sb_attention(z, batch, seq, tq=1024, tk=256, rt=256):
    T = z.shape[0]
    tq = min(tq, seq)
    tk = min(tk, tq)
    rt = min(rt, tk)
    nq = seq // tq
    ns = SB_SLOTS
    assert (tq // tk) % ns == 0, "slot of a block must not depend on the loop trip"
    hw = SB_W // SB_DIM
    scratch = ([pltpu.VMEM((tq, 2 * tk), F32)] * (ns // 2) + [pltpu.VMEM((tq, tk), BF16)] * ns
               + [pltpu.VMEM((tq, LANE), F32)] * ns + [pltpu.VMEM((tq, SB_DIM), F32)])
    return pl.pallas_call(
        functools.partial(_sb_kernel, tq=tq, tk=tk, rt=rt),
        grid=(batch, SB_HEADS, nq),
        in_specs=[
            pl.BlockSpec((tq, SB_DIM), lambda b, h, i: (b * nq + i, h)),
            pl.BlockSpec((seq, SB_DIM), lambda b, h, i: (b, hw + h)),
            pl.BlockSpec((seq, SB_DIM), lambda b, h, i: (b, 2 * hw + h)),
            pl.BlockSpec((tq, SB_DIM), lambda b, h, i: (b * nq + i, 3 * hw + h)),
        ],
        out_specs=pl.BlockSpec((tq, SB_DIM), lambda b, h, i: (b * nq + i, h)),
        out_shape=jax.ShapeDtypeStruct((T, SB_W), BF16),
        scratch_shapes=scratch,
        compiler_params=_cparams(("parallel", "parallel", "arbitrary")),
        name="sb_attention",
    )(z, z, z, z)


def _gla_kernel(q_ref, k_ref, v_ref, gg_ref, ga_ref, au_ref, ab_ref, ng_ref, o_ref, st_ref, incl_ref, *, nchunk):
    C, SUB, G = GLA_CHUNK, GLA_SUB, SUBLANE
    tb = nchunk * C

    @pl.when(pl.program_id(2) == 0)
    def _():
        st_ref[...] = jnp.zeros_like(st_ref)
        ri = lax.broadcasted_iota(jnp.int32, (tb, tb), 0)
        ci = lax.broadcasted_iota(jnp.int32, (tb, tb), 1)
        incl_ref[...] = jnp.where(ci <= ri, jnp.where(ci >= (ri & ~(C - 1)), 1.0, 0.0), 0.0).astype(BF16)

    grp_ri = lax.broadcasted_iota(jnp.int32, (G, C), 0)
    grp_ci = lax.broadcasted_iota(jnp.int32, (G, C), 1)

    g_hi, g_lo = _hi_lo(ga_ref[...])
    a_hi, a_lo = _hi_lo(au_ref[...])
    pre = _dot(g_hi, a_hi) + _dot(g_hi, a_lo) + _dot(g_lo, a_hi) + ab_ref[...]
    log_f = -(jnp.maximum(-pre, 0.0) + jnp.log1p(jnp.exp(-jnp.abs(pre)))) / GLA_GATE_NORM
    f_hi = log_f.astype(BF16)
    f_mid, f_lo = _hi_lo(log_f - f_hi.astype(F32))
    incl = incl_ref[...]
    b_all = (_dot(incl, f_hi) + _dot(incl, f_mid) + _dot(incl, f_lo)) * LOG2E

    st = st_ref[...]
    for c in range(nchunk):
        r0 = c * C
        q = q_ref[r0:r0 + C, :].astype(F32) * (GLA_DK ** -0.5)
        k = k_ref[r0:r0 + C, :].astype(F32)
        v = v_ref[r0:r0 + C, :]
        b = b_all[r0:r0 + C, :]
        b_last = b[C - 1:C, :]

        o_inter = _dot_nt((q * jnp.exp2(b)).astype(BF16), st.astype(BF16))

        att_rows = []
        for s in range(C // SUB):
            i0 = s * SUB
            if s > 0:
                b0 = b[i0:i0 + 1, :]
                q_dec = (q[i0:i0 + SUB, :] * jnp.exp2(b[i0:i0 + SUB, :] - b0)).astype(BF16)
                k_dec = (k * jnp.exp2(jnp.minimum(b0 - b, 0.0))).astype(BF16)
                att_far = _dot_nt(q_dec, k_dec)
            for g0 in range(i0, i0 + SUB, G):
                bg = b[g0:g0 + G, :]
                qg = q[g0:g0 + G, :]
                att = jnp.where(grp_ci < i0, att_far[g0 - i0:g0 - i0 + G, :], 0.0) if s > 0 else jnp.zeros((G, C), F32)
                for jj in range(i0, g0 + G):
                    a = jnp.sum(qg * k[jj:jj + 1, :] * jnp.exp2(bg - b[jj:jj + 1, :]), axis=-1, keepdims=True)
                    att = jnp.where(grp_ci == jj, a, att)
                att_rows.append(jnp.where(grp_ci <= grp_ri + g0, att, 0.0))
        att = jnp.concatenate(att_rows, axis=0)
        o = o_inter + _dot(att.astype(BF16), v)

        k_dec = (k * jnp.exp2(b_last - b)).astype(BF16)
        st = st * jnp.exp2(b_last) + _dot_tn(v, k_dec)

        y = _rms(o, ng_ref[...]) * _silu(gg_ref[r0:r0 + C, :].astype(F32))
        o_ref[r0:r0 + C, :] = y.astype(o_ref.dtype)
    st_ref[...] = st


def gla(z, ga, au, ab, ng, batch, seq, tb=512):
    T = z.shape[0]
    tb = min(tb, seq)
    nb = seq // tb
    q0 = 4 * SB_W // GLA_DK
    k0 = q0 + GLA_HEADS
    v0 = (4 * SB_W + 2 * GLA_KW) // GLA_DV
    g0 = v0 + GLA_HEADS
    return pl.pallas_call(
        functools.partial(_gla_kernel, nchunk=tb // GLA_CHUNK),
        grid=(batch, GLA_HEADS, nb),
        in_specs=[
            pl.BlockSpec((tb, GLA_DK), lambda b, h, i: (b * nb + i, q0 + h)),
            pl.BlockSpec((tb, GLA_DK), lambda b, h, i: (b * nb + i, k0 + h)),
            pl.BlockSpec((tb, GLA_DV), lambda b, h, i: (b * nb + i, v0 + h)),
            pl.BlockSpec((tb, GLA_DV), lambda b, h, i: (b * nb + i, g0 + h)),
            pl.BlockSpec((tb, LANE), lambda b, h, i: (b * nb + i, 0)),
            pl.BlockSpec((LANE, GLA_DK), lambda b, h, i: (0, h)),
            pl.BlockSpec((1, GLA_DK), lambda b, h, i: (0, h)),
            pl.BlockSpec((1, GLA_DV), lambda b, h, i: (0, 0)),
        ],
        out_specs=pl.BlockSpec((tb, GLA_DV), lambda b, h, i: (b * nb + i, h)),
        out_shape=jax.ShapeDtypeStruct((T, GLA_VW), BF16),
        scratch_shapes=[pltpu.VMEM((GLA_DV, GLA_DK), F32), pltpu.VMEM((tb, tb), BF16)],
        compiler_params=_cparams(("parallel", "parallel", "arbitrary")),
        name="gla",
    )(z, z, z, z, ga, au, ab, ng)


def _out_proj_kernel(y1_ref, y2_ref, w1_ref, w2_ref, x_ref, gf_ref, o_ref, *, final):
    acc = x_ref[...] + _dot(y1_ref[...], w1_ref[...]) + _dot(y2_ref[...], w2_ref[...])
    if final:
        acc = _rms(acc, gf_ref[...])
    o_ref[...] = acc


def out_proj(y1, c1, y2, c2, w, x, gf, final, tm=256):
    T, D = x.shape
    kh = w.shape[0] // 2
    tm = min(tm, T)
    return pl.pallas_call(
        functools.partial(_out_proj_kernel, final=final),
        grid=(T // tm,),
        in_specs=[
            pl.BlockSpec((tm, kh), lambda i: (i, c1)),
            pl.BlockSpec((tm, kh), lambda i: (i, c2)),
            pl.BlockSpec((kh, D), lambda i: (0, 0)),
            pl.BlockSpec((kh, D), lambda i: (1, 0)),
            pl.BlockSpec((tm, D), lambda i: (i, 0)),
            pl.BlockSpec((1, D), lambda i: (0, 0)),
        ],
        out_specs=pl.BlockSpec((tm, D), lambda i: (i, 0)),
        out_shape=jax.ShapeDtypeStruct((T, D), F32),
        compiler_params=_cparams(("parallel",)),
        name="out_proj",
    )(y1, y2, w, w, x, gf.reshape(1, D))


def _mla_up_kernel(lat_ref, pos_ref, inv_ref, qg_ref, kvg_ref, wq_ref, wkv_ref, q_ref, k_ref, v_ref, *, scale):
    hq = _rms(lat_ref[:, 0:MLA_QR], qg_ref[...]).astype(BF16)
    hkv = _rms(lat_ref[:, MLA_QR:MLA_QR + MLA_KVR], kvg_ref[...]).astype(BF16)
    ang = pos_ref[...].astype(F32) * inv_ref[...]
    lane = lax.broadcasted_iota(jnp.int32, ang.shape, 1)
    half = MLA_ROPE // 2
    cos2 = jnp.where(lane < MLA_ROPE, jnp.cos(ang), 0.0)
    sin2 = jnp.where(lane < half, -jnp.sin(ang), jnp.where(lane < MLA_ROPE, jnp.sin(ang), 0.0))

    def rope(r):
        return r * cos2 + pltpu.roll(r, MLA_ROPE, 1) * sin2

    kr = rope(lat_ref[:, MLA_QR + MLA_KVR:MLA_QR + MLA_KVR + LANE]).astype(BF16)
    kn = _dot(hkv, wkv_ref[:, 0:MLA_HEADS * MLA_NOPE]).astype(BF16)
    v_ref[...] = _dot(hkv, wkv_ref[:, MLA_HEADS * MLA_NOPE:]).astype(BF16)
    for h in range(MLA_HEADS):
        c0 = h * MLA_QK_PAD
        y = _dot(hq, wq_ref[:, c0:c0 + MLA_QK_PAD])
        q_ref[:, c0:c0 + MLA_NOPE] = (y[:, 0:MLA_NOPE] * scale).astype(BF16)
        q_ref[:, c0 + MLA_NOPE:c0 + MLA_QK_PAD] = (rope(y[:, MLA_NOPE:]) * scale).astype(BF16)
        k_ref[:, c0:c0 + MLA_NOPE] = kn[:, h * MLA_NOPE:(h + 1) * MLA_NOPE]
        k_ref[:, c0 + MLA_NOPE:c0 + MLA_QK_PAD] = kr


def mla_up(lat, pos, inv, qg, kvg, wq, wkv, tm=256):
    T = lat.shape[0]
    tm = min(tm, T)
    qk_w = MLA_HEADS * MLA_QK_PAD
    v_w = MLA_HEADS * MLA_V
    return pl.pallas_call(
        functools.partial(_mla_up_kernel, scale=(MLA_NOPE + MLA_ROPE) ** -0.5 * LOG2E),
        grid=(T // tm,),
        in_specs=[
            pl.BlockSpec((tm, lat.shape[1]), lambda i: (i, 0)),
            pl.BlockSpec((tm, 1), lambda i: (i, 0)),
            pl.BlockSpec((1, LANE), lambda i: (0, 0)),
            pl.BlockSpec((1, MLA_QR), lambda i: (0, 0)),
            pl.BlockSpec((1, MLA_KVR), lambda i: (0, 0)),
            pl.BlockSpec(wq.shape, lambda i: (0, 0)),
            pl.BlockSpec(wkv.shape, lambda i: (0, 0)),
        ],
        out_specs=[
            pl.BlockSpec((tm, qk_w), lambda i: (i, 0)),
            pl.BlockSpec((tm, qk_w), lambda i: (i, 0)),
            pl.BlockSpec((tm, v_w), lambda i: (i, 0)),
        ],
        out_shape=[
            jax.ShapeDtypeStruct((T, qk_w), BF16),
            jax.ShapeDtypeStruct((T, qk_w), BF16),
            jax.ShapeDtypeStruct((T, v_w), BF16),
        ],
        compiler_params=_cparams(("parallel",)),
        name="mla_up",
    )(lat, pos, inv, qg.reshape(1, -1), kvg.reshape(1, -1), wq, wkv)


def _mla_attn_kernel(q_ref, k_ref, v_ref, g_ref, o_ref, s0_ref, s1_ref, m_ref, l_ref, acc_ref, *, tq, tk, rt):
    qi = pl.program_id(2)
    nd = tq // tk
    nf = qi * nd
    base = qi * tq
    reps = tk // LANE
    s_refs = (s0_ref, s1_ref)
    m_ref[...] = jnp.full_like(m_ref, -jnp.inf)
    l_ref[...] = jnp.zeros_like(l_ref)
    acc_ref[...] = jnp.zeros_like(acc_ref)

    def scores(start, r0, slot, r1=tq):
        s_refs[slot][r0:r1, :] = _dot_nt(q_ref[r0:r1, :], k_ref[pl.ds(start, tk), :])

    def update(start, r0, slot, masked):
        vb = v_ref[pl.ds(start, tk), :]
        for t0 in range(r0, tq, rt):
            rows = slice(t0, t0 + rt)
            s = s_refs[slot][rows, :]
            if masked and t0 - r0 < tk:
                col = lax.broadcasted_iota(jnp.int32, s.shape, 1)
                row = lax.broadcasted_iota(jnp.int32, s.shape, 0)
                s = jnp.where(col <= row + (t0 - r0), s, -jnp.inf)
            m_old = m_ref[rows, :]
            m_new = jnp.maximum(m_old, jnp.max(s, axis=-1, keepdims=True))
            alpha = jnp.exp2(m_old - m_new)
            p = jnp.exp2(s - jnp.tile(m_new, (1, reps)))
            l_ref[rows, :] = alpha * l_ref[rows, :] + jnp.sum(p, axis=-1, keepdims=True)
            acc_ref[rows, :] = alpha * acc_ref[rows, :] + _dot(p.astype(BF16), vb)
            m_ref[rows, :] = m_new

    scores(0, 0, 0, rt)

    def body(i, carry):
        for u in range(nd):
            j = i * nd + u
            if u == 0:
                scores(pl.multiple_of(j * tk, tk), rt, 0)
            scores(pl.multiple_of((j + 1) * tk, tk), 0, (u + 1) % 2, tq if u + 1 < nd else rt)
            update(pl.multiple_of(j * tk, tk), 0, u % 2, False)
        return carry

    lax.fori_loop(0, qi, body, 0)
    scores(pl.multiple_of(base, tk), rt, 0)
    for m in range(nd):
        if m + 1 < nd:
            scores(pl.multiple_of(base + (m + 1) * tk, tk), (m + 1) * tk, (m + 1) % 2)
        update(pl.multiple_of(base + m * tk, tk), m * tk, m % 2, True)
    o_ref[...] = (acc_ref[...] / l_ref[...] * _silu(g_ref[...].astype(F32))).astype(o_ref.dtype)


def mla_attention(qf, kf, v, gate, batch, seq, tq=2048, tk=512, rt=256):
    T = qf.shape[0]
    tq = min(tq, seq)
    tk = min(tk, tq)
    rt = min(rt, tk)
    nq = seq // tq
    assert (tq // tk) % 2 == 0 or nq == 1, "score-buffer parity is static only for an even block count per tile"
    return pl.pallas_call(
        functools.partial(_mla_attn_kernel, tq=tq, tk=tk, rt=rt),
        grid=(batch, MLA_HEADS, nq),
        in_specs=[
            pl.BlockSpec((tq, MLA_QK_PAD), lambda b, h, i: (b * nq + i, h)),
            pl.BlockSpec((seq, MLA_QK_PAD), lambda b, h, i: (b, h)),
            pl.BlockSpec((seq, MLA_V), lambda b, h, i: (b, h)),
            pl.BlockSpec((tq, MLA_V), lambda b, h, i: (b * nq + i, h)),
        ],
        out_specs=pl.BlockSpec((tq, MLA_V), lambda b, h, i: (b * nq + i, h)),
        out_shape=jax.ShapeDtypeStruct((T, MLA_HEADS * MLA_V), BF16),
        scratch_shapes=[pltpu.VMEM((tq, tk), F32), pltpu.VMEM((tq, tk), F32), pltpu.VMEM((tq, LANE), F32),
                        pltpu.VMEM((tq, LANE), F32), pltpu.VMEM((tq, MLA_V), F32)],
        compiler_params=_cparams(("parallel", "parallel", "arbitrary")),
        name="mla_attention",
    )(qf, kf, v, gate)


def _swap_halves(w):
    half = w.shape[-1] // 2
    return jnp.concatenate([w[..., half:], w[..., :half]], axis=-1)


def _prep_even(w_in, alpha_up, alpha_bias):
    col_scale = jnp.concatenate([jnp.full((SB_W,), -(SB_DIM ** -0.5) * LOG2E, F32), jnp.ones((EVEN_MAIN - SB_W,), F32)])
    w_main = (w_in[:, :EVEN_MAIN] * col_scale).astype(BF16)
    w_ga = jnp.pad(w_in[:, EVEN_MAIN:], ((0, 0), (0, LANE - GLA_RANK))).astype(BF16)
    au = jnp.pad(alpha_up, ((0, LANE - GLA_RANK), (0, 0)))
    return w_main, w_ga, au, alpha_bias.reshape(1, -1)


def _prep_odd(w_in, w_q_up, w_kv_up):
    d = w_in.shape[0]
    o1 = MLA_QR + MLA_KVR
    w_kr = w_in[:, o1:o1 + MLA_ROPE]
    w_lat = jnp.concatenate([w_in[:, :o1], w_kr, _swap_halves(w_kr)], axis=1).astype(BF16)
    w_gate = w_in[:, o1 + MLA_ROPE:].astype(BF16)
    wq = w_q_up.reshape(MLA_QR, MLA_HEADS, MLA_NOPE + MLA_ROPE)
    wq_r = wq[..., MLA_NOPE:]
    wq = jnp.concatenate([wq[..., :MLA_NOPE], wq_r, _swap_halves(wq_r)], axis=-1)
    wq = wq.reshape(MLA_QR, MLA_HEADS * MLA_QK_PAD).astype(BF16)
    wkv = w_kv_up.reshape(MLA_KVR, MLA_HEADS, MLA_NOPE + MLA_V)
    wkv = jnp.concatenate([wkv[..., :MLA_NOPE].reshape(MLA_KVR, -1), wkv[..., MLA_NOPE:].reshape(MLA_KVR, -1)], axis=1)
    del d
    return w_lat, w_gate, wq, wkv.astype(BF16)


def _even_layer(x, batch, seq, norm_g, w_in, alpha_up, alpha_bias, gla_norm_g, w_out, final_g, final):
    w_main, w_ga, au, ab = _prep_even(w_in, alpha_up, alpha_bias)
    z, ga = norm_matmul(x, norm_g, w_main, BF16, w_ga, F32, tm=1024)
    y_a = sb_attention(z, batch, seq)
    y_b = gla(z, ga, au, ab, gla_norm_g.reshape(1, -1), batch, seq)
    return out_proj(y_a, 0, y_b, 0, w_out.astype(BF16), x, final_g, final)


def _odd_layer(x, pos, inv, batch, seq, norm_g, w_in, q_norm_g, w_q_up, kv_norm_g, w_kv_up, w_out, final_g, final):
    w_lat, w_gate, wq, wkv = _prep_odd(w_in, w_q_up, w_kv_up)
    gate, lat = norm_matmul(x, norm_g, w_gate, BF16, w_lat, F32, tm=512)
    qf, kf, v = mla_up(lat, pos, inv, q_norm_g, kv_norm_g, wq, wkv)
    y = mla_attention(qf, kf, v, gate, batch, seq)
    return out_proj(y, 0, y, 1, w_out.astype(BF16), x, final_g, final)


def kernel(x, positions, ln_even, w_in_even, gla_alpha_up, gla_alpha_bias, gla_norm, w_out_even,
           ln_odd, w_in_odd, q_norm, w_q_up, kv_norm, w_kv_up, w_out_odd, final_norm):
    batch, seq, d = x.shape
    depth = ln_even.shape[0] + ln_odd.shape[0]
    h = x.reshape(batch * seq, d)
    pos = positions.reshape(batch * seq, 1)
    half = MLA_ROPE // 2
    inv = ROPE_THETA ** (-jnp.arange(half, dtype=F32) / half)
    inv = jnp.concatenate([inv, inv, jnp.zeros((LANE - MLA_ROPE,), F32)]).reshape(1, LANE)
    for layer in range(depth):
        i = layer // 2
        final = layer == depth - 1
        if layer % 2 == 0:
            h = _even_layer(h, batch, seq, ln_even[i], w_in_even[i], gla_alpha_up[i], gla_alpha_bias[i],
                            gla_norm[i], w_out_even[i], final_norm, final)
        else:
            h = _odd_layer(h, pos, inv, batch, seq, ln_odd[i], w_in_odd[i], q_norm[i], w_q_up[i],
                           kv_norm[i], w_kv_up[i], w_out_odd[i], final_norm, final)
    return h.reshape(batch, seq, d)
```

```python
import functools

import jax
import jax.numpy as jnp
from jax import lax
from jax.experimental import pallas as pl
from jax.experimental.pallas import tpu as pltpu

F32 = jnp.float32
BF16 = jnp.bfloat16

SB_HEADS = 8
SB_DIM = 128
GLA_HEADS = 4
GLA_DK = 128
GLA_DV = 256
GLA_RANK = 16
GLA_GATE_NORM = 16.0
GLA_CHUNK = 64
GLA_SUB = 16
SB_SLOTS = 4
MLA_SOFTMAX_ROWS = 64
MLA_HEADS = 16
MLA_QR = 512
MLA_KVR = 512
MLA_NOPE = 128
MLA_ROPE = 64
MLA_V = 128
MLA_QK_PAD = 256
ROPE_THETA = 10000.0
EPS = 1e-6
LOG2E = 1.4426950408889634

LANE = 128
SUBLANE = 8
VMEM_LIMIT = 48 * 1024 * 1024

SB_W = SB_HEADS * SB_DIM
GLA_KW = GLA_HEADS * GLA_DK
GLA_VW = GLA_HEADS * GLA_DV
EVEN_MAIN = 4 * SB_W + 2 * GLA_KW + 2 * GLA_VW


def _cparams(sem):
    return pltpu.CompilerParams(dimension_semantics=sem, vmem_limit_bytes=VMEM_LIMIT)


def _rms(x, g):
    return x * lax.rsqrt(jnp.mean(x * x, axis=-1, keepdims=True) + EPS) * g


def _silu(g):
    return g * (1.0 / (1.0 + jnp.exp(-g)))


def _dot(a, b):
    return jnp.dot(a, b, preferred_element_type=F32)


def _hi_lo(x):
    hi = x.astype(BF16)
    return hi, (x - hi.astype(F32)).astype(BF16)


def _dot_nt(a, b):
    return lax.dot_general(a, b, (((1,), (1,)), ((), ())), preferred_element_type=F32)


def _dot_tn(a, b):
    return lax.dot_general(a, b, (((0,), (0,)), ((), ())), preferred_element_type=F32)


def _norm_matmul_kernel(x_ref, g_ref, w_ref, ws_ref, o_ref, os_ref, h_ref):
    @pl.when(pl.program_id(1) == 0)
    def _():
        h = _rms(x_ref[...], g_ref[...]).astype(BF16)
        h_ref[...] = h
        os_ref[...] = _dot(h, ws_ref[...]).astype(os_ref.dtype)

    o_ref[...] = _dot(h_ref[...], w_ref[...]).astype(o_ref.dtype)


def _pick_tile(n, cap):
    best = LANE
    for t in range(LANE, cap + 1, LANE):
        if n % t == 0:
            best = t
    return best


def norm_matmul(x, g, w, out_dtype, w_side, side_dtype, tm, tn_cap=1024):
    T, K = x.shape
    N = w.shape[1]
    ns = w_side.shape[1]
    tn = _pick_tile(N, tn_cap)
    tm = min(tm, T)
    return pl.pallas_call(
        _norm_matmul_kernel,
        grid=(T // tm, N // tn),
        in_specs=[
            pl.BlockSpec((tm, K), lambda i, j: (i, 0)),
            pl.BlockSpec((1, K), lambda i, j: (0, 0)),
            pl.BlockSpec((K, tn), lambda i, j: (0, j)),
            pl.BlockSpec((K, ns), lambda i, j: (0, 0)),
        ],
        out_specs=[
            pl.BlockSpec((tm, tn), lambda i, j: (i, j)),
            pl.BlockSpec((tm, ns), lambda i, j: (i, 0)),
        ],
        out_shape=[jax.ShapeDtypeStruct((T, N), out_dtype), jax.ShapeDtypeStruct((T, ns), side_dtype)],
        scratch_shapes=[pltpu.VMEM((tm, K), BF16)],
        compiler_params=_cparams(("parallel", "arbitrary")),
        name="norm_matmul",
    )(x, g.reshape(1, K), w, w_side)


def _sb_kernel(q_ref, k_ref, v_ref, g_ref, o_ref, *scratch, tq, tk, rt):
    qi = pl.program_id(2)
    nd = tq // tk
    ns = SB_SLOTS
    nf = qi * nd
    nzw_b = scratch[0:ns // 2]
    ls_b, tot_b = (scratch[ns // 2 + i * ns:ns // 2 + (i + 1) * ns] for i in range(2))
    acc_ref = scratch[ns // 2 + 2 * ns]
    krow = lax.broadcasted_iota(jnp.int32, (tk, tk), 0)
    kcol = lax.broadcasted_iota(jnp.int32, (tk, tk), 1)
    from_s = (krow >= kcol).astype(BF16)
    acc_ref[...] = jnp.zeros_like(acc_ref)
    @pl.when(qi == 0)
    def _():
        nzw_b[ns // 2 - 1][...] = jnp.full((tq, 2 * tk), jnp.inf, F32)
        ls_b[ns - 2][...] = jnp.zeros((tq, tk), BF16)
        tot_b[ns - 2][...] = jnp.zeros((tq, LANE), F32)

    def nz_view(slot, rows):
        return nzw_b[slot // 2].at[rows, (slot % 2) * tk:(slot % 2 + 1) * tk]

    def stage_a(start, r0, slot):
        nzw_b[slot // 2][r0:, :] = _dot_nt(q_ref[r0:, :], k_ref[pl.ds(start, 2 * tk), :])

    def stage_b(r0, slot, masked):
        for t0 in range(r0, tq, rt):
            rows = slice(t0, t0 + rt)
            nz = nz_view(slot, rows)[...]
            neg_abs = lax.bitcast_convert_type(lax.bitcast_convert_type(nz, jnp.uint32) | jnp.uint32(0x80000000), F32)
            log_stay = jnp.minimum(nz, 0.0) - jnp.log(1.0 + jnp.exp2(neg_abs)) * LOG2E
            if masked and t0 - r0 < tk:
                col = lax.broadcasted_iota(jnp.int32, nz.shape, 1)
                row = lax.broadcasted_iota(jnp.int32, nz.shape, 0)
                keep = col < row + (t0 - r0)
                log_stay = jnp.where(keep, log_stay, 0.0)
                nz_view(slot, rows)[...] = jnp.where(keep, nz, jnp.inf)
            ls_b[slot][rows, :] = log_stay.astype(BF16)
            tot_b[slot][rows, :] = jnp.broadcast_to(jnp.sum(log_stay, axis=-1, keepdims=True), (rt, LANE))

    def stage_c(start, r0, slot):
        vb = v_ref[pl.ds(start, tk), :]
        for t0 in range(r0, tq, rt):
            rows = slice(t0, t0 + rt)
            upto = _dot(ls_b[slot][rows, :], from_s)
            w = jnp.exp2(upto - nz_view(slot, rows)[...])
            acc_ref[rows, :] = acc_ref[rows, :] * jnp.exp2(tot_b[slot][rows, :]) + _dot(w.astype(BF16), vb)

    def trip(t, fill):
        for u in range(nd):
            g = t * nd + u
            if u % 2 == 0:
                stage_a(pl.multiple_of(g * tk, tk), 0, u % ns)
            if not (fill and u < 1):
                stage_b(0, (u - 1) % ns, False)
            if not (fill and u < 2):
                stage_c(pl.multiple_of((g - 2) * tk, tk), 0, (u - 2) % ns)

    @pl.when(qi > 0)
    def _():
        trip(0, True)

    def body(t, carry):
        trip(t, False)
        return carry

    lax.fori_loop(1, qi, body, 0)
    for e in range(nd + 2):
        if e < nd and e % 2 == 0:
            stage_a(pl.multiple_of((nf + e) * tk, tk), e * tk, e % ns)
        if 1 <= e <= nd:
            stage_b((e - 1) * tk, (e - 1) % ns, True)
        elif e == 0:
            stage_b(0, ns - 1, False)
        if e >= 2:
            stage_c(pl.multiple_of((nf + e - 2) * tk, tk), (e - 2) * tk, (e - 2) % ns)
        else:
            stage_c(pl.multiple_of(jnp.maximum(nf + e - 2, 0) * tk, tk), 0, (e - 2) % ns)
    o_ref[...] = (acc_ref[...] * _silu(g_ref[...].astype(F32))).astype(o_ref.dtype)


def sb_attention(z, batch, seq, tq=1024, tk=256, rt=256):
    T = z.shape[0]
    tq = min(tq, seq)
    tk = min(tk, tq)
    rt = min(rt, tk)
    nq = seq // tq
    ns = SB_SLOTS
    assert (tq // tk) % ns == 0, "slot of a block must not depend on the loop trip"
    hw = SB_W // SB_DIM
    scratch = ([pltpu.VMEM((tq, 2 * tk), F32)] * (ns // 2) + [pltpu.VMEM((tq, tk), BF16)] * ns
               + [pltpu.VMEM((tq, LANE), F32)] * ns + [pltpu.VMEM((tq, SB_DIM), F32)])
    return pl.pallas_call(
        functools.partial(_sb_kernel, tq=tq, tk=tk, rt=rt),
        grid=(batch, SB_HEADS, nq),
        in_specs=[
            pl.BlockSpec((tq, SB_DIM), lambda b, h, i: (b * nq + i, h)),
            pl.BlockSpec((seq, SB_DIM), lambda b, h, i: (b, hw + h)),
            pl.BlockSpec((seq, SB_DIM), lambda b, h, i: (b, 2 * hw + h)),
            pl.BlockSpec((tq, SB_DIM), lambda b, h, i: (b * nq + i, 3 * hw + h)),
        ],
        out_specs=pl.BlockSpec((tq, SB_DIM), lambda b, h, i: (b * nq + i, h)),
        out_shape=jax.ShapeDtypeStruct((T, SB_W), BF16),
        scratch_shapes=scratch,
        compiler_params=_cparams(("parallel", "parallel", "arbitrary")),
        name="sb_attention",
    )(z, z, z, z)


def _gla_kernel(q_ref, k_ref, v_ref, gg_ref, ga_ref, au_ref, ab_ref, ng_ref, o_ref, st_ref, incl_ref, *, nchunk):
    C, SUB, G = GLA_CHUNK, GLA_SUB, SUBLANE
    tb = nchunk * C

    @pl.when(pl.program_id(2) == 0)
    def _():
        st_ref[...] = jnp.zeros_like(st_ref)
        ri = lax.broadcasted_iota(jnp.int32, (tb, tb), 0)
        ci = lax.broadcasted_iota(jnp.int32, (tb, tb), 1)
        incl_ref[...] = jnp.where(ci <= ri, jnp.where(ci >= (ri & ~(C - 1)), 1.0, 0.0), 0.0).astype(BF16)

    grp_ri = lax.broadcasted_iota(jnp.int32, (G, C), 0)
    grp_ci = lax.broadcasted_iota(jnp.int32, (G, C), 1)

    g_hi, g_lo = _hi_lo(ga_ref[...])
    a_hi, a_lo = _hi_lo(au_ref[...])
    pre = _dot(g_hi, a_hi) + _dot(g_hi, a_lo) + _dot(g_lo, a_hi) + ab_ref[...]
    log_f = -(jnp.maximum(-pre, 0.0) + jnp.log1p(jnp.exp(-jnp.abs(pre)))) / GLA_GATE_NORM
    f_hi = log_f.astype(BF16)
    f_mid, f_lo = _hi_lo(log_f - f_hi.astype(F32))
    incl = incl_ref[...]
    b_all = (_dot(incl, f_hi) + _dot(incl, f_mid) + _dot(incl, f_lo)) * LOG2E

    st = st_ref[...]
    for c in range(nchunk):
        r0 = c * C
        q = q_ref[r0:r0 + C, :].astype(F32) * (GLA_DK ** -0.5)
        k = k_ref[r0:r0 + C, :].astype(F32)
        v = v_ref[r0:r0 + C, :]
        b = b_all[r0:r0 + C, :]
        b_last = b[C - 1:C, :]

        o_inter = _dot_nt((q * jnp.exp2(b)).astype(BF16), st.astype(BF16))

        att_rows = []
        for s in range(C // SUB):
            i0 = s * SUB
            if s > 0:
                b0 = b[i0:i0 + 1, :]
                q_dec = (q[i0:i0 + SUB, :] * jnp.exp2(b[i0:i0 + SUB, :] - b0)).astype(BF16)
                k_dec = (k * jnp.exp2(jnp.minimum(b0 - b, 0.0))).astype(BF16)
                att_far = _dot_nt(q_dec, k_dec)
            for g0 in range(i0, i0 + SUB, G):
                bg = b[g0:g0 + G, :]
                qg = q[g0:g0 + G, :]
                att = jnp.where(grp_ci < i0, att_far[g0 - i0:g0 - i0 + G, :], 0.0) if s > 0 else jnp.zeros((G, C), F32)
                for jj in range(i0, g0 + G):
                    a = jnp.sum(qg * k[jj:jj + 1, :] * jnp.exp2(bg - b[jj:jj + 1, :]), axis=-1, keepdims=True)
                    att = jnp.where(grp_ci == jj, a, att)
                att_rows.append(jnp.where(grp_ci <= grp_ri + g0, att, 0.0))
        att = jnp.concatenate(att_rows, axis=0)
        o = o_inter + _dot(att.astype(BF16), v)

        k_dec = (k * jnp.exp2(b_last - b)).astype(BF16)
        st = st * jnp.exp2(b_last) + _dot_tn(v, k_dec)

        y = _rms(o, ng_ref[...]) * _silu(gg_ref[r0:r0 + C, :].astype(F32))
        o_ref[r0:r0 + C, :] = y.astype(o_ref.dtype)
    st_ref[...] = st


def gla(z, ga, au, ab, ng, batch, seq, tb=512):
    T = z.shape[0]
    tb = min(tb, seq)
    nb = seq // tb
    q0 = 4 * SB_W // GLA_DK
    k0 = q0 + GLA_HEADS
    v0 = (4 * SB_W + 2 * GLA_KW) // GLA_DV
    g0 = v0 + GLA_HEADS
    return pl.pallas_call(
        functools.partial(_gla_kernel, nchunk=tb // GLA_CHUNK),
        grid=(batch, GLA_HEADS, nb),
        in_specs=[
            pl.BlockSpec((tb, GLA_DK), lambda b, h, i: (b * nb + i, q0 + h)),
            pl.BlockSpec((tb, GLA_DK), lambda b, h, i: (b * nb + i, k0 + h)),
            pl.BlockSpec((tb, GLA_DV), lambda b, h, i: (b * nb + i, v0 + h)),
            pl.BlockSpec((tb, GLA_DV), lambda b, h, i: (b * nb + i, g0 + h)),
            pl.BlockSpec((tb, LANE), lambda b, h, i: (b * nb + i, 0)),
            pl.BlockSpec((LANE, GLA_DK), lambda b, h, i: (0, h)),
            pl.BlockSpec((1, GLA_DK), lambda b, h, i: (0, h)),
            pl.BlockSpec((1, GLA_DV), lambda b, h, i: (0, 0)),
        ],
        out_specs=pl.BlockSpec((tb, GLA_DV), lambda b, h, i: (b * nb + i, h)),
        out_shape=jax.ShapeDtypeStruct((T, GLA_VW), BF16),
        scratch_shapes=[pltpu.VMEM((GLA_DV, GLA_DK), F32), pltpu.VMEM((tb, tb), BF16)],
        compiler_params=_cparams(("parallel", "parallel", "arbitrary")),
        name="gla",
    )(z, z, z, z, ga, au, ab, ng)


def _out_proj_kernel(y1_ref, y2_ref, w1_ref, w2_ref, x_ref, gf_ref, o_ref, *, final):
    acc = x_ref[...] + _dot(y1_ref[...], w1_ref[...]) + _dot(y2_ref[...], w2_ref[...])
    if final:
        acc = _rms(acc, gf_ref[...])
    o_ref[...] = acc


def out_proj(y1, c1, y2, c2, w, x, gf, final, tm=256):
    T, D = x.shape
    kh = w.shape[0] // 2
    tm = min(tm, T)
    return pl.pallas_call(
        functools.partial(_out_proj_kernel, final=final),
        grid=(T // tm,),
        in_specs=[
            pl.BlockSpec((tm, kh), lambda i: (i, c1)),
            pl.BlockSpec((tm, kh), lambda i: (i, c2)),
            pl.BlockSpec((kh, D), lambda i: (0, 0)),
            pl.BlockSpec((kh, D), lambda i: (1, 0)),
            pl.BlockSpec((tm, D), lambda i: (i, 0)),
            pl.BlockSpec((1, D), lambda i: (0, 0)),
        ],
        out_specs=pl.BlockSpec((tm, D), lambda i: (i, 0)),
        out_shape=jax.ShapeDtypeStruct((T, D), F32),
        compiler_params=_cparams(("parallel",)),
        name="out_proj",
    )(y1, y2, w, w, x, gf.reshape(1, D))


def _mla_up_kernel(lat_ref, pos_ref, inv_ref, qg_ref, kvg_ref, wq_ref, wkv_ref, q_ref, k_ref, v_ref, *, scale):
    hq = _rms(lat_ref[:, 0:MLA_QR], qg_ref[...]).astype(BF16)
    hkv = _rms(lat_ref[:, MLA_QR:MLA_QR + MLA_KVR], kvg_ref[...]).astype(BF16)
    ang = pos_ref[...].astype(F32) * inv_ref[...]
    lane = lax.broadcasted_iota(jnp.int32, ang.shape, 1)
    half = MLA_ROPE // 2
    cos2 = jnp.where(lane < MLA_ROPE, jnp.cos(ang), 0.0)
    sin2 = jnp.where(lane < half, -jnp.sin(ang), jnp.where(lane < MLA_ROPE, jnp.sin(ang), 0.0))

    def rope(r):
        return r * cos2 + pltpu.roll(r, MLA_ROPE, 1) * sin2

    kr = rope(lat_ref[:, MLA_QR + MLA_KVR:MLA_QR + MLA_KVR + LANE]).astype(BF16)
    kn = _dot(hkv, wkv_ref[:, 0:MLA_HEADS * MLA_NOPE]).astype(BF16)
    v_ref[...] = _dot(hkv, wkv_ref[:, MLA_HEADS * MLA_NOPE:]).astype(BF16)
    for h in range(MLA_HEADS):
        c0 = h * MLA_QK_PAD
        y = _dot(hq, wq_ref[:, c0:c0 + MLA_QK_PAD])
        q_ref[:, c0:c0 + MLA_NOPE] = (y[:, 0:MLA_NOPE] * scale).astype(BF16)
        q_ref[:, c0 + MLA_NOPE:c0 + MLA_QK_PAD] = (rope(y[:, MLA_NOPE:]) * scale).astype(BF16)
        k_ref[:, c0:c0 + MLA_NOPE] = kn[:, h * MLA_NOPE:(h + 1) * MLA_NOPE]
        k_ref[:, c0 + MLA_NOPE:c0 + MLA_QK_PAD] = kr


def mla_up(lat, pos, inv, qg, kvg, wq, wkv, tm=256):
    T = lat.shape[0]
    tm = min(tm, T)
    qk_w = MLA_HEADS * MLA_QK_PAD
    v_w = MLA_HEADS * MLA_V
    return pl.pallas_call(
        functools.partial(_mla_up_kernel, scale=(MLA_NOPE + MLA_ROPE) ** -0.5 * LOG2E),
        grid=(T // tm,),
        in_specs=[
            pl.BlockSpec((tm, lat.shape[1]), lambda i: (i, 0)),
            pl.BlockSpec((tm, 1), lambda i: (i, 0)),
            pl.BlockSpec((1, LANE), lambda i: (0, 0)),
            pl.BlockSpec((1, MLA_QR), lambda i: (0, 0)),
            pl.BlockSpec((1, MLA_KVR), lambda i: (0, 0)),
            pl.BlockSpec(wq.shape, lambda i: (0, 0)),
            pl.BlockSpec(wkv.shape, lambda i: (0, 0)),
        ],
        out_specs=[
            pl.BlockSpec((tm, qk_w), lambda i: (i, 0)),
            pl.BlockSpec((tm, qk_w), lambda i: (i, 0)),
            pl.BlockSpec((tm, v_w), lambda i: (i, 0)),
        ],
        out_shape=[
            jax.ShapeDtypeStruct((T, qk_w), BF16),
            jax.ShapeDtypeStruct((T, qk_w), BF16),
            jax.ShapeDtypeStruct((T, v_w), BF16),
        ],
        compiler_params=_cparams(("parallel",)),
        name="mla_up",
    )(lat, pos, inv, qg.reshape(1, -1), kvg.reshape(1, -1), wq, wkv)


def _mla_attn_kernel(q_ref, k_ref, v_ref, g_ref, o_ref, s0_ref, s1_ref, p0_ref, p1_ref, m_ref, l_ref, acc_ref,
                     *, tq, tk, rt, st):
    qi = pl.program_id(2)
    nd = tq // tk
    nf = qi * nd
    base = qi * tq
    reps = tk // LANE
    s_refs = (s0_ref, s1_ref)
    p_refs = (p0_ref, p1_ref)
    m_ref[...] = jnp.full_like(m_ref, -jnp.inf)
    l_ref[...] = jnp.zeros_like(l_ref)
    acc_ref[...] = jnp.zeros_like(acc_ref)

    def scores(start, r0, slot, r1=tq):
        s_refs[slot][r0:r1, :] = _dot_nt(q_ref[r0:r1, :], k_ref[pl.ds(start, tk), :])

    def update(start, r0, slot, masked):
        vb = v_ref[pl.ds(start, tk), :]
        for g0 in range(r0, tq, rt):
            alphas = []
            for t0 in range(g0, g0 + rt, st):
                rows = slice(t0, t0 + st)
                s = s_refs[slot][rows, :]
                if masked and t0 - r0 < tk:
                    col = lax.broadcasted_iota(jnp.int32, s.shape, 1)
                    row = lax.broadcasted_iota(jnp.int32, s.shape, 0)
                    s = jnp.where(col <= row + (t0 - r0), s, -jnp.inf)
                m_old = m_ref[rows, :]
                m_new = jnp.maximum(m_old, jnp.max(s, axis=-1, keepdims=True))
                alpha = jnp.exp2(m_old - m_new)
                p = jnp.exp2(s - jnp.tile(m_new, (1, reps)))
                l_ref[rows, :] = alpha * l_ref[rows, :] + jnp.sum(p, axis=-1, keepdims=True)
                p_refs[slot][rows, :] = p.astype(BF16)
                m_ref[rows, :] = m_new
                alphas.append(alpha)
            grp = slice(g0, g0 + rt)
            acc_ref[grp, :] = jnp.concatenate(alphas, axis=0) * acc_ref[grp, :] + _dot(p_refs[slot][grp, :], vb)

    scores(0, 0, 0, rt)

    def body(i, carry):
        for u in range(nd):
            j = i * nd + u
            if u == 0:
                scores(pl.multiple_of(j * tk, tk), rt, 0)
            scores(pl.multiple_of((j + 1) * tk, tk), 0, (u + 1) % 2, tq if u + 1 < nd else rt)
            update(pl.multiple_of(j * tk, tk), 0, u % 2, False)
        return carry

    lax.fori_loop(0, qi, body, 0)
    scores(pl.multiple_of(base, tk), rt, 0)
    for m in range(nd):
        if m + 1 < nd:
            scores(pl.multiple_of(base + (m + 1) * tk, tk), (m + 1) * tk, (m + 1) % 2)
        update(pl.multiple_of(base + m * tk, tk), m * tk, m % 2, True)
    o_ref[...] = (acc_ref[...] / l_ref[...] * _silu(g_ref[...].astype(F32))).astype(o_ref.dtype)


def mla_attention(qf, kf, v, gate, batch, seq, tq=2048, tk=512, rt=256):
    T = qf.shape[0]
    tq = min(tq, seq)
    tk = min(tk, tq)
    rt = min(rt, tk)
    nq = seq // tq
    assert (tq // tk) % 2 == 0 or nq == 1, "score-buffer parity is static only for an even block count per tile"
    return pl.pallas_call(
        functools.partial(_mla_attn_kernel, tq=tq, tk=tk, rt=rt, st=min(MLA_SOFTMAX_ROWS, rt)),
        grid=(batch, MLA_HEADS, nq),
        in_specs=[
            pl.BlockSpec((tq, MLA_QK_PAD), lambda b, h, i: (b * nq + i, h)),
            pl.BlockSpec((seq, MLA_QK_PAD), lambda b, h, i: (b, h)),
            pl.BlockSpec((seq, MLA_V), lambda b, h, i: (b, h)),
            pl.BlockSpec((tq, MLA_V), lambda b, h, i: (b * nq + i, h)),
        ],
        out_specs=pl.BlockSpec((tq, MLA_V), lambda b, h, i: (b * nq + i, h)),
        out_shape=jax.ShapeDtypeStruct((T, MLA_HEADS * MLA_V), BF16),
        scratch_shapes=[pltpu.VMEM((tq, tk), F32), pltpu.VMEM((tq, tk), F32),
                        pltpu.VMEM((tq, tk), BF16), pltpu.VMEM((tq, tk), BF16), pltpu.VMEM((tq, LANE), F32),
                        pltpu.VMEM((tq, LANE), F32), pltpu.VMEM((tq, MLA_V), F32)],
        compiler_params=_cparams(("parallel", "parallel", "arbitrary")),
        name="mla_attention",
    )(qf, kf, v, gate)


def _swap_halves(w):
    half = w.shape[-1] // 2
    return jnp.concatenate([w[..., half:], w[..., :half]], axis=-1)


def _prep_even(w_in, alpha_up, alpha_bias):
    col_scale = jnp.concatenate([jnp.full((SB_W,), -(SB_DIM ** -0.5) * LOG2E, F32), jnp.ones((EVEN_MAIN - SB_W,), F32)])
    w_main = (w_in[:, :EVEN_MAIN] * col_scale).astype(BF16)
    w_ga = jnp.pad(w_in[:, EVEN_MAIN:], ((0, 0), (0, LANE - GLA_RANK))).astype(BF16)
    au = jnp.pad(alpha_up, ((0, LANE - GLA_RANK), (0, 0)))
    return w_main, w_ga, au, alpha_bias.reshape(1, -1)


def _prep_odd(w_in, w_q_up, w_kv_up):
    d = w_in.shape[0]
    o1 = MLA_QR + MLA_KVR
    w_kr = w_in[:, o1:o1 + MLA_ROPE]
    w_lat = jnp.concatenate([w_in[:, :o1], w_kr, _swap_halves(w_kr)], axis=1).astype(BF16)
    w_gate = w_in[:, o1 + MLA_ROPE:].astype(BF16)
    wq = w_q_up.reshape(MLA_QR, MLA_HEADS, MLA_NOPE + MLA_ROPE)
    wq_r = wq[..., MLA_NOPE:]
    wq = jnp.concatenate([wq[..., :MLA_NOPE], wq_r, _swap_halves(wq_r)], axis=-1)
    wq = wq.reshape(MLA_QR, MLA_HEADS * MLA_QK_PAD).astype(BF16)
    wkv = w_kv_up.reshape(MLA_KVR, MLA_HEADS, MLA_NOPE + MLA_V)
    wkv = jnp.concatenate([wkv[..., :MLA_NOPE].reshape(MLA_KVR, -1), wkv[..., MLA_NOPE:].reshape(MLA_KVR, -1)], axis=1)
    del d
    return w_lat, w_gate, wq, wkv.astype(BF16)


def _even_layer(x, batch, seq, norm_g, w_in, alpha_up, alpha_bias, gla_norm_g, w_out, final_g, final):
    w_main, w_ga, au, ab = _prep_even(w_in, alpha_up, alpha_bias)
    z, ga = norm_matmul(x, norm_g, w_main, BF16, w_ga, F32, tm=1024)
    y_a = sb_attention(z, batch, seq)
    y_b = gla(z, ga, au, ab, gla_norm_g.reshape(1, -1), batch, seq)
    return out_proj(y_a, 0, y_b, 0, w_out.astype(BF16), x, final_g, final)


def _odd_layer(x, pos, inv, batch, seq, norm_g, w_in, q_norm_g, w_q_up, kv_norm_g, w_kv_up, w_out, final_g, final):
    w_lat, w_gate, wq, wkv = _prep_odd(w_in, w_q_up, w_kv_up)
    gate, lat = norm_matmul(x, norm_g, w_gate, BF16, w_lat, F32, tm=512)
    qf, kf, v = mla_up(lat, pos, inv, q_norm_g, kv_norm_g, wq, wkv)
    y = mla_attention(qf, kf, v, gate, batch, seq)
    return out_proj(y, 0, y, 1, w_out.astype(BF16), x, final_g, final)


def kernel(x, positions, ln_even, w_in_even, gla_alpha_up, gla_alpha_bias, gla_norm, w_out_even,
           ln_odd, w_in_odd, q_norm, w_q_up, kv_norm, w_kv_up, w_out_odd, final_norm):
    batch, seq, d = x.shape
    depth = ln_even.shape[0] + ln_odd.shape[0]
    h = x.reshape(batch * seq, d)
    pos = positions.reshape(batch * seq, 1)
    half = MLA_ROPE // 2
    inv = ROPE_THETA ** (-jnp.arange(half, dtype=F32) / half)
    inv = jnp.concatenate([inv, inv, jnp.zeros((LANE - MLA_ROPE,), F32)]).reshape(1, LANE)
    for layer in range(depth):
        i = layer // 2
        final = layer == depth - 1
        if layer % 2 == 0:
            h = _even_layer(h, batch, seq, ln_even[i], w_in_even[i], gla_alpha_up[i], gla_alpha_bias[i],
                            gla_norm[i], w_out_even[i], final_norm, final)
        else:
            h = _odd_layer(h, pos, inv, batch, seq, ln_odd[i], w_in_odd[i], q_norm[i], w_q_up[i],
                           kv_norm[i], w_kv_up[i], w_out_odd[i], final_norm, final)
    return h.reshape(batch, seq, d)
```

```python
import functools

import jax
import jax.numpy as jnp
from jax import lax
from jax.experimental import pallas as pl
from jax.experimental.pallas import tpu as pltpu

F32 = jnp.float32
BF16 = jnp.bfloat16

SB_HEADS = 8
SB_DIM = 128
GLA_HEADS = 4
GLA_DK = 128
GLA_DV = 256
GLA_RANK = 16
GLA_GATE_NORM = 16.0
GLA_CHUNK = 64
GLA_SUB = 16
SB_SLOTS = 4
MLA_SOFTMAX_ROWS = 64
MLA_HEADS = 16
MLA_QR = 512
MLA_KVR = 512
MLA_NOPE = 128
MLA_ROPE = 64
MLA_V = 128
MLA_QK_PAD = 256
ROPE_THETA = 10000.0
EPS = 1e-6
LOG2E = 1.4426950408889634

LANE = 128
SUBLANE = 8
VMEM_LIMIT = 48 * 1024 * 1024

SB_W = SB_HEADS * SB_DIM
GLA_KW = GLA_HEADS * GLA_DK
GLA_VW = GLA_HEADS * GLA_DV
EVEN_MAIN = 4 * SB_W + 2 * GLA_KW + 2 * GLA_VW


def _cparams(sem):
    return pltpu.CompilerParams(dimension_semantics=sem, vmem_limit_bytes=VMEM_LIMIT)


def _rms(x, g):
    return x * lax.rsqrt(jnp.mean(x * x, axis=-1, keepdims=True) + EPS) * g


def _silu(g):
    return g * (1.0 / (1.0 + jnp.exp(-g)))


def _dot(a, b):
    return jnp.dot(a, b, preferred_element_type=F32)


def _hi_lo(x):
    hi = x.astype(BF16)
    return hi, (x - hi.astype(F32)).astype(BF16)


def _dot_nt(a, b):
    return lax.dot_general(a, b, (((1,), (1,)), ((), ())), preferred_element_type=F32)


def _dot_tn(a, b):
    return lax.dot_general(a, b, (((0,), (0,)), ((), ())), preferred_element_type=F32)


def _norm_matmul_kernel(x_ref, g_ref, w_ref, ws_ref, o_ref, os_ref, h_ref):
    @pl.when(pl.program_id(1) == 0)
    def _():
        h = _rms(x_ref[...], g_ref[...]).astype(BF16)
        h_ref[...] = h
        os_ref[...] = _dot(h, ws_ref[...]).astype(os_ref.dtype)

    o_ref[...] = _dot(h_ref[...], w_ref[...]).astype(o_ref.dtype)


def _pick_tile(n, cap):
    best = LANE
    for t in range(LANE, cap + 1, LANE):
        if n % t == 0:
            best = t
    return best


def norm_matmul(x, g, w, out_dtype, w_side, side_dtype, tm, tn_cap=1024):
    T, K = x.shape
    N = w.shape[1]
    ns = w_side.shape[1]
    tn = _pick_tile(N, tn_cap)
    tm = min(tm, T)
    return pl.pallas_call(
        _norm_matmul_kernel,
        grid=(T // tm, N // tn),
        in_specs=[
            pl.BlockSpec((tm, K), lambda i, j: (i, 0)),
            pl.BlockSpec((1, K), lambda i, j: (0, 0)),
            pl.BlockSpec((K, tn), lambda i, j: (0, j)),
            pl.BlockSpec((K, ns), lambda i, j: (0, 0)),
        ],
        out_specs=[
            pl.BlockSpec((tm, tn), lambda i, j: (i, j)),
            pl.BlockSpec((tm, ns), lambda i, j: (i, 0)),
        ],
        out_shape=[jax.ShapeDtypeStruct((T, N), out_dtype), jax.ShapeDtypeStruct((T, ns), side_dtype)],
        scratch_shapes=[pltpu.VMEM((tm, K), BF16)],
        compiler_params=_cparams(("parallel", "arbitrary")),
        name="norm_matmul",
    )(x, g.reshape(1, K), w, w_side)


def _sb_kernel(q_ref, k_ref, v_ref, g_ref, o_ref, *scratch, tq, tk, rt):
    qi = pl.program_id(2)
    nd = tq // tk
    ns = SB_SLOTS
    nf = qi * nd
    nzw_b = scratch[0:ns // 2]
    ls_b, tot_b = (scratch[ns // 2 + i * ns:ns // 2 + (i + 1) * ns] for i in range(2))
    acc_ref = scratch[ns // 2 + 2 * ns]
    krow = lax.broadcasted_iota(jnp.int32, (tk, tk), 0)
    kcol = lax.broadcasted_iota(jnp.int32, (tk, tk), 1)
    from_s = (krow >= kcol).astype(BF16)
    acc_ref[...] = jnp.zeros_like(acc_ref)
    @pl.when(qi == 0)
    def _():
        nzw_b[ns // 2 - 1][...] = jnp.full((tq, 2 * tk), jnp.inf, F32)
        ls_b[ns - 2][...] = jnp.zeros((tq, tk), BF16)
        tot_b[ns - 2][...] = jnp.zeros((tq, LANE), F32)

    def nz_view(slot, rows):
        return nzw_b[slot // 2].at[rows, (slot % 2) * tk:(slot % 2 + 1) * tk]

    def stage_a(start, r0, slot):
        nzw_b[slot // 2][r0:, :] = _dot_nt(q_ref[r0:, :], k_ref[pl.ds(start, 2 * tk), :])

    def stage_b(r0, slot, masked):
        for t0 in range(r0, tq, rt):
            rows = slice(t0, t0 + rt)
            nz = nz_view(slot, rows)[...]
            neg_abs = lax.bitcast_convert_type(lax.bitcast_convert_type(nz, jnp.uint32) | jnp.uint32(0x80000000), F32)
            log_stay = jnp.minimum(nz, 0.0) - jnp.log(1.0 + jnp.exp2(neg_abs)) * LOG2E
            if masked and t0 - r0 < tk:
                col = lax.broadcasted_iota(jnp.int32, nz.shape, 1)
                row = lax.broadcasted_iota(jnp.int32, nz.shape, 0)
                keep = col < row + (t0 - r0)
                log_stay = jnp.where(keep, log_stay, 0.0)
                nz_view(slot, rows)[...] = jnp.where(keep, nz, jnp.inf)
            ls_b[slot][rows, :] = log_stay.astype(BF16)
            tot_b[slot][rows, :] = jnp.broadcast_to(jnp.sum(log_stay, axis=-1, keepdims=True), (rt, LANE))

    def stage_c(start, r0, slot):
        vb = v_ref[pl.ds(start, tk), :]
        for t0 in range(r0, tq, rt):
            rows = slice(t0, t0 + rt)
            upto = _dot(ls_b[slot][rows, :], from_s)
            w = jnp.exp2(upto - nz_view(slot, rows)[...])
            acc_ref[rows, :] = acc_ref[rows, :] * jnp.exp2(tot_b[slot][rows, :]) + _dot(w.astype(BF16), vb)

    def trip(t, fill):
        for u in range(nd):
            g = t * nd + u
            if u % 2 == 0:
                stage_a(pl.multiple_of(g * tk, tk), 0, u % ns)
            if not (fill and u < 1):
                stage_b(0, (u - 1) % ns, False)
            if not (fill and u < 2):
                stage_c(pl.multiple_of((g - 2) * tk, tk), 0, (u - 2) % ns)

    @pl.when(qi > 0)
    def _():
        trip(0, True)

    def body(t, carry):
        trip(t, False)
        return carry

    lax.fori_loop(1, qi, body, 0)
    for e in range(nd + 2):
        if e < nd and e % 2 == 0:
            stage_a(pl.multiple_of((nf + e) * tk, tk), e * tk, e % ns)
        if 1 <= e <= nd:
            stage_b((e - 1) * tk, (e - 1) % ns, True)
        elif e == 0:
            stage_b(0, ns - 1, False)
        if e >= 2:
            stage_c(pl.multiple_of((nf + e - 2) * tk, tk), (e - 2) * tk, (e - 2) % ns)
        else:
            stage_c(pl.multiple_of(jnp.maximum(nf + e - 2, 0) * tk, tk), 0, (e - 2) % ns)
    o_ref[...] = (acc_ref[...] * _silu(g_ref[...].astype(F32))).astype(o_ref.dtype)


def sb_attention(z, batch, seq, tq=1024, tk=256, rt=256):
    T = z.shape[0]
    tq = min(tq, seq)
    tk = min(tk, tq)
    rt = min(rt, tk)
    nq = seq // tq
    ns = SB_SLOTS
    assert (tq // tk) % ns == 0, "slot of a block must not depend on the loop trip"
    hw = SB_W // SB_DIM
    scratch = ([pltpu.VMEM((tq, 2 * tk), F32)] * (ns // 2) + [pltpu.VMEM((tq, tk), BF16)] * ns
               + [pltpu.VMEM((tq, LANE), F32)] * ns + [pltpu.VMEM((tq, SB_DIM), F32)])
    return pl.pallas_call(
        functools.partial(_sb_kernel, tq=tq, tk=tk, rt=rt),
        grid=(batch, SB_HEADS, nq),
        in_specs=[
            pl.BlockSpec((tq, SB_DIM), lambda b, h, i: (b * nq + i, h)),
            pl.BlockSpec((seq, SB_DIM), lambda b, h, i: (b, hw + h)),
            pl.BlockSpec((seq, SB_DIM), lambda b, h, i: (b, 2 * hw + h)),
            pl.BlockSpec((tq, SB_DIM), lambda b, h, i: (b * nq + i, 3 * hw + h)),
        ],
        out_specs=pl.BlockSpec((tq, SB_DIM), lambda b, h, i: (b * nq + i, h)),
        out_shape=jax.ShapeDtypeStruct((T, SB_W), BF16),
        scratch_shapes=scratch,
        compiler_params=_cparams(("parallel", "parallel", "arbitrary")),
        name="sb_attention",
    )(z, z, z, z)


def _gla_kernel(q_ref, k_ref, v_ref, gg_ref, ga_ref, au_ref, ab_ref, ng_ref, o_ref, st_ref, incl_ref, *, nchunk):
    C, SUB, G = GLA_CHUNK, GLA_SUB, SUBLANE
    tb = nchunk * C

    @pl.when(pl.program_id(2) == 0)
    def _():
        st_ref[...] = jnp.zeros_like(st_ref)
        ri = lax.broadcasted_iota(jnp.int32, (tb, tb), 0)
        ci = lax.broadcasted_iota(jnp.int32, (tb, tb), 1)
        incl_ref[...] = jnp.where(ci <= ri, jnp.where(ci >= (ri & ~(C - 1)), 1.0, 0.0), 0.0).astype(BF16)

    grp_ri = lax.broadcasted_iota(jnp.int32, (G, C), 0)
    grp_ci = lax.broadcasted_iota(jnp.int32, (G, C), 1)

    g_hi, g_lo = _hi_lo(ga_ref[...])
    a_hi, a_lo = _hi_lo(au_ref[...])
    pre = _dot(g_hi, a_hi) + _dot(g_hi, a_lo) + _dot(g_lo, a_hi) + ab_ref[...]
    log_f = -(jnp.maximum(-pre, 0.0) + jnp.log1p(jnp.exp(-jnp.abs(pre)))) / GLA_GATE_NORM
    f_hi = log_f.astype(BF16)
    f_mid, f_lo = _hi_lo(log_f - f_hi.astype(F32))
    incl = incl_ref[...]
    b_all = (_dot(incl, f_hi) + _dot(incl, f_mid) + _dot(incl, f_lo)) * LOG2E

    st = st_ref[...]
    for c in range(nchunk):
        r0 = c * C
        q = q_ref[r0:r0 + C, :].astype(F32) * (GLA_DK ** -0.5)
        k = k_ref[r0:r0 + C, :].astype(F32)
        v = v_ref[r0:r0 + C, :]
        b = b_all[r0:r0 + C, :]
        b_last = b[C - 1:C, :]

        o_inter = _dot_nt((q * jnp.exp2(b)).astype(BF16), st.astype(BF16))

        att_rows = []
        for s in range(C // SUB):
            i0 = s * SUB
            if s > 0:
                b0 = b[i0:i0 + 1, :]
                q_dec = (q[i0:i0 + SUB, :] * jnp.exp2(b[i0:i0 + SUB, :] - b0)).astype(BF16)
                k_dec = (k * jnp.exp2(jnp.minimum(b0 - b, 0.0))).astype(BF16)
                att_far = _dot_nt(q_dec, k_dec)
            for g0 in range(i0, i0 + SUB, G):
                bg = b[g0:g0 + G, :]
                qg = q[g0:g0 + G, :]
                att = jnp.where(grp_ci < i0, att_far[g0 - i0:g0 - i0 + G, :], 0.0) if s > 0 else jnp.zeros((G, C), F32)
                for jj in range(i0, g0 + G):
                    a = jnp.sum(qg * k[jj:jj + 1, :] * jnp.exp2(bg - b[jj:jj + 1, :]), axis=-1, keepdims=True)
                    att = jnp.where(grp_ci == jj, a, att)
                att_rows.append(jnp.where(grp_ci <= grp_ri + g0, att, 0.0))
        att = jnp.concatenate(att_rows, axis=0)
        o = o_inter + _dot(att.astype(BF16), v)

        k_dec = (k * jnp.exp2(b_last - b)).astype(BF16)
        st = st * jnp.exp2(b_last) + _dot_tn(v, k_dec)

        y = _rms(o, ng_ref[...]) * _silu(gg_ref[r0:r0 + C, :].astype(F32))
        o_ref[r0:r0 + C, :] = y.astype(o_ref.dtype)
    st_ref[...] = st


def gla(z, ga, au, ab, ng, batch, seq, tb=512):
    T = z.shape[0]
    tb = min(tb, seq)
    nb = seq // tb
    q0 = 4 * SB_W // GLA_DK
    k0 = q0 + GLA_HEADS
    v0 = (4 * SB_W + 2 * GLA_KW) // GLA_DV
    g0 = v0 + GLA_HEADS
    return pl.pallas_call(
        functools.partial(_gla_kernel, nchunk=tb // GLA_CHUNK),
        grid=(batch, GLA_HEADS, nb),
        in_specs=[
            pl.BlockSpec((tb, GLA_DK), lambda b, h, i: (b * nb + i, q0 + h)),
            pl.BlockSpec((tb, GLA_DK), lambda b, h, i: (b * nb + i, k0 + h)),
            pl.BlockSpec((tb, GLA_DV), lambda b, h, i: (b * nb + i, v0 + h)),
            pl.BlockSpec((tb, GLA_DV), lambda b, h, i: (b * nb + i, g0 + h)),
            pl.BlockSpec((tb, LANE), lambda b, h, i: (b * nb + i, 0)),
            pl.BlockSpec((LANE, GLA_DK), lambda b, h, i: (0, h)),
            pl.BlockSpec((1, GLA_DK), lambda b, h, i: (0, h)),
            pl.BlockSpec((1, GLA_DV), lambda b, h, i: (0, 0)),
        ],
        out_specs=pl.BlockSpec((tb, GLA_DV), lambda b, h, i: (b * nb + i, h)),
        out_shape=jax.ShapeDtypeStruct((T, GLA_VW), BF16),
        scratch_shapes=[pltpu.VMEM((GLA_DV, GLA_DK), F32), pltpu.VMEM((tb, tb), BF16)],
        compiler_params=_cparams(("parallel", "parallel", "arbitrary")),
        name="gla",
    )(z, z, z, z, ga, au, ab, ng)


def _out_proj_kernel(y1_ref, y2_ref, w1_ref, w2_ref, x_ref, gf_ref, o_ref, *, final):
    acc = x_ref[...] + _dot(y1_ref[...], w1_ref[...]) + _dot(y2_ref[...], w2_ref[...])
    if final:
        acc = _rms(acc, gf_ref[...])
    o_ref[...] = acc


def out_proj(y1, c1, y2, c2, w, x, gf, final, tm=512):
    T, D = x.shape
    kh = w.shape[0] // 2
    tm = min(tm, T)
    return pl.pallas_call(
        functools.partial(_out_proj_kernel, final=final),
        grid=(T // tm,),
        in_specs=[
            pl.BlockSpec((tm, kh), lambda i: (i, c1)),
            pl.BlockSpec((tm, kh), lambda i: (i, c2)),
            pl.BlockSpec((kh, D), lambda i: (0, 0)),
            pl.BlockSpec((kh, D), lambda i: (1, 0)),
            pl.BlockSpec((tm, D), lambda i: (i, 0)),
            pl.BlockSpec((1, D), lambda i: (0, 0)),
        ],
        out_specs=pl.BlockSpec((tm, D), lambda i: (i, 0)),
        out_shape=jax.ShapeDtypeStruct((T, D), F32),
        compiler_params=_cparams(("parallel",)),
        name="out_proj",
    )(y1, y2, w, w, x, gf.reshape(1, D))


def _mla_up_kernel(lat_ref, pos_ref, inv_ref, qg_ref, kvg_ref, wq_ref, wkv_ref, q_ref, k_ref, v_ref, *, scale):
    hq = _rms(lat_ref[:, 0:MLA_QR], qg_ref[...]).astype(BF16)
    hkv = _rms(lat_ref[:, MLA_QR:MLA_QR + MLA_KVR], kvg_ref[...]).astype(BF16)
    ang = pos_ref[...].astype(F32) * inv_ref[...]
    lane = lax.broadcasted_iota(jnp.int32, ang.shape, 1)
    half = MLA_ROPE // 2
    cos2 = jnp.where(lane < MLA_ROPE, jnp.cos(ang), 0.0)
    sin2 = jnp.where(lane < half, -jnp.sin(ang), jnp.where(lane < MLA_ROPE, jnp.sin(ang), 0.0))

    def rope(r):
        return r * cos2 + pltpu.roll(r, MLA_ROPE, 1) * sin2

    kr = rope(lat_ref[:, MLA_QR + MLA_KVR:MLA_QR + MLA_KVR + LANE]).astype(BF16)
    kn = _dot(hkv, wkv_ref[:, 0:MLA_HEADS * MLA_NOPE]).astype(BF16)
    v_ref[...] = _dot(hkv, wkv_ref[:, MLA_HEADS * MLA_NOPE:]).astype(BF16)
    for h in range(MLA_HEADS):
        c0 = h * MLA_QK_PAD
        y = _dot(hq, wq_ref[:, c0:c0 + MLA_QK_PAD])
        q_ref[:, c0:c0 + MLA_NOPE] = (y[:, 0:MLA_NOPE] * scale).astype(BF16)
        q_ref[:, c0 + MLA_NOPE:c0 + MLA_QK_PAD] = (rope(y[:, MLA_NOPE:]) * scale).astype(BF16)
        k_ref[:, c0:c0 + MLA_NOPE] = kn[:, h * MLA_NOPE:(h + 1) * MLA_NOPE]
        k_ref[:, c0 + MLA_NOPE:c0 + MLA_QK_PAD] = kr


def mla_up(lat, pos, inv, qg, kvg, wq, wkv, tm=512):
    T = lat.shape[0]
    tm = min(tm, T)
    qk_w = MLA_HEADS * MLA_QK_PAD
    v_w = MLA_HEADS * MLA_V
    return pl.pallas_call(
        functools.partial(_mla_up_kernel, scale=(MLA_NOPE + MLA_ROPE) ** -0.5 * LOG2E),
        grid=(T // tm,),
        in_specs=[
            pl.BlockSpec((tm, lat.shape[1]), lambda i: (i, 0)),
            pl.BlockSpec((tm, 1), lambda i: (i, 0)),
            pl.BlockSpec((1, LANE), lambda i: (0, 0)),
            pl.BlockSpec((1, MLA_QR), lambda i: (0, 0)),
            pl.BlockSpec((1, MLA_KVR), lambda i: (0, 0)),
            pl.BlockSpec(wq.shape, lambda i: (0, 0)),
            pl.BlockSpec(wkv.shape, lambda i: (0, 0)),
        ],
        out_specs=[
            pl.BlockSpec((tm, qk_w), lambda i: (i, 0)),
            pl.BlockSpec((tm, qk_w), lambda i: (i, 0)),
            pl.BlockSpec((tm, v_w), lambda i: (i, 0)),
        ],
        out_shape=[
            jax.ShapeDtypeStruct((T, qk_w), BF16),
            jax.ShapeDtypeStruct((T, qk_w), BF16),
            jax.ShapeDtypeStruct((T, v_w), BF16),
        ],
        compiler_params=_cparams(("parallel",)),
        name="mla_up",
    )(lat, pos, inv, qg.reshape(1, -1), kvg.reshape(1, -1), wq, wkv)


def _mla_attn_kernel(q_ref, k_ref, v_ref, g_ref, o_ref, s0_ref, s1_ref, p0_ref, p1_ref, m_ref, l_ref, acc_ref,
                     *, tq, tk, rt, st):
    qi = pl.program_id(2)
    nd = tq // tk
    nf = qi * nd
    base = qi * tq
    reps = tk // LANE
    s_refs = (s0_ref, s1_ref)
    p_refs = (p0_ref, p1_ref)
    m_ref[...] = jnp.full_like(m_ref, -jnp.inf)
    l_ref[...] = jnp.zeros_like(l_ref)
    acc_ref[...] = jnp.zeros_like(acc_ref)

    def scores(start, r0, slot, r1=tq):
        s_refs[slot][r0:r1, :] = _dot_nt(q_ref[r0:r1, :], k_ref[pl.ds(start, tk), :])

    def update(start, r0, slot, masked):
        vb = v_ref[pl.ds(start, tk), :]
        for g0 in range(r0, tq, rt):
            alphas = []
            for t0 in range(g0, g0 + rt, st):
                rows = slice(t0, t0 + st)
                s = s_refs[slot][rows, :]
                if masked and t0 - r0 < tk:
                    col = lax.broadcasted_iota(jnp.int32, s.shape, 1)
                    row = lax.broadcasted_iota(jnp.int32, s.shape, 0)
                    s = jnp.where(col <= row + (t0 - r0), s, -jnp.inf)
                m_old = m_ref[rows, :]
                m_new = jnp.maximum(m_old, jnp.max(s, axis=-1, keepdims=True))
                alpha = jnp.exp2(m_old - m_new)
                p = jnp.exp2(s - jnp.tile(m_new, (1, reps)))
                l_ref[rows, :] = alpha * l_ref[rows, :] + jnp.sum(p, axis=-1, keepdims=True)
                p_refs[slot][rows, :] = p.astype(BF16)
                m_ref[rows, :] = m_new
                alphas.append(alpha)
            grp = slice(g0, g0 + rt)
            acc_ref[grp, :] = jnp.concatenate(alphas, axis=0) * acc_ref[grp, :] + _dot(p_refs[slot][grp, :], vb)

    scores(0, 0, 0, rt)

    def body(i, carry):
        for u in range(nd):
            j = i * nd + u
            if u == 0:
                scores(pl.multiple_of(j * tk, tk), rt, 0)
            scores(pl.multiple_of((j + 1) * tk, tk), 0, (u + 1) % 2, tq if u + 1 < nd else rt)
            update(pl.multiple_of(j * tk, tk), 0, u % 2, False)
        return carry

    lax.fori_loop(0, qi, body, 0)
    scores(pl.multiple_of(base, tk), rt, 0)
    for m in range(nd):
        if m + 1 < nd:
            scores(pl.multiple_of(base + (m + 1) * tk, tk), (m + 1) * tk, (m + 1) % 2)
        update(pl.multiple_of(base + m * tk, tk), m * tk, m % 2, True)
    o_ref[...] = (acc_ref[...] / l_ref[...] * _silu(g_ref[...].astype(F32))).astype(o_ref.dtype)


def mla_attention(qf, kf, v, gate, batch, seq, tq=2048, tk=512, rt=256):
    T = qf.shape[0]
    tq = min(tq, seq)
    tk = min(tk, tq)
    rt = min(rt, tk)
    nq = seq // tq
    assert (tq // tk) % 2 == 0 or nq == 1, "score-buffer parity is static only for an even block count per tile"
    return pl.pallas_call(
        functools.partial(_mla_attn_kernel, tq=tq, tk=tk, rt=rt, st=min(MLA_SOFTMAX_ROWS, rt)),
        grid=(batch, MLA_HEADS, nq),
        in_specs=[
            pl.BlockSpec((tq, MLA_QK_PAD), lambda b, h, i: (b * nq + i, h)),
            pl.BlockSpec((seq, MLA_QK_PAD), lambda b, h, i: (b, h)),
            pl.BlockSpec((seq, MLA_V), lambda b, h, i: (b, h)),
            pl.BlockSpec((tq, MLA_V), lambda b, h, i: (b * nq + i, h)),
        ],
        out_specs=pl.BlockSpec((tq, MLA_V), lambda b, h, i: (b * nq + i, h)),
        out_shape=jax.ShapeDtypeStruct((T, MLA_HEADS * MLA_V), BF16),
        scratch_shapes=[pltpu.VMEM((tq, tk), F32), pltpu.VMEM((tq, tk), F32),
                        pltpu.VMEM((tq, tk), BF16), pltpu.VMEM((tq, tk), BF16), pltpu.VMEM((tq, LANE), F32),
                        pltpu.VMEM((tq, LANE), F32), pltpu.VMEM((tq, MLA_V), F32)],
        compiler_params=_cparams(("parallel", "parallel", "arbitrary")),
        name="mla_attention",
    )(qf, kf, v, gate)


def _swap_halves(w):
    half = w.shape[-1] // 2
    return jnp.concatenate([w[..., half:], w[..., :half]], axis=-1)


def _prep_even(w_in, alpha_up, alpha_bias):
    col_scale = jnp.concatenate([jnp.full((SB_W,), -(SB_DIM ** -0.5) * LOG2E, F32), jnp.ones((EVEN_MAIN - SB_W,), F32)])
    w_main = (w_in[:, :EVEN_MAIN] * col_scale).astype(BF16)
    w_ga = jnp.pad(w_in[:, EVEN_MAIN:], ((0, 0), (0, LANE - GLA_RANK))).astype(BF16)
    au = jnp.pad(alpha_up, ((0, LANE - GLA_RANK), (0, 0)))
    return w_main, w_ga, au, alpha_bias.reshape(1, -1)


def _prep_odd(w_in, w_q_up, w_kv_up):
    o1 = MLA_QR + MLA_KVR
    w_kr = w_in[:, o1:o1 + MLA_ROPE]
    w_lat = jnp.concatenate([w_in[:, :o1], w_kr, _swap_halves(w_kr)], axis=1).astype(BF16)
    w_gate = w_in[:, o1 + MLA_ROPE:].astype(BF16)
    wq = w_q_up.reshape(MLA_QR, MLA_HEADS, MLA_NOPE + MLA_ROPE)
    wq_r = wq[..., MLA_NOPE:]
    wq = jnp.concatenate([wq[..., :MLA_NOPE], wq_r, _swap_halves(wq_r)], axis=-1)
    wq = wq.reshape(MLA_QR, MLA_HEADS * MLA_QK_PAD).astype(BF16)
    wkv = w_kv_up.reshape(MLA_KVR, MLA_HEADS, MLA_NOPE + MLA_V)
    wkv = jnp.concatenate([wkv[..., :MLA_NOPE].reshape(MLA_KVR, -1), wkv[..., MLA_NOPE:].reshape(MLA_KVR, -1)], axis=1)
    return w_lat, w_gate, wq, wkv.astype(BF16)


def _even_layer(x, batch, seq, norm_g, w_in, alpha_up, alpha_bias, gla_norm_g, w_out, final_g, final):
    w_main, w_ga, au, ab = _prep_even(w_in, alpha_up, alpha_bias)
    z, ga = norm_matmul(x, norm_g, w_main, BF16, w_ga, F32, tm=1024)
    y_a = sb_attention(z, batch, seq)
    y_b = gla(z, ga, au, ab, gla_norm_g.reshape(1, -1), batch, seq)
    return out_proj(y_a, 0, y_b, 0, w_out.astype(BF16), x, final_g, final)


def _odd_layer(x, pos, inv, batch, seq, norm_g, w_in, q_norm_g, w_q_up, kv_norm_g, w_kv_up, w_out, final_g, final):
    w_lat, w_gate, wq, wkv = _prep_odd(w_in, w_q_up, w_kv_up)
    gate, lat = norm_matmul(x, norm_g, w_gate, BF16, w_lat, F32, tm=512)
    qf, kf, v = mla_up(lat, pos, inv, q_norm_g, kv_norm_g, wq, wkv)
    y = mla_attention(qf, kf, v, gate, batch, seq)
    return out_proj(y, 0, y, 1, w_out.astype(BF16), x, final_g, final)


def kernel(x, positions, ln_even, w_in_even, gla_alpha_up, gla_alpha_bias, gla_norm, w_out_even,
           ln_odd, w_in_odd, q_norm, w_q_up, kv_norm, w_kv_up, w_out_odd, final_norm):
    batch, seq, d = x.shape
    depth = ln_even.shape[0] + ln_odd.shape[0]
    h = x.reshape(batch * seq, d)
    pos = positions.reshape(batch * seq, 1)
    half = MLA_ROPE // 2
    inv = ROPE_THETA ** (-jnp.arange(half, dtype=F32) / half)
    inv = jnp.concatenate([inv, inv, jnp.zeros((LANE - MLA_ROPE,), F32)]).reshape(1, LANE)
    for layer in range(depth):
        i = layer // 2
        final = layer == depth - 1
        if layer % 2 == 0:
            h = _even_layer(h, batch, seq, ln_even[i], w_in_even[i], gla_alpha_up[i], gla_alpha_bias[i],
                            gla_norm[i], w_out_even[i], final_norm, final)
        else:
            h = _odd_layer(h, pos, inv, batch, seq, ln_odd[i], w_in_odd[i], q_norm[i], w_q_up[i],
                           kv_norm[i], w_kv_up[i], w_out_odd[i], final_norm, final)
    return h.reshape(batch, seq, d)
```

```python
import functools

import jax
import jax.numpy as jnp
from jax import lax
from jax.experimental import pallas as pl
from jax.experimental.pallas import tpu as pltpu

F32 = jnp.float32
BF16 = jnp.bfloat16

SB_HEADS = 8
SB_DIM = 128
GLA_HEADS = 4
GLA_DK = 128
GLA_DV = 256
GLA_RANK = 16
GLA_GATE_NORM = 16.0
GLA_CHUNK = 64
GLA_SUB = 16
SB_SLOTS = 4
MLA_SOFTMAX_ROWS = 64
MLA_HEADS = 16
MLA_QR = 512
MLA_KVR = 512
MLA_NOPE = 128
MLA_ROPE = 64
MLA_V = 128
MLA_QK_PAD = 256
ROPE_THETA = 10000.0
EPS = 1e-6
LOG2E = 1.4426950408889634

LANE = 128
SUBLANE = 8
VMEM_LIMIT = 48 * 1024 * 1024

SB_W = SB_HEADS * SB_DIM
GLA_KW = GLA_HEADS * GLA_DK
GLA_VW = GLA_HEADS * GLA_DV
EVEN_MAIN = 4 * SB_W + 2 * GLA_KW + 2 * GLA_VW


def _cparams(sem):
    return pltpu.CompilerParams(dimension_semantics=sem, vmem_limit_bytes=VMEM_LIMIT)


def _rms(x, g):
    return x * lax.rsqrt(jnp.mean(x * x, axis=-1, keepdims=True) + EPS) * g


def _silu(g):
    return g * (1.0 / (1.0 + jnp.exp(-g)))


def _dot(a, b):
    return jnp.dot(a, b, preferred_element_type=F32)


def _hi_lo(x):
    hi = x.astype(BF16)
    return hi, (x - hi.astype(F32)).astype(BF16)


def _dot_nt(a, b):
    return lax.dot_general(a, b, (((1,), (1,)), ((), ())), preferred_element_type=F32)


def _dot_tn(a, b):
    return lax.dot_general(a, b, (((0,), (0,)), ((), ())), preferred_element_type=F32)


def _norm_matmul_kernel(x_ref, g_ref, w_ref, ws_ref, o_ref, os_ref, h_ref):
    @pl.when(pl.program_id(1) == 0)
    def _():
        h = _rms(x_ref[...], g_ref[...]).astype(BF16)
        h_ref[...] = h
        os_ref[...] = _dot(h, ws_ref[...]).astype(os_ref.dtype)

    o_ref[...] = _dot(h_ref[...], w_ref[...]).astype(o_ref.dtype)


def _pick_tile(n, cap):
    best = LANE
    for t in range(LANE, cap + 1, LANE):
        if n % t == 0:
            best = t
    return best


def norm_matmul(x, g, w, out_dtype, w_side, side_dtype, tm, tn_cap=1024):
    T, K = x.shape
    N = w.shape[1]
    ns = w_side.shape[1]
    tn = _pick_tile(N, tn_cap)
    tm = min(tm, T)
    return pl.pallas_call(
        _norm_matmul_kernel,
        grid=(T // tm, N // tn),
        in_specs=[
            pl.BlockSpec((tm, K), lambda i, j: (i, 0)),
            pl.BlockSpec((1, K), lambda i, j: (0, 0)),
            pl.BlockSpec((K, tn), lambda i, j: (0, j)),
            pl.BlockSpec((K, ns), lambda i, j: (0, 0), pipeline_mode=pl.Buffered(1)),
        ],
        out_specs=[
            pl.BlockSpec((tm, tn), lambda i, j: (i, j)),
            pl.BlockSpec((tm, ns), lambda i, j: (i, 0)),
        ],
        out_shape=[jax.ShapeDtypeStruct((T, N), out_dtype), jax.ShapeDtypeStruct((T, ns), side_dtype)],
        scratch_shapes=[pltpu.VMEM((tm, K), BF16)],
        compiler_params=_cparams(("parallel", "arbitrary")),
        name="norm_matmul",
    )(x, g.reshape(1, K), w, w_side)


def _sb_kernel(q_ref, k_ref, v_ref, g_ref, o_ref, *scratch, tq, tk, rt):
    qi = pl.program_id(2)
    nd = tq // tk
    ns = SB_SLOTS
    nf = qi * nd
    nzw_b = scratch[0:ns // 2]
    ls_b, tot_b = (scratch[ns // 2 + i * ns:ns // 2 + (i + 1) * ns] for i in range(2))
    acc_ref = scratch[ns // 2 + 2 * ns]
    krow = lax.broadcasted_iota(jnp.int32, (tk, tk), 0)
    kcol = lax.broadcasted_iota(jnp.int32, (tk, tk), 1)
    from_s = (krow >= kcol).astype(BF16)
    acc_ref[...] = jnp.zeros_like(acc_ref)
    @pl.when(qi == 0)
    def _():
        nzw_b[ns // 2 - 1][...] = jnp.full((tq, 2 * tk), jnp.inf, F32)
        ls_b[ns - 2][...] = jnp.zeros((tq, tk), BF16)
        tot_b[ns - 2][...] = jnp.zeros((tq, LANE), F32)

    def nz_view(slot, rows):
        return nzw_b[slot // 2].at[rows, (slot % 2) * tk:(slot % 2 + 1) * tk]

    def stage_a(start, r0, slot):
        nzw_b[slot // 2][r0:, :] = _dot_nt(q_ref[r0:, :], k_ref[pl.ds(start, 2 * tk), :])

    def stage_b(r0, slot, masked):
        for t0 in range(r0, tq, rt):
            rows = slice(t0, t0 + rt)
            nz = nz_view(slot, rows)[...]
            neg_abs = lax.bitcast_convert_type(lax.bitcast_convert_type(nz, jnp.uint32) | jnp.uint32(0x80000000), F32)
            log_stay = jnp.minimum(nz, 0.0) - jnp.log(1.0 + jnp.exp2(neg_abs)) * LOG2E
            if masked and t0 - r0 < tk:
                col = lax.broadcasted_iota(jnp.int32, nz.shape, 1)
                row = lax.broadcasted_iota(jnp.int32, nz.shape, 0)
                keep = col < row + (t0 - r0)
                log_stay = jnp.where(keep, log_stay, 0.0)
                nz_view(slot, rows)[...] = jnp.where(keep, nz, jnp.inf)
            ls_b[slot][rows, :] = log_stay.astype(BF16)
            tot_b[slot][rows, :] = jnp.broadcast_to(jnp.sum(log_stay, axis=-1, keepdims=True), (rt, LANE))

    def stage_c(start, r0, slot):
        vb = v_ref[pl.ds(start, tk), :]
        for t0 in range(r0, tq, rt):
            rows = slice(t0, t0 + rt)
            upto = _dot(ls_b[slot][rows, :], from_s)
            w = jnp.exp2(upto - nz_view(slot, rows)[...])
            acc_ref[rows, :] = acc_ref[rows, :] * jnp.exp2(tot_b[slot][rows, :]) + _dot(w.astype(BF16), vb)

    def trip(t, fill):
        for u in range(nd):
            g = t * nd + u
            if u % 2 == 0:
                stage_a(pl.multiple_of(g * tk, tk), 0, u % ns)
            if not (fill and u < 1):
                stage_b(0, (u - 1) % ns, False)
            if not (fill and u < 2):
                stage_c(pl.multiple_of((g - 2) * tk, tk), 0, (u - 2) % ns)

    @pl.when(qi > 0)
    def _():
        trip(0, True)

    def body(t, carry):
        trip(t, False)
        return carry

    lax.fori_loop(1, qi, body, 0)
    for e in range(nd + 2):
        if e < nd and e % 2 == 0:
            stage_a(pl.multiple_of((nf + e) * tk, tk), e * tk, e % ns)
        if 1 <= e <= nd:
            stage_b((e - 1) * tk, (e - 1) % ns, True)
        elif e == 0:
            stage_b(0, ns - 1, False)
        if e >= 2:
            stage_c(pl.multiple_of((nf + e - 2) * tk, tk), (e - 2) * tk, (e - 2) % ns)
        else:
            stage_c(pl.multiple_of(jnp.maximum(nf + e - 2, 0) * tk, tk), 0, (e - 2) % ns)
    o_ref[...] = (acc_ref[...] * _silu(g_ref[...].astype(F32))).astype(o_ref.dtype)


def sb_attention(z, batch, seq, tq=1024, tk=256, rt=256):
    T = z.shape[0]
    tq = min(tq, seq)
    tk = min(tk, tq)
    rt = min(rt, tk)
    nq = seq // tq
    ns = SB_SLOTS
    assert (tq // tk) % ns == 0, "slot of a block must not depend on the loop trip"
    hw = SB_W // SB_DIM
    scratch = ([pltpu.VMEM((tq, 2 * tk), F32)] * (ns // 2) + [pltpu.VMEM((tq, tk), BF16)] * ns
               + [pltpu.VMEM((tq, LANE), F32)] * ns + [pltpu.VMEM((tq, SB_DIM), F32)])
    return pl.pallas_call(
        functools.partial(_sb_kernel, tq=tq, tk=tk, rt=rt),
        grid=(batch, SB_HEADS, nq),
        in_specs=[
            pl.BlockSpec((tq, SB_DIM), lambda b, h, i: (b * nq + i, h)),
            pl.BlockSpec((seq, SB_DIM), lambda b, h, i: (b, hw + h)),
            pl.BlockSpec((seq, SB_DIM), lambda b, h, i: (b, 2 * hw + h)),
            pl.BlockSpec((tq, SB_DIM), lambda b, h, i: (b * nq + i, 3 * hw + h)),
        ],
        out_specs=pl.BlockSpec((tq, SB_DIM), lambda b, h, i: (b * nq + i, h)),
        out_shape=jax.ShapeDtypeStruct((T, SB_W), BF16),
        scratch_shapes=scratch,
        compiler_params=_cparams(("parallel", "parallel", "arbitrary")),
        name="sb_attention",
    )(z, z, z, z)


def _gla_kernel(q_ref, k_ref, v_ref, gg_ref, ga_ref, au_ref, ab_ref, ng_ref, o_ref, st_ref, incl_ref, *, nchunk):
    C, SUB, G = GLA_CHUNK, GLA_SUB, SUBLANE
    tb = nchunk * C

    @pl.when(pl.program_id(2) == 0)
    def _():
        st_ref[...] = jnp.zeros_like(st_ref)
        ri = lax.broadcasted_iota(jnp.int32, (tb, tb), 0)
        ci = lax.broadcasted_iota(jnp.int32, (tb, tb), 1)
        incl_ref[...] = jnp.where(ci <= ri, jnp.where(ci >= (ri & ~(C - 1)), 1.0, 0.0), 0.0).astype(BF16)

    grp_ri = lax.broadcasted_iota(jnp.int32, (G, C), 0)
    grp_ci = lax.broadcasted_iota(jnp.int32, (G, C), 1)

    g_hi, g_lo = _hi_lo(ga_ref[...])
    a_hi, a_lo = _hi_lo(au_ref[...])
    pre = _dot(g_hi, a_hi) + _dot(g_hi, a_lo) + _dot(g_lo, a_hi) + ab_ref[...]
    log_f = -(jnp.maximum(-pre, 0.0) + jnp.log1p(jnp.exp(-jnp.abs(pre)))) / GLA_GATE_NORM
    f_hi = log_f.astype(BF16)
    f_mid, f_lo = _hi_lo(log_f - f_hi.astype(F32))
    incl = incl_ref[...]
    b_all = (_dot(incl, f_hi) + _dot(incl, f_mid) + _dot(incl, f_lo)) * LOG2E

    st = st_ref[...]
    for c in range(nchunk):
        r0 = c * C
        q = q_ref[r0:r0 + C, :].astype(F32) * (GLA_DK ** -0.5)
        k = k_ref[r0:r0 + C, :].astype(F32)
        v = v_ref[r0:r0 + C, :]
        b = b_all[r0:r0 + C, :]
        b_last = b[C - 1:C, :]

        o_inter = _dot_nt((q * jnp.exp2(b)).astype(BF16), st.astype(BF16))

        att_rows = []
        for s in range(C // SUB):
            i0 = s * SUB
            if s > 0:
                b0 = b[i0:i0 + 1, :]
                q_dec = (q[i0:i0 + SUB, :] * jnp.exp2(b[i0:i0 + SUB, :] - b0)).astype(BF16)
                k_dec = (k * jnp.exp2(jnp.minimum(b0 - b, 0.0))).astype(BF16)
                att_far = _dot_nt(q_dec, k_dec)
            for g0 in range(i0, i0 + SUB, G):
                bg = b[g0:g0 + G, :]
                qg = q[g0:g0 + G, :]
                att = jnp.where(grp_ci < i0, att_far[g0 - i0:g0 - i0 + G, :], 0.0) if s > 0 else jnp.zeros((G, C), F32)
                for jj in range(i0, g0 + G):
                    a = jnp.sum(qg * k[jj:jj + 1, :] * jnp.exp2(bg - b[jj:jj + 1, :]), axis=-1, keepdims=True)
                    att = jnp.where(grp_ci == jj, a, att)
                att_rows.append(jnp.where(grp_ci <= grp_ri + g0, att, 0.0))
        att = jnp.concatenate(att_rows, axis=0)
        o = o_inter + _dot(att.astype(BF16), v)

        k_dec = (k * jnp.exp2(b_last - b)).astype(BF16)
        st = st * jnp.exp2(b_last) + _dot_tn(v, k_dec)

        y = _rms(o, ng_ref[...]) * _silu(gg_ref[r0:r0 + C, :].astype(F32))
        o_ref[r0:r0 + C, :] = y.astype(o_ref.dtype)
    st_ref[...] = st


def gla(z, ga, au, ab, ng, batch, seq, tb=512):
    T = z.shape[0]
    tb = min(tb, seq)
    nb = seq // tb
    q0 = 4 * SB_W // GLA_DK
    k0 = q0 + GLA_HEADS
    v0 = (4 * SB_W + 2 * GLA_KW) // GLA_DV
    g0 = v0 + GLA_HEADS
    return pl.pallas_call(
        functools.partial(_gla_kernel, nchunk=tb // GLA_CHUNK),
        grid=(batch, GLA_HEADS, nb),
        in_specs=[
            pl.BlockSpec((tb, GLA_DK), lambda b, h, i: (b * nb + i, q0 + h)),
            pl.BlockSpec((tb, GLA_DK), lambda b, h, i: (b * nb + i, k0 + h)),
            pl.BlockSpec((tb, GLA_DV), lambda b, h, i: (b * nb + i, v0 + h)),
            pl.BlockSpec((tb, GLA_DV), lambda b, h, i: (b * nb + i, g0 + h)),
            pl.BlockSpec((tb, LANE), lambda b, h, i: (b * nb + i, 0)),
            pl.BlockSpec((LANE, GLA_DK), lambda b, h, i: (0, h)),
            pl.BlockSpec((1, GLA_DK), lambda b, h, i: (0, h)),
            pl.BlockSpec((1, GLA_DV), lambda b, h, i: (0, 0)),
        ],
        out_specs=pl.BlockSpec((tb, GLA_DV), lambda b, h, i: (b * nb + i, h)),
        out_shape=jax.ShapeDtypeStruct((T, GLA_VW), BF16),
        scratch_shapes=[pltpu.VMEM((GLA_DV, GLA_DK), F32), pltpu.VMEM((tb, tb), BF16)],
        compiler_params=_cparams(("parallel", "parallel", "arbitrary")),
        name="gla",
    )(z, z, z, z, ga, au, ab, ng)


def _out_proj_kernel(y1_ref, y2_ref, w1_ref, w2_ref, x_ref, gf_ref, o_ref, *, final):
    acc = x_ref[...] + _dot(y1_ref[...], w1_ref[...]) + _dot(y2_ref[...], w2_ref[...])
    if final:
        acc = _rms(acc, gf_ref[...])
    o_ref[...] = acc


def out_proj(y1, c1, y2, c2, w, x, gf, final, tm=512):
    T, D = x.shape
    kh = w.shape[0] // 2
    tm = min(tm, T)
    return pl.pallas_call(
        functools.partial(_out_proj_kernel, final=final),
        grid=(T // tm,),
        in_specs=[
            pl.BlockSpec((tm, kh), lambda i: (i, c1)),
            pl.BlockSpec((tm, kh), lambda i: (i, c2)),
            pl.BlockSpec((kh, D), lambda i: (0, 0)),
            pl.BlockSpec((kh, D), lambda i: (1, 0)),
            pl.BlockSpec((tm, D), lambda i: (i, 0)),
            pl.BlockSpec((1, D), lambda i: (0, 0)),
        ],
        out_specs=pl.BlockSpec((tm, D), lambda i: (i, 0)),
        out_shape=jax.ShapeDtypeStruct((T, D), F32),
        compiler_params=_cparams(("parallel",)),
        name="out_proj",
    )(y1, y2, w, w, x, gf.reshape(1, D))


def _mla_up_kernel(lat_ref, pos_ref, inv_ref, qg_ref, kvg_ref, wq_ref, wkv_ref, q_ref, k_ref, v_ref, *, scale):
    hq = _rms(lat_ref[:, 0:MLA_QR], qg_ref[...]).astype(BF16)
    hkv = _rms(lat_ref[:, MLA_QR:MLA_QR + MLA_KVR], kvg_ref[...]).astype(BF16)
    ang = pos_ref[...].astype(F32) * inv_ref[...]
    lane = lax.broadcasted_iota(jnp.int32, ang.shape, 1)
    half = MLA_ROPE // 2
    cos2 = jnp.where(lane < MLA_ROPE, jnp.cos(ang), 0.0)
    sin2 = jnp.where(lane < half, -jnp.sin(ang), jnp.where(lane < MLA_ROPE, jnp.sin(ang), 0.0))

    def rope(r):
        return r * cos2 + pltpu.roll(r, MLA_ROPE, 1) * sin2

    kr = rope(lat_ref[:, MLA_QR + MLA_KVR:MLA_QR + MLA_KVR + LANE]).astype(BF16)
    kn = _dot(hkv, wkv_ref[:, 0:MLA_HEADS * MLA_NOPE]).astype(BF16)
    v_ref[...] = _dot(hkv, wkv_ref[:, MLA_HEADS * MLA_NOPE:]).astype(BF16)
    for h in range(MLA_HEADS):
        c0 = h * MLA_QK_PAD
        y = _dot(hq, wq_ref[:, c0:c0 + MLA_QK_PAD])
        q_ref[:, c0:c0 + MLA_NOPE] = (y[:, 0:MLA_NOPE] * scale).astype(BF16)
        q_ref[:, c0 + MLA_NOPE:c0 + MLA_QK_PAD] = (rope(y[:, MLA_NOPE:]) * scale).astype(BF16)
        k_ref[:, c0:c0 + MLA_NOPE] = kn[:, h * MLA_NOPE:(h + 1) * MLA_NOPE]
        k_ref[:, c0 + MLA_NOPE:c0 + MLA_QK_PAD] = kr


def mla_up(lat, pos, inv, qg, kvg, wq, wkv, tm=512):
    T = lat.shape[0]
    tm = min(tm, T)
    qk_w = MLA_HEADS * MLA_QK_PAD
    v_w = MLA_HEADS * MLA_V
    return pl.pallas_call(
        functools.partial(_mla_up_kernel, scale=(MLA_NOPE + MLA_ROPE) ** -0.5 * LOG2E),
        grid=(T // tm,),
        in_specs=[
            pl.BlockSpec((tm, lat.shape[1]), lambda i: (i, 0)),
            pl.BlockSpec((tm, 1), lambda i: (i, 0)),
            pl.BlockSpec((1, LANE), lambda i: (0, 0)),
            pl.BlockSpec((1, MLA_QR), lambda i: (0, 0)),
            pl.BlockSpec((1, MLA_KVR), lambda i: (0, 0)),
            pl.BlockSpec(wq.shape, lambda i: (0, 0)),
            pl.BlockSpec(wkv.shape, lambda i: (0, 0)),
        ],
        out_specs=[
            pl.BlockSpec((tm, qk_w), lambda i: (i, 0)),
            pl.BlockSpec((tm, qk_w), lambda i: (i, 0)),
            pl.BlockSpec((tm, v_w), lambda i: (i, 0)),
        ],
        out_shape=[
            jax.ShapeDtypeStruct((T, qk_w), BF16),
            jax.ShapeDtypeStruct((T, qk_w), BF16),
            jax.ShapeDtypeStruct((T, v_w), BF16),
        ],
        compiler_params=_cparams(("parallel",)),
        name="mla_up",
    )(lat, pos, inv, qg.reshape(1, -1), kvg.reshape(1, -1), wq, wkv)


def _mla_attn_kernel(q_ref, k_ref, v_ref, g_ref, o_ref, s0_ref, s1_ref, p0_ref, p1_ref, m_ref, l_ref, acc_ref,
                     *, tq, tk, rt, st):
    qi = pl.program_id(2)
    nd = tq // tk
    nf = qi * nd
    base = qi * tq
    reps = tk // LANE
    s_refs = (s0_ref, s1_ref)
    p_refs = (p0_ref, p1_ref)
    m_ref[...] = jnp.full_like(m_ref, -jnp.inf)
    l_ref[...] = jnp.zeros_like(l_ref)
    acc_ref[...] = jnp.zeros_like(acc_ref)

    def scores(start, r0, slot, r1=tq):
        s_refs[slot][r0:r1, :] = _dot_nt(q_ref[r0:r1, :], k_ref[pl.ds(start, tk), :])

    def update(start, r0, slot, masked):
        vb = v_ref[pl.ds(start, tk), :]
        for g0 in range(r0, tq, rt):
            alphas = []
            for t0 in range(g0, g0 + rt, st):
                rows = slice(t0, t0 + st)
                s = s_refs[slot][rows, :]
                if masked and t0 - r0 < tk:
                    col = lax.broadcasted_iota(jnp.int32, s.shape, 1)
                    row = lax.broadcasted_iota(jnp.int32, s.shape, 0)
                    s = jnp.where(col <= row + (t0 - r0), s, -jnp.inf)
                m_old = m_ref[rows, :]
                m_new = jnp.maximum(m_old, jnp.max(s, axis=-1, keepdims=True))
                alpha = jnp.exp2(m_old - m_new)
                p = jnp.exp2(s - jnp.tile(m_new, (1, reps)))
                l_ref[rows, :] = alpha * l_ref[rows, :] + jnp.sum(p, axis=-1, keepdims=True)
                p_refs[slot][rows, :] = p.astype(BF16)
                m_ref[rows, :] = m_new
                alphas.append(alpha)
            grp = slice(g0, g0 + rt)
            acc_ref[grp, :] = jnp.concatenate(alphas, axis=0) * acc_ref[grp, :] + _dot(p_refs[slot][grp, :], vb)

    scores(0, 0, 0, rt)

    def body(i, carry):
        for u in range(nd):
            j = i * nd + u
            if u == 0:
                scores(pl.multiple_of(j * tk, tk), rt, 0)
            scores(pl.multiple_of((j + 1) * tk, tk), 0, (u + 1) % 2, tq if u + 1 < nd else rt)
            update(pl.multiple_of(j * tk, tk), 0, u % 2, False)
        return carry

    lax.fori_loop(0, qi, body, 0)
    scores(pl.multiple_of(base, tk), rt, 0)
    for m in range(nd):
        if m + 1 < nd:
            scores(pl.multiple_of(base + (m + 1) * tk, tk), (m + 1) * tk, (m + 1) % 2)
        update(pl.multiple_of(base + m * tk, tk), m * tk, m % 2, True)
    o_ref[...] = (acc_ref[...] / l_ref[...] * _silu(g_ref[...].astype(F32))).astype(o_ref.dtype)


def mla_attention(qf, kf, v, gate, batch, seq, tq=2048, tk=512, rt=256):
    T = qf.shape[0]
    tq = min(tq, seq)
    tk = min(tk, tq)
    rt = min(rt, tk)
    nq = seq // tq
    assert (tq // tk) % 2 == 0 or nq == 1, "score-buffer parity is static only for an even block count per tile"
    return pl.pallas_call(
        functools.partial(_mla_attn_kernel, tq=tq, tk=tk, rt=rt, st=min(MLA_SOFTMAX_ROWS, rt)),
        grid=(batch, MLA_HEADS, nq),
        in_specs=[
            pl.BlockSpec((tq, MLA_QK_PAD), lambda b, h, i: (b * nq + i, h)),
            pl.BlockSpec((seq, MLA_QK_PAD), lambda b, h, i: (b, h)),
            pl.BlockSpec((seq, MLA_V), lambda b, h, i: (b, h)),
            pl.BlockSpec((tq, MLA_V), lambda b, h, i: (b * nq + i, h)),
        ],
        out_specs=pl.BlockSpec((tq, MLA_V), lambda b, h, i: (b * nq + i, h)),
        out_shape=jax.ShapeDtypeStruct((T, MLA_HEADS * MLA_V), BF16),
        scratch_shapes=[pltpu.VMEM((tq, tk), F32), pltpu.VMEM((tq, tk), F32),
                        pltpu.VMEM((tq, tk), BF16), pltpu.VMEM((tq, tk), BF16), pltpu.VMEM((tq, LANE), F32),
                        pltpu.VMEM((tq, LANE), F32), pltpu.VMEM((tq, MLA_V), F32)],
        compiler_params=_cparams(("parallel", "parallel", "arbitrary")),
        name="mla_attention",
    )(qf, kf, v, gate)


def _swap_halves(w):
    half = w.shape[-1] // 2
    return jnp.concatenate([w[..., half:], w[..., :half]], axis=-1)


def _prep_even(w_in, alpha_up, alpha_bias):
    col_scale = jnp.concatenate([jnp.full((SB_W,), -(SB_DIM ** -0.5) * LOG2E, F32), jnp.ones((EVEN_MAIN - SB_W,), F32)])
    w_main = (w_in[:, :EVEN_MAIN] * col_scale).astype(BF16)
    w_ga = jnp.pad(w_in[:, EVEN_MAIN:], ((0, 0), (0, LANE - GLA_RANK))).astype(BF16)
    au = jnp.pad(alpha_up, ((0, LANE - GLA_RANK), (0, 0)))
    return w_main, w_ga, au, alpha_bias.reshape(1, -1)


def _prep_odd(w_in, w_q_up, w_kv_up):
    o1 = MLA_QR + MLA_KVR
    w_kr = w_in[:, o1:o1 + MLA_ROPE]
    w_lat = jnp.concatenate([w_in[:, :o1], w_kr, _swap_halves(w_kr)], axis=1).astype(BF16)
    w_gate = w_in[:, o1 + MLA_ROPE:].astype(BF16)
    wq = w_q_up.reshape(MLA_QR, MLA_HEADS, MLA_NOPE + MLA_ROPE)
    wq_r = wq[..., MLA_NOPE:]
    wq = jnp.concatenate([wq[..., :MLA_NOPE], wq_r, _swap_halves(wq_r)], axis=-1)
    wq = wq.reshape(MLA_QR, MLA_HEADS * MLA_QK_PAD).astype(BF16)
    wkv = w_kv_up.reshape(MLA_KVR, MLA_HEADS, MLA_NOPE + MLA_V)
    wkv = jnp.concatenate([wkv[..., :MLA_NOPE].reshape(MLA_KVR, -1), wkv[..., MLA_NOPE:].reshape(MLA_KVR, -1)], axis=1)
    return w_lat, w_gate, wq, wkv.astype(BF16)


def _even_layer(x, batch, seq, norm_g, w_in, alpha_up, alpha_bias, gla_norm_g, w_out, final_g, final):
    w_main, w_ga, au, ab = _prep_even(w_in, alpha_up, alpha_bias)
    z, ga = norm_matmul(x, norm_g, w_main, BF16, w_ga, F32, tm=1024)
    y_a = sb_attention(z, batch, seq)
    y_b = gla(z, ga, au, ab, gla_norm_g.reshape(1, -1), batch, seq)
    return out_proj(y_a, 0, y_b, 0, w_out.astype(BF16), x, final_g, final)


def _odd_layer(x, pos, inv, batch, seq, norm_g, w_in, q_norm_g, w_q_up, kv_norm_g, w_kv_up, w_out, final_g, final):
    w_lat, w_gate, wq, wkv = _prep_odd(w_in, w_q_up, w_kv_up)
    gate, lat = norm_matmul(x, norm_g, w_gate, BF16, w_lat, F32, tm=1024, tn_cap=512)
    qf, kf, v = mla_up(lat, pos, inv, q_norm_g, kv_norm_g, wq, wkv)
    y = mla_attention(qf, kf, v, gate, batch, seq)
    return out_proj(y, 0, y, 1, w_out.astype(BF16), x, final_g, final)


def kernel(x, positions, ln_even, w_in_even, gla_alpha_up, gla_alpha_bias, gla_norm, w_out_even,
           ln_odd, w_in_odd, q_norm, w_q_up, kv_norm, w_kv_up, w_out_odd, final_norm):
    batch, seq, d = x.shape
    depth = ln_even.shape[0] + ln_odd.shape[0]
    h = x.reshape(batch * seq, d)
    pos = positions.reshape(batch * seq, 1)
    half = MLA_ROPE // 2
    inv = ROPE_THETA ** (-jnp.arange(half, dtype=F32) / half)
    inv = jnp.concatenate([inv, inv, jnp.zeros((LANE - MLA_ROPE,), F32)]).reshape(1, LANE)
    for layer in range(depth):
        i = layer // 2
        final = layer == depth - 1
        if layer % 2 == 0:
            h = _even_layer(h, batch, seq, ln_even[i], w_in_even[i], gla_alpha_up[i], gla_alpha_bias[i],
                            gla_norm[i], w_out_even[i], final_norm, final)
        else:
            h = _odd_layer(h, pos, inv, batch, seq, ln_odd[i], w_in_odd[i], q_norm[i], w_q_up[i],
                           kv_norm[i], w_kv_up[i], w_out_odd[i], final_norm, final)
    return h.reshape(batch, seq, d)
```

```python
import functools

import jax
import jax.numpy as jnp
from jax import lax
from jax.experimental import pallas as pl
from jax.experimental.pallas import tpu as pltpu

F32 = jnp.float32
BF16 = jnp.bfloat16

SB_HEADS = 8
SB_DIM = 128
GLA_HEADS = 4
GLA_DK = 128
GLA_DV = 256
GLA_RANK = 16
GLA_GATE_NORM = 16.0
GLA_CHUNK = 64
GLA_SUB = 16
SB_SLOTS = 4
MLA_SOFTMAX_ROWS = 64
MLA_HEADS = 16
MLA_QR = 512
MLA_KVR = 512
MLA_NOPE = 128
MLA_ROPE = 64
MLA_V = 128
MLA_QK_PAD = 256
ROPE_THETA = 10000.0
EPS = 1e-6
LOG2E = 1.4426950408889634

LANE = 128
SUBLANE = 8
VMEM_LIMIT = 48 * 1024 * 1024

SB_W = SB_HEADS * SB_DIM
GLA_KW = GLA_HEADS * GLA_DK
GLA_VW = GLA_HEADS * GLA_DV
EVEN_MAIN = 4 * SB_W + 2 * GLA_KW + 2 * GLA_VW


def _cparams(sem):
    return pltpu.CompilerParams(dimension_semantics=sem, vmem_limit_bytes=VMEM_LIMIT)


def _rms(x, g):
    return x * lax.rsqrt(jnp.mean(x * x, axis=-1, keepdims=True) + EPS) * g


def _silu(g):
    return g * (1.0 / (1.0 + jnp.exp(-g)))


def _dot(a, b):
    return jnp.dot(a, b, preferred_element_type=F32)


def _hi_lo(x):
    hi = x.astype(BF16)
    return hi, (x - hi.astype(F32)).astype(BF16)


def _dot_nt(a, b):
    return lax.dot_general(a, b, (((1,), (1,)), ((), ())), preferred_element_type=F32)


def _dot_tn(a, b):
    return lax.dot_general(a, b, (((0,), (0,)), ((), ())), preferred_element_type=F32)


def _norm_matmul_kernel(x_ref, g_ref, w_ref, ws_ref, o_ref, os_ref, h_ref):
    @pl.when(pl.program_id(1) == 0)
    def _():
        h = _rms(x_ref[...], g_ref[...]).astype(BF16)
        h_ref[...] = h
        os_ref[...] = _dot(h, ws_ref[...]).astype(os_ref.dtype)

    o_ref[...] = _dot(h_ref[...], w_ref[...]).astype(o_ref.dtype)


def _pick_tile(n, cap):
    best = LANE
    for t in range(LANE, cap + 1, LANE):
        if n % t == 0:
            best = t
    return best


def norm_matmul(x, g, w, out_dtype, w_side, side_dtype, tm, tn_cap=1024):
    T, K = x.shape
    N = w.shape[1]
    ns = w_side.shape[1]
    tn = _pick_tile(N, tn_cap)
    tm = min(tm, T)
    return pl.pallas_call(
        _norm_matmul_kernel,
        grid=(T // tm, N // tn),
        in_specs=[
            pl.BlockSpec((tm, K), lambda i, j: (i, 0)),
            pl.BlockSpec((1, K), lambda i, j: (0, 0)),
            pl.BlockSpec((K, tn), lambda i, j: (0, j)),
            pl.BlockSpec((K, ns), lambda i, j: (0, 0), pipeline_mode=pl.Buffered(1)),
        ],
        out_specs=[
            pl.BlockSpec((tm, tn), lambda i, j: (i, j)),
            pl.BlockSpec((tm, ns), lambda i, j: (i, 0)),
        ],
        out_shape=[jax.ShapeDtypeStruct((T, N), out_dtype), jax.ShapeDtypeStruct((T, ns), side_dtype)],
        scratch_shapes=[pltpu.VMEM((tm, K), BF16)],
        compiler_params=_cparams(("parallel", "arbitrary")),
        name="norm_matmul",
    )(x, g.reshape(1, K), w, w_side)


def _sb_kernel(q_ref, k_ref, v_ref, g_ref, o_ref, *scratch, tq, tk, rt):
    qi = pl.program_id(2)
    nd = tq // tk
    ns = SB_SLOTS
    nf = qi * nd
    nzw_b = scratch[0:ns // 2]
    ls_b, tot_b = (scratch[ns // 2 + i * ns:ns // 2 + (i + 1) * ns] for i in range(2))
    acc_ref = scratch[ns // 2 + 2 * ns]
    krow = lax.broadcasted_iota(jnp.int32, (tk, tk), 0)
    kcol = lax.broadcasted_iota(jnp.int32, (tk, tk), 1)
    from_s = (krow >= kcol).astype(BF16)
    acc_ref[...] = jnp.zeros_like(acc_ref)
    @pl.when(qi == 0)
    def _():
        nzw_b[ns // 2 - 1][...] = jnp.full((tq, 2 * tk), jnp.inf, F32)
        ls_b[ns - 2][...] = jnp.zeros((tq, tk), BF16)
        tot_b[ns - 2][...] = jnp.zeros((tq, LANE), F32)

    def nz_view(slot, rows):
        return nzw_b[slot // 2].at[rows, (slot % 2) * tk:(slot % 2 + 1) * tk]

    def stage_a(start, r0, slot):
        nzw_b[slot // 2][r0:, :] = _dot_nt(q_ref[r0:, :], k_ref[pl.ds(start, 2 * tk), :])

    def stage_b(r0, slot, masked):
        for t0 in range(r0, tq, rt):
            rows = slice(t0, t0 + rt)
            nz = nz_view(slot, rows)[...]
            neg_abs = lax.bitcast_convert_type(lax.bitcast_convert_type(nz, jnp.uint32) | jnp.uint32(0x80000000), F32)
            log_stay = jnp.minimum(nz, 0.0) - jnp.log(1.0 + jnp.exp2(neg_abs)) * LOG2E
            if masked and t0 - r0 < tk:
                col = lax.broadcasted_iota(jnp.int32, nz.shape, 1)
                row = lax.broadcasted_iota(jnp.int32, nz.shape, 0)
                keep = col < row + (t0 - r0)
                log_stay = jnp.where(keep, log_stay, 0.0)
                nz_view(slot, rows)[...] = jnp.where(keep, nz, jnp.inf)
            ls_b[slot][rows, :] = log_stay.astype(BF16)
            tot_b[slot][rows, :] = jnp.broadcast_to(jnp.sum(log_stay, axis=-1, keepdims=True), (rt, LANE))

    def stage_c(start, r0, slot):
        vb = v_ref[pl.ds(start, tk), :]
        for t0 in range(r0, tq, rt):
            rows = slice(t0, t0 + rt)
            upto = _dot(ls_b[slot][rows, :], from_s)
            w = jnp.exp2(upto - nz_view(slot, rows)[...])
            acc_ref[rows, :] = acc_ref[rows, :] * jnp.exp2(tot_b[slot][rows, :]) + _dot(w.astype(BF16), vb)

    def trip(t, fill):
        for u in range(nd):
            g = t * nd + u
            if u % 2 == 0:
                stage_a(pl.multiple_of(g * tk, tk), 0, u % ns)
            if not (fill and u < 1):
                stage_b(0, (u - 1) % ns, False)
            if not (fill and u < 2):
                stage_c(pl.multiple_of((g - 2) * tk, tk), 0, (u - 2) % ns)

    @pl.when(qi > 0)
    def _():
        trip(0, True)

    def body(t, carry):
        trip(t, False)
        return carry

    lax.fori_loop(1, qi, body, 0)
    for e in range(nd + 2):
        if e < nd and e % 2 == 0:
            stage_a(pl.multiple_of((nf + e) * tk, tk), e * tk, e % ns)
        if 1 <= e <= nd:
            stage_b((e - 1) * tk, (e - 1) % ns, True)
        elif e == 0:
            stage_b(0, ns - 1, False)
        if e >= 2:
            stage_c(pl.multiple_of((nf + e - 2) * tk, tk), (e - 2) * tk, (e - 2) % ns)
        else:
            stage_c(pl.multiple_of(jnp.maximum(nf + e - 2, 0) * tk, tk), 0, (e - 2) % ns)
    o_ref[...] = (acc_ref[...] * _silu(g_ref[...].astype(F32))).astype(o_ref.dtype)


def sb_attention(z, batch, seq, tq=2048, tk=256, rt=256):
    T = z.shape[0]
    tq = min(tq, seq)
    tk = min(tk, tq)
    rt = min(rt, tk)
    nq = seq // tq
    ns = SB_SLOTS
    assert (tq // tk) % ns == 0, "slot of a block must not depend on the loop trip"
    hw = SB_W // SB_DIM
    scratch = ([pltpu.VMEM((tq, 2 * tk), F32)] * (ns // 2) + [pltpu.VMEM((tq, tk), BF16)] * ns
               + [pltpu.VMEM((tq, LANE), F32)] * ns + [pltpu.VMEM((tq, SB_DIM), F32)])
    return pl.pallas_call(
        functools.partial(_sb_kernel, tq=tq, tk=tk, rt=rt),
        grid=(batch, SB_HEADS, nq),
        in_specs=[
            pl.BlockSpec((tq, SB_DIM), lambda b, h, i: (b * nq + i, h)),
            pl.BlockSpec((seq, SB_DIM), lambda b, h, i: (b, hw + h)),
            pl.BlockSpec((seq, SB_DIM), lambda b, h, i: (b, 2 * hw + h)),
            pl.BlockSpec((tq, SB_DIM), lambda b, h, i: (b * nq + i, 3 * hw + h)),
        ],
        out_specs=pl.BlockSpec((tq, SB_DIM), lambda b, h, i: (b * nq + i, h)),
        out_shape=jax.ShapeDtypeStruct((T, SB_W), BF16),
        scratch_shapes=scratch,
        compiler_params=_cparams(("parallel", "parallel", "arbitrary")),
        name="sb_attention",
    )(z, z, z, z)


def _gla_kernel(q_ref, k_ref, v_ref, gg_ref, ga_ref, au_ref, ab_ref, ng_ref, o_ref, st_ref, incl_ref, *, nchunk):
    C, SUB, G = GLA_CHUNK, GLA_SUB, SUBLANE
    tb = nchunk * C

    @pl.when(pl.program_id(2) == 0)
    def _():
        st_ref[...] = jnp.zeros_like(st_ref)
        ri = lax.broadcasted_iota(jnp.int32, (tb, tb), 0)
        ci = lax.broadcasted_iota(jnp.int32, (tb, tb), 1)
        incl_ref[...] = jnp.where(ci <= ri, jnp.where(ci >= (ri & ~(C - 1)), 1.0, 0.0), 0.0).astype(BF16)

    grp_ri = lax.broadcasted_iota(jnp.int32, (G, C), 0)
    grp_ci = lax.broadcasted_iota(jnp.int32, (G, C), 1)

    g_hi, g_lo = _hi_lo(ga_ref[...])
    a_hi, a_lo = _hi_lo(au_ref[...])
    pre = _dot(g_hi, a_hi) + _dot(g_hi, a_lo) + _dot(g_lo, a_hi) + ab_ref[...]
    log_f = -(jnp.maximum(-pre, 0.0) + jnp.log1p(jnp.exp(-jnp.abs(pre)))) / GLA_GATE_NORM
    f_hi = log_f.astype(BF16)
    f_mid, f_lo = _hi_lo(log_f - f_hi.astype(F32))
    incl = incl_ref[...]
    b_all = (_dot(incl, f_hi) + _dot(incl, f_mid) + _dot(incl, f_lo)) * LOG2E

    st = st_ref[...]
    for c in range(nchunk):
        r0 = c * C
        q = q_ref[r0:r0 + C, :].astype(F32) * (GLA_DK ** -0.5)
        k = k_ref[r0:r0 + C, :].astype(F32)
        v = v_ref[r0:r0 + C, :]
        b = b_all[r0:r0 + C, :]
        b_last = b[C - 1:C, :]

        o_inter = _dot_nt((q * jnp.exp2(b)).astype(BF16), st.astype(BF16))

        att_rows = []
        for s in range(C // SUB):
            i0 = s * SUB
            if s > 0:
                b0 = b[i0:i0 + 1, :]
                q_dec = (q[i0:i0 + SUB, :] * jnp.exp2(b[i0:i0 + SUB, :] - b0)).astype(BF16)
                k_dec = (k * jnp.exp2(jnp.minimum(b0 - b, 0.0))).astype(BF16)
                att_far = _dot_nt(q_dec, k_dec)
            for g0 in range(i0, i0 + SUB, G):
                bg = b[g0:g0 + G, :]
                qg = q[g0:g0 + G, :]
                att = jnp.where(grp_ci < i0, att_far[g0 - i0:g0 - i0 + G, :], 0.0) if s > 0 else jnp.zeros((G, C), F32)
                for jj in range(i0, g0 + G):
                    a = jnp.sum(qg * k[jj:jj + 1, :] * jnp.exp2(bg - b[jj:jj + 1, :]), axis=-1, keepdims=True)
                    att = jnp.where(grp_ci == jj, a, att)
                att_rows.append(jnp.where(grp_ci <= grp_ri + g0, att, 0.0))
        att = jnp.concatenate(att_rows, axis=0)
        o = o_inter + _dot(att.astype(BF16), v)

        k_dec = (k * jnp.exp2(b_last - b)).astype(BF16)
        st = st * jnp.exp2(b_last) + _dot_tn(v, k_dec)

        y = _rms(o, ng_ref[...]) * _silu(gg_ref[r0:r0 + C, :].astype(F32))
        o_ref[r0:r0 + C, :] = y.astype(o_ref.dtype)
    st_ref[...] = st


def gla(z, ga, au, ab, ng, batch, seq, tb=512):
    T = z.shape[0]
    tb = min(tb, seq)
    nb = seq // tb
    q0 = 4 * SB_W // GLA_DK
    k0 = q0 + GLA_HEADS
    v0 = (4 * SB_W + 2 * GLA_KW) // GLA_DV
    g0 = v0 + GLA_HEADS
    return pl.pallas_call(
        functools.partial(_gla_kernel, nchunk=tb // GLA_CHUNK),
        grid=(batch, GLA_HEADS, nb),
        in_specs=[
            pl.BlockSpec((tb, GLA_DK), lambda b, h, i: (b * nb + i, q0 + h)),
            pl.BlockSpec((tb, GLA_DK), lambda b, h, i: (b * nb + i, k0 + h)),
            pl.BlockSpec((tb, GLA_DV), lambda b, h, i: (b * nb + i, v0 + h)),
            pl.BlockSpec((tb, GLA_DV), lambda b, h, i: (b * nb + i, g0 + h)),
            pl.BlockSpec((tb, LANE), lambda b, h, i: (b * nb + i, 0)),
            pl.BlockSpec((LANE, GLA_DK), lambda b, h, i: (0, h)),
            pl.BlockSpec((1, GLA_DK), lambda b, h, i: (0, h)),
            pl.BlockSpec((1, GLA_DV), lambda b, h, i: (0, 0)),
        ],
        out_specs=pl.BlockSpec((tb, GLA_DV), lambda b, h, i: (b * nb + i, h)),
        out_shape=jax.ShapeDtypeStruct((T, GLA_VW), BF16),
        scratch_shapes=[pltpu.VMEM((GLA_DV, GLA_DK), F32), pltpu.VMEM((tb, tb), BF16)],
        compiler_params=_cparams(("parallel", "parallel", "arbitrary")),
        name="gla",
    )(z, z, z, z, ga, au, ab, ng)


def _out_proj_kernel(y1_ref, y2_ref, w1_ref, w2_ref, x_ref, gf_ref, o_ref, *, final):
    acc = x_ref[...] + _dot(y1_ref[...], w1_ref[...]) + _dot(y2_ref[...], w2_ref[...])
    if final:
        acc = _rms(acc, gf_ref[...])
    o_ref[...] = acc


def out_proj(y1, c1, y2, c2, w, x, gf, final, tm=512):
    T, D = x.shape
    kh = w.shape[0] // 2
    tm = min(tm, T)
    return pl.pallas_call(
        functools.partial(_out_proj_kernel, final=final),
        grid=(T // tm,),
        in_specs=[
            pl.BlockSpec((tm, kh), lambda i: (i, c1)),
            pl.BlockSpec((tm, kh), lambda i: (i, c2)),
            pl.BlockSpec((kh, D), lambda i: (0, 0)),
            pl.BlockSpec((kh, D), lambda i: (1, 0)),
            pl.BlockSpec((tm, D), lambda i: (i, 0)),
            pl.BlockSpec((1, D), lambda i: (0, 0)),
        ],
        out_specs=pl.BlockSpec((tm, D), lambda i: (i, 0)),
        out_shape=jax.ShapeDtypeStruct((T, D), F32),
        compiler_params=_cparams(("parallel",)),
        name="out_proj",
    )(y1, y2, w, w, x, gf.reshape(1, D))


def _mla_up_kernel(lat_ref, pos_ref, inv_ref, qg_ref, kvg_ref, wq_ref, wkv_ref, q_ref, k_ref, v_ref, *, scale):
    hq = _rms(lat_ref[:, 0:MLA_QR], qg_ref[...]).astype(BF16)
    hkv = _rms(lat_ref[:, MLA_QR:MLA_QR + MLA_KVR], kvg_ref[...]).astype(BF16)
    ang = pos_ref[...].astype(F32) * inv_ref[...]
    lane = lax.broadcasted_iota(jnp.int32, ang.shape, 1)
    half = MLA_ROPE // 2
    cos2 = jnp.where(lane < MLA_ROPE, jnp.cos(ang), 0.0)
    sin2 = jnp.where(lane < half, -jnp.sin(ang), jnp.where(lane < MLA_ROPE, jnp.sin(ang), 0.0))

    def rope(r):
        return r * cos2 + pltpu.roll(r, MLA_ROPE, 1) * sin2

    kr = rope(lat_ref[:, MLA_QR + MLA_KVR:MLA_QR + MLA_KVR + LANE]).astype(BF16)
    kn = _dot(hkv, wkv_ref[:, 0:MLA_HEADS * MLA_NOPE]).astype(BF16)
    v_ref[...] = _dot(hkv, wkv_ref[:, MLA_HEADS * MLA_NOPE:]).astype(BF16)
    for h in range(MLA_HEADS):
        c0 = h * MLA_QK_PAD
        y = _dot(hq, wq_ref[:, c0:c0 + MLA_QK_PAD])
        q_ref[:, c0:c0 + MLA_NOPE] = (y[:, 0:MLA_NOPE] * scale).astype(BF16)
        q_ref[:, c0 + MLA_NOPE:c0 + MLA_QK_PAD] = (rope(y[:, MLA_NOPE:]) * scale).astype(BF16)
        k_ref[:, c0:c0 + MLA_NOPE] = kn[:, h * MLA_NOPE:(h + 1) * MLA_NOPE]
        k_ref[:, c0 + MLA_NOPE:c0 + MLA_QK_PAD] = kr


def mla_up(lat, pos, inv, qg, kvg, wq, wkv, tm=512):
    T = lat.shape[0]
    tm = min(tm, T)
    qk_w = MLA_HEADS * MLA_QK_PAD
    v_w = MLA_HEADS * MLA_V
    return pl.pallas_call(
        functools.partial(_mla_up_kernel, scale=(MLA_NOPE + MLA_ROPE) ** -0.5 * LOG2E),
        grid=(T // tm,),
        in_specs=[
            pl.BlockSpec((tm, lat.shape[1]), lambda i: (i, 0)),
            pl.BlockSpec((tm, 1), lambda i: (i, 0)),
            pl.BlockSpec((1, LANE), lambda i: (0, 0)),
            pl.BlockSpec((1, MLA_QR), lambda i: (0, 0)),
            pl.BlockSpec((1, MLA_KVR), lambda i: (0, 0)),
            pl.BlockSpec(wq.shape, lambda i: (0, 0)),
            pl.BlockSpec(wkv.shape, lambda i: (0, 0)),
        ],
        out_specs=[
            pl.BlockSpec((tm, qk_w), lambda i: (i, 0)),
            pl.BlockSpec((tm, qk_w), lambda i: (i, 0)),
            pl.BlockSpec((tm, v_w), lambda i: (i, 0)),
        ],
        out_shape=[
            jax.ShapeDtypeStruct((T, qk_w), BF16),
            jax.ShapeDtypeStruct((T, qk_w), BF16),
            jax.ShapeDtypeStruct((T, v_w), BF16),
        ],
        compiler_params=_cparams(("parallel",)),
        name="mla_up",
    )(lat, pos, inv, qg.reshape(1, -1), kvg.reshape(1, -1), wq, wkv)


def _mla_attn_kernel(q_ref, k_ref, v_ref, g_ref, o_ref, s0_ref, s1_ref, p0_ref, p1_ref, m_ref, l_ref, acc_ref,
                     *, tq, tk, rt, st):
    qi = pl.program_id(2)
    nd = tq // tk
    nf = qi * nd
    base = qi * tq
    reps = tk // LANE
    s_refs = (s0_ref, s1_ref)
    p_refs = (p0_ref, p1_ref)
    m_ref[...] = jnp.full_like(m_ref, -jnp.inf)
    l_ref[...] = jnp.zeros_like(l_ref)
    acc_ref[...] = jnp.zeros_like(acc_ref)

    def scores(start, r0, slot, r1=tq):
        s_refs[slot][r0:r1, :] = _dot_nt(q_ref[r0:r1, :], k_ref[pl.ds(start, tk), :])

    def update(start, r0, slot, masked):
        vb = v_ref[pl.ds(start, tk), :]
        for g0 in range(r0, tq, rt):
            alphas = []
            for t0 in range(g0, g0 + rt, st):
                rows = slice(t0, t0 + st)
                s = s_refs[slot][rows, :]
                if masked and t0 - r0 < tk:
                    col = lax.broadcasted_iota(jnp.int32, s.shape, 1)
                    row = lax.broadcasted_iota(jnp.int32, s.shape, 0)
                    s = jnp.where(col <= row + (t0 - r0), s, -jnp.inf)
                m_old = m_ref[rows, :]
                m_new = jnp.maximum(m_old, jnp.max(s, axis=-1, keepdims=True))
                alpha = jnp.exp2(m_old - m_new)
                p = jnp.exp2(s - jnp.tile(m_new, (1, reps)))
                l_ref[rows, :] = alpha * l_ref[rows, :] + jnp.sum(p, axis=-1, keepdims=True)
                p_refs[slot][rows, :] = p.astype(BF16)
                m_ref[rows, :] = m_new
                alphas.append(alpha)
            grp = slice(g0, g0 + rt)
            acc_ref[grp, :] = jnp.concatenate(alphas, axis=0) * acc_ref[grp, :] + _dot(p_refs[slot][grp, :], vb)

    scores(0, 0, 0, rt)

    def body(i, carry):
        for u in range(nd):
            j = i * nd + u
            if u == 0:
                scores(pl.multiple_of(j * tk, tk), rt, 0)
            scores(pl.multiple_of((j + 1) * tk, tk), 0, (u + 1) % 2, tq if u + 1 < nd else rt)
            update(pl.multiple_of(j * tk, tk), 0, u % 2, False)
        return carry

    lax.fori_loop(0, qi, body, 0)
    scores(pl.multiple_of(base, tk), rt, 0)
    for m in range(nd):
        if m + 1 < nd:
            scores(pl.multiple_of(base + (m + 1) * tk, tk), (m + 1) * tk, (m + 1) % 2)
        update(pl.multiple_of(base + m * tk, tk), m * tk, m % 2, True)
    o_ref[...] = (acc_ref[...] / l_ref[...] * _silu(g_ref[...].astype(F32))).astype(o_ref.dtype)


def mla_attention(qf, kf, v, gate, batch, seq, tq=2048, tk=512, rt=256):
    T = qf.shape[0]
    tq = min(tq, seq)
    tk = min(tk, tq)
    rt = min(rt, tk)
    nq = seq // tq
    assert (tq // tk) % 2 == 0 or nq == 1, "score-buffer parity is static only for an even block count per tile"
    return pl.pallas_call(
        functools.partial(_mla_attn_kernel, tq=tq, tk=tk, rt=rt, st=min(MLA_SOFTMAX_ROWS, rt)),
        grid=(batch, MLA_HEADS, nq),
        in_specs=[
            pl.BlockSpec((tq, MLA_QK_PAD), lambda b, h, i: (b * nq + i, h)),
            pl.BlockSpec((seq, MLA_QK_PAD), lambda b, h, i: (b, h)),
            pl.BlockSpec((seq, MLA_V), lambda b, h, i: (b, h)),
            pl.BlockSpec((tq, MLA_V), lambda b, h, i: (b * nq + i, h)),
        ],
        out_specs=pl.BlockSpec((tq, MLA_V), lambda b, h, i: (b * nq + i, h)),
        out_shape=jax.ShapeDtypeStruct((T, MLA_HEADS * MLA_V), BF16),
        scratch_shapes=[pltpu.VMEM((tq, tk), F32), pltpu.VMEM((tq, tk), F32),
                        pltpu.VMEM((tq, tk), BF16), pltpu.VMEM((tq, tk), BF16), pltpu.VMEM((tq, LANE), F32),
                        pltpu.VMEM((tq, LANE), F32), pltpu.VMEM((tq, MLA_V), F32)],
        compiler_params=_cparams(("parallel", "parallel", "arbitrary")),
        name="mla_attention",
    )(qf, kf, v, gate)


def _swap_halves(w):
    half = w.shape[-1] // 2
    return jnp.concatenate([w[..., half:], w[..., :half]], axis=-1)


def _prep_even(w_in, alpha_up, alpha_bias):
    col_scale = jnp.concatenate([jnp.full((SB_W,), -(SB_DIM ** -0.5) * LOG2E, F32), jnp.ones((EVEN_MAIN - SB_W,), F32)])
    w_main = (w_in[:, :EVEN_MAIN] * col_scale).astype(BF16)
    w_ga = jnp.pad(w_in[:, EVEN_MAIN:], ((0, 0), (0, LANE - GLA_RANK))).astype(BF16)
    au = jnp.pad(alpha_up, ((0, LANE - GLA_RANK), (0, 0)))
    return w_main, w_ga, au, alpha_bias.reshape(1, -1)


def _prep_odd(w_in, w_q_up, w_kv_up):
    o1 = MLA_QR + MLA_KVR
    w_kr = w_in[:, o1:o1 + MLA_ROPE]
    w_lat = jnp.concatenate([w_in[:, :o1], w_kr, _swap_halves(w_kr)], axis=1).astype(BF16)
    w_gate = w_in[:, o1 + MLA_ROPE:].astype(BF16)
    wq = w_q_up.reshape(MLA_QR, MLA_HEADS, MLA_NOPE + MLA_ROPE)
    wq_r = wq[..., MLA_NOPE:]
    wq = jnp.concatenate([wq[..., :MLA_NOPE], wq_r, _swap_halves(wq_r)], axis=-1)
    wq = wq.reshape(MLA_QR, MLA_HEADS * MLA_QK_PAD).astype(BF16)
    wkv = w_kv_up.reshape(MLA_KVR, MLA_HEADS, MLA_NOPE + MLA_V)
    wkv = jnp.concatenate([wkv[..., :MLA_NOPE].reshape(MLA_KVR, -1), wkv[..., MLA_NOPE:].reshape(MLA_KVR, -1)], axis=1)
    return w_lat, w_gate, wq, wkv.astype(BF16)


def _even_layer(x, batch, seq, norm_g, prepped, gla_norm_g, w_out, final_g, final):
    w_main, w_ga, au, ab = prepped
    z, ga = norm_matmul(x, norm_g, w_main, BF16, w_ga, F32, tm=1024)
    y_a = sb_attention(z, batch, seq)
    y_b = gla(z, ga, au, ab, gla_norm_g.reshape(1, -1), batch, seq)
    return out_proj(y_a, 0, y_b, 0, w_out, x, final_g, final)


def _odd_layer(x, pos, inv, batch, seq, norm_g, prepped, q_norm_g, kv_norm_g, w_out, final_g, final):
    w_lat, w_gate, wq, wkv = prepped
    gate, lat = norm_matmul(x, norm_g, w_gate, BF16, w_lat, F32, tm=1024, tn_cap=512)
    qf, kf, v = mla_up(lat, pos, inv, q_norm_g, kv_norm_g, wq, wkv)
    y = mla_attention(qf, kf, v, gate, batch, seq)
    return out_proj(y, 0, y, 1, w_out, x, final_g, final)


def kernel(x, positions, ln_even, w_in_even, gla_alpha_up, gla_alpha_bias, gla_norm, w_out_even,
           ln_odd, w_in_odd, q_norm, w_q_up, kv_norm, w_kv_up, w_out_odd, final_norm):
    batch, seq, d = x.shape
    depth = ln_even.shape[0] + ln_odd.shape[0]
    h = x.reshape(batch * seq, d)
    pos = positions.reshape(batch * seq, 1)
    half = MLA_ROPE // 2
    inv = ROPE_THETA ** (-jnp.arange(half, dtype=F32) / half)
    inv = jnp.concatenate([inv, inv, jnp.zeros((LANE - MLA_ROPE,), F32)]).reshape(1, LANE)
    even_w = jax.vmap(_prep_even)(w_in_even, gla_alpha_up, gla_alpha_bias)
    odd_w = jax.vmap(_prep_odd)(w_in_odd, w_q_up, w_kv_up)
    wo_even = w_out_even.astype(BF16)
    wo_odd = w_out_odd.astype(BF16)
    for layer in range(depth):
        i = layer // 2
        final = layer == depth - 1
        if layer % 2 == 0:
            h = _even_layer(h, batch, seq, ln_even[i], tuple(w[i] for w in even_w), gla_norm[i], wo_even[i],
                            final_norm, final)
        else:
            h = _odd_layer(h, pos, inv, batch, seq, ln_odd[i], tuple(w[i] for w in odd_w), q_norm[i], kv_norm[i],
                           wo_odd[i], final_norm, final)
    return h.reshape(batch, seq, d)
```

```python
import functools

import jax
import jax.numpy as jnp
from jax import lax
from jax.experimental import pallas as pl
from jax.experimental.pallas import tpu as pltpu

F32 = jnp.float32
BF16 = jnp.bfloat16

SB_HEADS = 8
SB_DIM = 128
GLA_HEADS = 4
GLA_DK = 128
GLA_DV = 256
GLA_RANK = 16
GLA_GATE_NORM = 16.0
GLA_CHUNK = 64
GLA_SUB = 16
SB_SLOTS = 4
MLA_SOFTMAX_ROWS = 64
MLA_HEADS = 16
MLA_QR = 512
MLA_KVR = 512
MLA_NOPE = 128
MLA_ROPE = 64
MLA_V = 128
MLA_QK_PAD = 256
ROPE_THETA = 10000.0
EPS = 1e-6
LOG2E = 1.4426950408889634

LANE = 128
SUBLANE = 8
VMEM_LIMIT = 48 * 1024 * 1024

SB_W = SB_HEADS * SB_DIM
GLA_KW = GLA_HEADS * GLA_DK
GLA_VW = GLA_HEADS * GLA_DV
EVEN_MAIN = 4 * SB_W + 2 * GLA_KW + 2 * GLA_VW


def _cparams(sem):
    return pltpu.CompilerParams(dimension_semantics=sem, vmem_limit_bytes=VMEM_LIMIT)


def _rms(x, g):
    return x * lax.rsqrt(jnp.mean(x * x, axis=-1, keepdims=True) + EPS) * g


def _silu(g):
    return g * (1.0 / (1.0 + jnp.exp(-g)))


def _dot(a, b):
    return jnp.dot(a, b, preferred_element_type=F32)


def _hi_lo(x):
    hi = x.astype(BF16)
    return hi, (x - hi.astype(F32)).astype(BF16)


def _dot_nt(a, b):
    return lax.dot_general(a, b, (((1,), (1,)), ((), ())), preferred_element_type=F32)


def _dot_tn(a, b):
    return lax.dot_general(a, b, (((0,), (0,)), ((), ())), preferred_element_type=F32)


def _norm_matmul_kernel(x_ref, g_ref, w_ref, ws_ref, o_ref, os_ref, h_ref):
    @pl.when(pl.program_id(1) == 0)
    def _():
        h = _rms(x_ref[...], g_ref[...]).astype(BF16)
        h_ref[...] = h
        os_ref[...] = _dot(h, ws_ref[...]).astype(os_ref.dtype)

    o_ref[...] = _dot(h_ref[...], w_ref[...]).astype(o_ref.dtype)


def _pick_tile(n, cap):
    best = LANE
    for t in range(LANE, cap + 1, LANE):
        if n % t == 0:
            best = t
    return best


def norm_matmul(x, g, w, out_dtype, w_side, side_dtype, tm, tn_cap=1024):
    T, K = x.shape
    N = w.shape[1]
    ns = w_side.shape[1]
    tn = _pick_tile(N, tn_cap)
    tm = min(tm, T)
    return pl.pallas_call(
        _norm_matmul_kernel,
        grid=(T // tm, N // tn),
        in_specs=[
            pl.BlockSpec((tm, K), lambda i, j: (i, 0)),
            pl.BlockSpec((1, K), lambda i, j: (0, 0)),
            pl.BlockSpec((K, tn), lambda i, j: (0, j)),
            pl.BlockSpec((K, ns), lambda i, j: (0, 0), pipeline_mode=pl.Buffered(1)),
        ],
        out_specs=[
            pl.BlockSpec((tm, tn), lambda i, j: (i, j)),
            pl.BlockSpec((tm, ns), lambda i, j: (i, 0)),
        ],
        out_shape=[jax.ShapeDtypeStruct((T, N), out_dtype), jax.ShapeDtypeStruct((T, ns), side_dtype)],
        scratch_shapes=[pltpu.VMEM((tm, K), BF16)],
        compiler_params=_cparams(("parallel", "arbitrary")),
        name="norm_matmul",
    )(x, g.reshape(1, K), w, w_side)


def _sb_kernel(q_ref, k_ref, v_ref, g_ref, o_ref, *scratch, tq, tk, rt):
    qi = pl.program_id(2)
    nd = tq // tk
    ns = SB_SLOTS
    nf = qi * nd
    nzw_b = scratch[0:ns // 2]
    ls_b, tot_b = (scratch[ns // 2 + i * ns:ns // 2 + (i + 1) * ns] for i in range(2))
    acc_ref = scratch[ns // 2 + 2 * ns]
    krow = lax.broadcasted_iota(jnp.int32, (tk, tk), 0)
    kcol = lax.broadcasted_iota(jnp.int32, (tk, tk), 1)
    from_s = (krow >= kcol).astype(BF16)
    acc_ref[...] = jnp.zeros_like(acc_ref)
    @pl.when(qi == 0)
    def _():
        nzw_b[ns // 2 - 1][...] = jnp.full((tq, 2 * tk), jnp.inf, F32)
        ls_b[ns - 2][...] = jnp.zeros((tq, tk), BF16)
        tot_b[ns - 2][...] = jnp.zeros((tq, LANE), F32)

    def nz_view(slot, rows):
        return nzw_b[slot // 2].at[rows, (slot % 2) * tk:(slot % 2 + 1) * tk]

    def stage_a(start, r0, slot):
        nzw_b[slot // 2][r0:, :] = _dot_nt(q_ref[r0:, :], k_ref[pl.ds(start, 2 * tk), :])

    def stage_b(r0, slot, masked):
        for t0 in range(r0, tq, rt):
            rows = slice(t0, t0 + rt)
            nz = nz_view(slot, rows)[...]
            neg_abs = lax.bitcast_convert_type(lax.bitcast_convert_type(nz, jnp.uint32) | jnp.uint32(0x80000000), F32)
            log_stay = jnp.minimum(nz, 0.0) - jnp.log(1.0 + jnp.exp2(neg_abs)) * LOG2E
            if masked and t0 - r0 < tk:
                col = lax.broadcasted_iota(jnp.int32, nz.shape, 1)
                row = lax.broadcasted_iota(jnp.int32, nz.shape, 0)
                keep = col < row + (t0 - r0)
                log_stay = jnp.where(keep, log_stay, 0.0)
                nz_view(slot, rows)[...] = jnp.where(keep, nz, jnp.inf)
            ls_b[slot][rows, :] = log_stay.astype(BF16)
            tot_b[slot][rows, :] = jnp.broadcast_to(jnp.sum(log_stay, axis=-1, keepdims=True), (rt, LANE))

    def stage_c(start, r0, slot):
        vb = v_ref[pl.ds(start, tk), :]
        for t0 in range(r0, tq, rt):
            rows = slice(t0, t0 + rt)
            upto = _dot(ls_b[slot][rows, :], from_s)
            w = jnp.exp2(upto - nz_view(slot, rows)[...])
            acc_ref[rows, :] = acc_ref[rows, :] * jnp.exp2(tot_b[slot][rows, :]) + _dot(w.astype(BF16), vb)

    def trip(t, fill):
        for u in range(nd):
            g = t * nd + u
            if u % 2 == 0:
                stage_a(pl.multiple_of(g * tk, tk), 0, u % ns)
            if not (fill and u < 1):
                stage_b(0, (u - 1) % ns, False)
            if not (fill and u < 2):
                stage_c(pl.multiple_of((g - 2) * tk, tk), 0, (u - 2) % ns)

    @pl.when(qi > 0)
    def _():
        trip(0, True)

    def body(t, carry):
        trip(t, False)
        return carry

    lax.fori_loop(1, qi, body, 0)
    for e in range(nd + 2):
        if e < nd and e % 2 == 0:
            stage_a(pl.multiple_of((nf + e) * tk, tk), e * tk, e % ns)
        if 1 <= e <= nd:
            stage_b((e - 1) * tk, (e - 1) % ns, True)
        elif e == 0:
            stage_b(0, ns - 1, False)
        if e >= 2:
            stage_c(pl.multiple_of((nf + e - 2) * tk, tk), (e - 2) * tk, (e - 2) % ns)
        else:
            stage_c(pl.multiple_of(jnp.maximum(nf + e - 2, 0) * tk, tk), 0, (e - 2) % ns)
    o_ref[...] = (acc_ref[...] * _silu(g_ref[...].astype(F32))).astype(o_ref.dtype)


def sb_attention(z, batch, seq, tq=2048, tk=256, rt=256):
    T = z.shape[0]
    tq = min(tq, seq)
    tk = min(tk, tq)
    rt = min(rt, tk)
    nq = seq // tq
    ns = SB_SLOTS
    assert (tq // tk) % ns == 0, "slot of a block must not depend on the loop trip"
    hw = SB_W // SB_DIM
    scratch = ([pltpu.VMEM((tq, 2 * tk), F32)] * (ns // 2) + [pltpu.VMEM((tq, tk), BF16)] * ns
               + [pltpu.VMEM((tq, LANE), F32)] * ns + [pltpu.VMEM((tq, SB_DIM), F32)])
    return pl.pallas_call(
        functools.partial(_sb_kernel, tq=tq, tk=tk, rt=rt),
        grid=(batch, SB_HEADS, nq),
        in_specs=[
            pl.BlockSpec((tq, SB_DIM), lambda b, h, i: (b * nq + i, h)),
            pl.BlockSpec((seq, SB_DIM), lambda b, h, i: (b, hw + h)),
            pl.BlockSpec((seq, SB_DIM), lambda b, h, i: (b, 2 * hw + h)),
            pl.BlockSpec((tq, SB_DIM), lambda b, h, i: (b * nq + i, 3 * hw + h)),
        ],
        out_specs=pl.BlockSpec((tq, SB_DIM), lambda b, h, i: (b * nq + i, h)),
        out_shape=jax.ShapeDtypeStruct((T, SB_W), BF16),
        scratch_shapes=scratch,
        compiler_params=_cparams(("parallel", "parallel", "arbitrary")),
        name="sb_attention",
    )(z, z, z, z)


def _gla_kernel(q_ref, k_ref, v_ref, gg_ref, ga_ref, au_ref, ab_ref, ng_ref, o_ref, st_ref, incl_ref, *, nchunk):
    C, SUB, G = GLA_CHUNK, GLA_SUB, SUBLANE
    tb = nchunk * C

    @pl.when(pl.program_id(2) == 0)
    def _():
        st_ref[...] = jnp.zeros_like(st_ref)
        ri = lax.broadcasted_iota(jnp.int32, (tb, tb), 0)
        ci = lax.broadcasted_iota(jnp.int32, (tb, tb), 1)
        incl_ref[...] = jnp.where(ci <= ri, jnp.where(ci >= (ri & ~(C - 1)), 1.0, 0.0), 0.0).astype(BF16)

    grp_ri = lax.broadcasted_iota(jnp.int32, (G, C), 0)
    grp_ci = lax.broadcasted_iota(jnp.int32, (G, C), 1)

    g_hi, g_lo = _hi_lo(ga_ref[...])
    a_hi, a_lo = _hi_lo(au_ref[...])
    pre = _dot(g_hi, a_hi) + _dot(g_hi, a_lo) + _dot(g_lo, a_hi) + ab_ref[...]
    log_f = -(jnp.maximum(-pre, 0.0) + jnp.log1p(jnp.exp(-jnp.abs(pre)))) / GLA_GATE_NORM
    f_hi = log_f.astype(BF16)
    f_mid, f_lo = _hi_lo(log_f - f_hi.astype(F32))
    incl = incl_ref[...]
    b_all = (_dot(incl, f_hi) + _dot(incl, f_mid) + _dot(incl, f_lo)) * LOG2E

    st = st_ref[...]
    for c in range(nchunk):
        r0 = c * C
        q = q_ref[r0:r0 + C, :].astype(F32) * (GLA_DK ** -0.5)
        k = k_ref[r0:r0 + C, :].astype(F32)
        v = v_ref[r0:r0 + C, :]
        b = b_all[r0:r0 + C, :]
        b_last = b[C - 1:C, :]

        o_inter = _dot_nt((q * jnp.exp2(b)).astype(BF16), st.astype(BF16))

        att_rows = []
        for s in range(C // SUB):
            i0 = s * SUB
            if s > 0:
                b0 = b[i0:i0 + 1, :]
                q_dec = (q[i0:i0 + SUB, :] * jnp.exp2(b[i0:i0 + SUB, :] - b0)).astype(BF16)
                k_dec = (k * jnp.exp2(jnp.minimum(b0 - b, 0.0))).astype(BF16)
                att_far = _dot_nt(q_dec, k_dec)
            for g0 in range(i0, i0 + SUB, G):
                bg = b[g0:g0 + G, :]
                qg = q[g0:g0 + G, :]
                att = jnp.where(grp_ci < i0, att_far[g0 - i0:g0 - i0 + G, :], 0.0) if s > 0 else jnp.zeros((G, C), F32)
                for jj in range(i0, g0 + G):
                    a = jnp.sum(qg * k[jj:jj + 1, :] * jnp.exp2(bg - b[jj:jj + 1, :]), axis=-1, keepdims=True)
                    att = jnp.where(grp_ci == jj, a, att)
                att_rows.append(jnp.where(grp_ci <= grp_ri + g0, att, 0.0))
        att = jnp.concatenate(att_rows, axis=0)
        o = o_inter + _dot(att.astype(BF16), v)

        k_dec = (k * jnp.exp2(b_last - b)).astype(BF16)
        st = st * jnp.exp2(b_last) + _dot_tn(v, k_dec)

        y = _rms(o, ng_ref[...]) * _silu(gg_ref[r0:r0 + C, :].astype(F32))
        o_ref[r0:r0 + C, :] = y.astype(o_ref.dtype)
    st_ref[...] = st


def gla(z, ga, au, ab, ng, batch, seq, tb=512):
    T = z.shape[0]
    tb = min(tb, seq)
    nb = seq // tb
    q0 = 4 * SB_W // GLA_DK
    k0 = q0 + GLA_HEADS
    v0 = (4 * SB_W + 2 * GLA_KW) // GLA_DV
    g0 = v0 + GLA_HEADS
    return pl.pallas_call(
        functools.partial(_gla_kernel, nchunk=tb // GLA_CHUNK),
        grid=(batch, GLA_HEADS, nb),
        in_specs=[
            pl.BlockSpec((tb, GLA_DK), lambda b, h, i: (b * nb + i, q0 + h)),
            pl.BlockSpec((tb, GLA_DK), lambda b, h, i: (b * nb + i, k0 + h)),
            pl.BlockSpec((tb, GLA_DV), lambda b, h, i: (b * nb + i, v0 + h)),
            pl.BlockSpec((tb, GLA_DV), lambda b, h, i: (b * nb + i, g0 + h)),
            pl.BlockSpec((tb, LANE), lambda b, h, i: (b * nb + i, 0)),
            pl.BlockSpec((LANE, GLA_DK), lambda b, h, i: (0, h)),
            pl.BlockSpec((1, GLA_DK), lambda b, h, i: (0, h)),
            pl.BlockSpec((1, GLA_DV), lambda b, h, i: (0, 0)),
        ],
        out_specs=pl.BlockSpec((tb, GLA_DV), lambda b, h, i: (b * nb + i, h)),
        out_shape=jax.ShapeDtypeStruct((T, GLA_VW), BF16),
        scratch_shapes=[pltpu.VMEM((GLA_DV, GLA_DK), F32), pltpu.VMEM((tb, tb), BF16)],
        compiler_params=_cparams(("parallel", "parallel", "arbitrary")),
        name="gla",
    )(z, z, z, z, ga, au, ab, ng)


def _out_proj_kernel(y1_ref, y2_ref, w1_ref, w2_ref, x_ref, gf_ref, o_ref, *, final):
    acc = x_ref[...] + _dot(y1_ref[...], w1_ref[...]) + _dot(y2_ref[...], w2_ref[...])
    if final:
        acc = _rms(acc, gf_ref[...])
    o_ref[...] = acc


def out_proj(y1, c1, y2, c2, w, x, gf, final, tm=512):
    T, D = x.shape
    kh = w.shape[0] // 2
    tm = min(tm, T)
    return pl.pallas_call(
        functools.partial(_out_proj_kernel, final=final),
        grid=(T // tm,),
        in_specs=[
            pl.BlockSpec((tm, kh), lambda i: (i, c1)),
            pl.BlockSpec((tm, kh), lambda i: (i, c2)),
            pl.BlockSpec((kh, D), lambda i: (0, 0)),
            pl.BlockSpec((kh, D), lambda i: (1, 0)),
            pl.BlockSpec((tm, D), lambda i: (i, 0)),
            pl.BlockSpec((1, D), lambda i: (0, 0)),
        ],
        out_specs=pl.BlockSpec((tm, D), lambda i: (i, 0)),
        out_shape=jax.ShapeDtypeStruct((T, D), F32),
        compiler_params=_cparams(("parallel",)),
        name="out_proj",
    )(y1, y2, w, w, x, gf.reshape(1, D))


def _mla_up_kernel(lat_ref, pos_ref, inv_ref, qg_ref, kvg_ref, wq_ref, wkv_ref, q_ref, k_ref, v_ref, *, scale):
    hq = _rms(lat_ref[:, 0:MLA_QR], qg_ref[...]).astype(BF16)
    hkv = _rms(lat_ref[:, MLA_QR:MLA_QR + MLA_KVR], kvg_ref[...]).astype(BF16)
    ang = pos_ref[...].astype(F32) * inv_ref[...]
    lane = lax.broadcasted_iota(jnp.int32, ang.shape, 1)
    half = MLA_ROPE // 2
    cos2 = jnp.where(lane < MLA_ROPE, jnp.cos(ang), 0.0)
    sin2 = jnp.where(lane < half, -jnp.sin(ang), jnp.where(lane < MLA_ROPE, jnp.sin(ang), 0.0))

    def rope(r):
        return r * cos2 + pltpu.roll(r, MLA_ROPE, 1) * sin2

    kr = rope(lat_ref[:, MLA_QR + MLA_KVR:MLA_QR + MLA_KVR + LANE]).astype(BF16)
    kn = _dot(hkv, wkv_ref[:, 0:MLA_HEADS * MLA_NOPE]).astype(BF16)
    v_ref[...] = _dot(hkv, wkv_ref[:, MLA_HEADS * MLA_NOPE:]).astype(BF16)
    for h in range(MLA_HEADS):
        c0 = h * MLA_QK_PAD
        y = _dot(hq, wq_ref[:, c0:c0 + MLA_QK_PAD])
        q_ref[:, c0:c0 + MLA_NOPE] = (y[:, 0:MLA_NOPE] * scale).astype(BF16)
        q_ref[:, c0 + MLA_NOPE:c0 + MLA_QK_PAD] = (rope(y[:, MLA_NOPE:]) * scale).astype(BF16)
        k_ref[:, c0:c0 + MLA_NOPE] = kn[:, h * MLA_NOPE:(h + 1) * MLA_NOPE]
        k_ref[:, c0 + MLA_NOPE:c0 + MLA_QK_PAD] = kr


def mla_up(lat, pos, inv, qg, kvg, wq, wkv, tm=512):
    T = lat.shape[0]
    tm = min(tm, T)
    qk_w = MLA_HEADS * MLA_QK_PAD
    v_w = MLA_HEADS * MLA_V
    return pl.pallas_call(
        functools.partial(_mla_up_kernel, scale=(MLA_NOPE + MLA_ROPE) ** -0.5 * LOG2E),
        grid=(T // tm,),
        in_specs=[
            pl.BlockSpec((tm, lat.shape[1]), lambda i: (i, 0)),
            pl.BlockSpec((tm, 1), lambda i: (i, 0)),
            pl.BlockSpec((1, LANE), lambda i: (0, 0)),
            pl.BlockSpec((1, MLA_QR), lambda i: (0, 0)),
            pl.BlockSpec((1, MLA_KVR), lambda i: (0, 0)),
            pl.BlockSpec(wq.shape, lambda i: (0, 0)),
            pl.BlockSpec(wkv.shape, lambda i: (0, 0)),
        ],
        out_specs=[
            pl.BlockSpec((tm, qk_w), lambda i: (i, 0)),
            pl.BlockSpec((tm, qk_w), lambda i: (i, 0)),
            pl.BlockSpec((tm, v_w), lambda i: (i, 0)),
        ],
        out_shape=[
            jax.ShapeDtypeStruct((T, qk_w), BF16),
            jax.ShapeDtypeStruct((T, qk_w), BF16),
            jax.ShapeDtypeStruct((T, v_w), BF16),
        ],
        compiler_params=_cparams(("parallel",)),
        name="mla_up",
    )(lat, pos, inv, qg.reshape(1, -1), kvg.reshape(1, -1), wq, wkv)


def _mla_attn_kernel(q_ref, k_ref, v_ref, g_ref, o_ref, s0_ref, s1_ref, p0_ref, p1_ref, m_ref, l_ref, acc_ref,
                     *, tq, tk, rt, st):
    qi = pl.program_id(2)
    nd = tq // tk
    nf = qi * nd
    base = qi * tq
    reps = tk // LANE
    s_refs = (s0_ref, s1_ref)
    p_refs = (p0_ref, p1_ref)
    m_ref[...] = jnp.full_like(m_ref, -jnp.inf)
    l_ref[...] = jnp.zeros_like(l_ref)
    acc_ref[...] = jnp.zeros_like(acc_ref)

    def scores(start, r0, slot, r1=tq):
        s_refs[slot][r0:r1, :] = _dot_nt(q_ref[r0:r1, :], k_ref[pl.ds(start, tk), :])

    def update(start, r0, slot, masked):
        vb = v_ref[pl.ds(start, tk), :]
        for g0 in range(r0, tq, rt):
            alphas = []
            for t0 in range(g0, g0 + rt, st):
                rows = slice(t0, t0 + st)
                s = s_refs[slot][rows, :]
                if masked and t0 - r0 < tk:
                    col = lax.broadcasted_iota(jnp.int32, s.shape, 1)
                    row = lax.broadcasted_iota(jnp.int32, s.shape, 0)
                    s = jnp.where(col <= row + (t0 - r0), s, -jnp.inf)
                m_old = m_ref[rows, :]
                m_new = jnp.maximum(m_old, jnp.max(s, axis=-1, keepdims=True))
                alpha = jnp.exp2(m_old - m_new)
                p = jnp.exp2(s - jnp.tile(m_new, (1, reps)))
                l_ref[rows, :] = alpha * l_ref[rows, :] + jnp.sum(p, axis=-1, keepdims=True)
                p_refs[slot][rows, :] = p.astype(BF16)
                m_ref[rows, :] = m_new
                alphas.append(alpha)
            grp = slice(g0, g0 + rt)
            acc_ref[grp, :] = jnp.concatenate(alphas, axis=0) * acc_ref[grp, :] + _dot(p_refs[slot][grp, :], vb)

    scores(0, 0, 0, rt)

    def body(i, carry):
        for u in range(nd):
            j = i * nd + u
            if u == 0:
                scores(pl.multiple_of(j * tk, tk), rt, 0)
            scores(pl.multiple_of((j + 1) * tk, tk), 0, (u + 1) % 2, tq if u + 1 < nd else rt)
            update(pl.multiple_of(j * tk, tk), 0, u % 2, False)
        return carry

    lax.fori_loop(0, qi, body, 0)
    scores(pl.multiple_of(base, tk), rt, 0)
    for m in range(nd):
        if m + 1 < nd:
            scores(pl.multiple_of(base + (m + 1) * tk, tk), (m + 1) * tk, (m + 1) % 2)
        update(pl.multiple_of(base + m * tk, tk), m * tk, m % 2, True)
    o_ref[...] = (acc_ref[...] / l_ref[...] * _silu(g_ref[...].astype(F32))).astype(o_ref.dtype)


def mla_attention(qf, kf, v, gate, batch, seq, tq=2048, tk=512, rt=256):
    T = qf.shape[0]
    tq = min(tq, seq)
    tk = min(tk, tq)
    rt = min(rt, tk)
    nq = seq // tq
    assert (tq // tk) % 2 == 0 or nq == 1, "score-buffer parity is static only for an even block count per tile"
    return pl.pallas_call(
        functools.partial(_mla_attn_kernel, tq=tq, tk=tk, rt=rt, st=min(MLA_SOFTMAX_ROWS, rt)),
        grid=(batch, MLA_HEADS, nq),
        in_specs=[
            pl.BlockSpec((tq, MLA_QK_PAD), lambda b, h, i: (b * nq + i, h)),
            pl.BlockSpec((seq, MLA_QK_PAD), lambda b, h, i: (b, h)),
            pl.BlockSpec((seq, MLA_V), lambda b, h, i: (b, h)),
            pl.BlockSpec((tq, MLA_V), lambda b, h, i: (b * nq + i, h)),
        ],
        out_specs=pl.BlockSpec((tq, MLA_V), lambda b, h, i: (b * nq + i, h)),
        out_shape=jax.ShapeDtypeStruct((T, MLA_HEADS * MLA_V), BF16),
        scratch_shapes=[pltpu.VMEM((tq, tk), F32), pltpu.VMEM((tq, tk), F32),
                        pltpu.VMEM((tq, tk), BF16), pltpu.VMEM((tq, tk), BF16), pltpu.VMEM((tq, LANE), F32),
                        pltpu.VMEM((tq, LANE), F32), pltpu.VMEM((tq, MLA_V), F32)],
        compiler_params=_cparams(("parallel", "parallel", "arbitrary")),
        name="mla_attention",
    )(qf, kf, v, gate)


def _swap_halves(w):
    half = w.shape[-1] // 2
    return jnp.concatenate([w[..., half:], w[..., :half]], axis=-1)


def _prep_even(w_in, alpha_up, alpha_bias):
    col_scale = jnp.concatenate([jnp.full((SB_W,), -(SB_DIM ** -0.5) * LOG2E, F32), jnp.ones((EVEN_MAIN - SB_W,), F32)])
    w_main = (w_in[:, :EVEN_MAIN] * col_scale).astype(BF16)
    w_ga = jnp.pad(w_in[:, EVEN_MAIN:], ((0, 0), (0, LANE - GLA_RANK))).astype(BF16)
    au = jnp.pad(alpha_up, ((0, LANE - GLA_RANK), (0, 0)))
    return w_main, w_ga, au, alpha_bias.reshape(1, -1)


def _prep_odd(w_in, w_q_up, w_kv_up):
    o1 = MLA_QR + MLA_KVR
    w_kr = w_in[:, o1:o1 + MLA_ROPE]
    w_lat = jnp.concatenate([w_in[:, :o1], w_kr, _swap_halves(w_kr)], axis=1).astype(BF16)
    w_gate = w_in[:, o1 + MLA_ROPE:].astype(BF16)
    wq = w_q_up.reshape(MLA_QR, MLA_HEADS, MLA_NOPE + MLA_ROPE)
    wq_r = wq[..., MLA_NOPE:]
    wq = jnp.concatenate([wq[..., :MLA_NOPE], wq_r, _swap_halves(wq_r)], axis=-1)
    wq = wq.reshape(MLA_QR, MLA_HEADS * MLA_QK_PAD).astype(BF16)
    wkv = w_kv_up.reshape(MLA_KVR, MLA_HEADS, MLA_NOPE + MLA_V)
    wkv = jnp.concatenate([wkv[..., :MLA_NOPE].reshape(MLA_KVR, -1), wkv[..., MLA_NOPE:].reshape(MLA_KVR, -1)], axis=1)
    return w_lat, w_gate, wq, wkv.astype(BF16)


def _even_layer(x, batch, seq, norm_g, prepped, gla_norm_g, w_out, final_g, final):
    w_main, w_ga, au, ab = prepped
    z, ga = norm_matmul(x, norm_g, w_main, BF16, w_ga, F32, tm=1024, tn_cap=1792)
    y_a = sb_attention(z, batch, seq)
    y_b = gla(z, ga, au, ab, gla_norm_g.reshape(1, -1), batch, seq)
    return out_proj(y_a, 0, y_b, 0, w_out, x, final_g, final)


def _odd_layer(x, pos, inv, batch, seq, norm_g, prepped, q_norm_g, kv_norm_g, w_out, final_g, final):
    w_lat, w_gate, wq, wkv = prepped
    gate, lat = norm_matmul(x, norm_g, w_gate, BF16, w_lat, F32, tm=1024, tn_cap=512)
    qf, kf, v = mla_up(lat, pos, inv, q_norm_g, kv_norm_g, wq, wkv)
    y = mla_attention(qf, kf, v, gate, batch, seq)
    return out_proj(y, 0, y, 1, w_out, x, final_g, final)


def kernel(x, positions, ln_even, w_in_even, gla_alpha_up, gla_alpha_bias, gla_norm, w_out_even,
           ln_odd, w_in_odd, q_norm, w_q_up, kv_norm, w_kv_up, w_out_odd, final_norm):
    batch, seq, d = x.shape
    depth = ln_even.shape[0] + ln_odd.shape[0]
    h = x.reshape(batch * seq, d)
    pos = positions.reshape(batch * seq, 1)
    half = MLA_ROPE // 2
    inv = ROPE_THETA ** (-jnp.arange(half, dtype=F32) / half)
    inv = jnp.concatenate([inv, inv, jnp.zeros((LANE - MLA_ROPE,), F32)]).reshape(1, LANE)
    even_w = jax.vmap(_prep_even)(w_in_even, gla_alpha_up, gla_alpha_bias)
    odd_w = jax.vmap(_prep_odd)(w_in_odd, w_q_up, w_kv_up)
    wo_even = w_out_even.astype(BF16)
    wo_odd = w_out_odd.astype(BF16)
    for layer in range(depth):
        i = layer // 2
        final = layer == depth - 1
        if layer % 2 == 0:
            h = _even_layer(h, batch, seq, ln_even[i], tuple(w[i] for w in even_w), gla_norm[i], wo_even[i],
                            final_norm, final)
        else:
            h = _odd_layer(h, pos, inv, batch, seq, ln_odd[i], tuple(w[i] for w in odd_w), q_norm[i], kv_norm[i],
                           wo_odd[i], final_norm, final)
    return h.reshape(batch, seq, d)
```

```python
import functools

import jax
import jax.numpy as jnp
from jax import lax
from jax.experimental import pallas as pl
from jax.experimental.pallas import tpu as pltpu

F32 = jnp.float32
BF16 = jnp.bfloat16

SB_HEADS = 8
SB_DIM = 128
GLA_HEADS = 4
GLA_DK = 128
GLA_DV = 256
GLA_RANK = 16
GLA_GATE_NORM = 16.0
GLA_CHUNK = 64
GLA_SUB = 16
SB_SLOTS = 4
MLA_SOFTMAX_ROWS = 64
MLA_HEADS = 16
MLA_QR = 512
MLA_KVR = 512
MLA_NOPE = 128
MLA_ROPE = 64
MLA_V = 128
MLA_QK_PAD = 256
ROPE_THETA = 10000.0
EPS = 1e-6
LOG2E = 1.4426950408889634

LANE = 128
SUBLANE = 8
VMEM_LIMIT = 48 * 1024 * 1024

SB_W = SB_HEADS * SB_DIM
GLA_KW = GLA_HEADS * GLA_DK
GLA_VW = GLA_HEADS * GLA_DV
EVEN_MAIN = 4 * SB_W + 2 * GLA_KW + 2 * GLA_VW


def _cparams(sem):
    return pltpu.CompilerParams(dimension_semantics=sem, vmem_limit_bytes=VMEM_LIMIT)


def _rms(x, g):
    return x * lax.rsqrt(jnp.mean(x * x, axis=-1, keepdims=True) + EPS) * g


def _silu(g):
    return g * (1.0 / (1.0 + jnp.exp(-g)))


def _dot(a, b):
    return jnp.dot(a, b, preferred_element_type=F32)


def _hi_lo(x):
    hi = x.astype(BF16)
    return hi, (x - hi.astype(F32)).astype(BF16)


def _dot_nt(a, b):
    return lax.dot_general(a, b, (((1,), (1,)), ((), ())), preferred_element_type=F32)


def _dot_tn(a, b):
    return lax.dot_general(a, b, (((0,), (0,)), ((), ())), preferred_element_type=F32)


def _norm_matmul_kernel(x_ref, g_ref, w_ref, ws_ref, o_ref, os_ref, h_ref):
    @pl.when(pl.program_id(1) == 0)
    def _():
        h = _rms(x_ref[...], g_ref[...]).astype(BF16)
        h_ref[...] = h
        os_ref[...] = _dot(h, ws_ref[...]).astype(os_ref.dtype)

    o_ref[...] = _dot(h_ref[...], w_ref[...]).astype(o_ref.dtype)


def _pick_tile(n, cap):
    best = LANE
    for t in range(LANE, cap + 1, LANE):
        if n % t == 0:
            best = t
    return best


def norm_matmul(x, g, w, out_dtype, w_side, side_dtype, tm, tn_cap=1024, n_cols=None):
    T, K = x.shape
    N = w.shape[1] if n_cols is None else n_cols
    ns = w_side.shape[1]
    tn = _pick_tile(N, tn_cap)
    tm = min(tm, T)
    return pl.pallas_call(
        _norm_matmul_kernel,
        grid=(T // tm, N // tn),
        in_specs=[
            pl.BlockSpec((tm, K), lambda i, j: (i, 0)),
            pl.BlockSpec((1, K), lambda i, j: (0, 0)),
            pl.BlockSpec((K, tn), lambda i, j: (0, j)),
            pl.BlockSpec((K, ns), lambda i, j: (0, 0), pipeline_mode=pl.Buffered(1)),
        ],
        out_specs=[
            pl.BlockSpec((tm, tn), lambda i, j: (i, j)),
            pl.BlockSpec((tm, ns), lambda i, j: (i, 0)),
        ],
        out_shape=[jax.ShapeDtypeStruct((T, N), out_dtype), jax.ShapeDtypeStruct((T, ns), side_dtype)],
        scratch_shapes=[pltpu.VMEM((tm, K), BF16)],
        compiler_params=_cparams(("parallel", "arbitrary")),
        name="norm_matmul",
    )(x, g.reshape(1, K), w, w_side)


def _sb_kernel(q_ref, k_ref, v_ref, g_ref, o_ref, *scratch, tq, tk, rt):
    qi = pl.program_id(2)
    nd = tq // tk
    ns = SB_SLOTS
    nf = qi * nd
    nzw_b = scratch[0:ns // 2]
    ls_b, tot_b = (scratch[ns // 2 + i * ns:ns // 2 + (i + 1) * ns] for i in range(2))
    acc_ref = scratch[ns // 2 + 2 * ns]
    krow = lax.broadcasted_iota(jnp.int32, (tk, tk), 0)
    kcol = lax.broadcasted_iota(jnp.int32, (tk, tk), 1)
    from_s = (krow >= kcol).astype(BF16)
    acc_ref[...] = jnp.zeros_like(acc_ref)
    @pl.when(qi == 0)
    def _():
        nzw_b[ns // 2 - 1][...] = jnp.full((tq, 2 * tk), jnp.inf, F32)
        ls_b[ns - 2][...] = jnp.zeros((tq, tk), BF16)
        tot_b[ns - 2][...] = jnp.zeros((tq, LANE), F32)

    def nz_view(slot, rows):
        return nzw_b[slot // 2].at[rows, (slot % 2) * tk:(slot % 2 + 1) * tk]

    def stage_a(start, r0, slot):
        nzw_b[slot // 2][r0:, :] = _dot_nt(q_ref[r0:, :], k_ref[pl.ds(start, 2 * tk), :])

    def stage_b(r0, slot, masked):
        for t0 in range(r0, tq, rt):
            rows = slice(t0, t0 + rt)
            nz = nz_view(slot, rows)[...]
            neg_abs = lax.bitcast_convert_type(lax.bitcast_convert_type(nz, jnp.uint32) | jnp.uint32(0x80000000), F32)
            log_stay = jnp.minimum(nz, 0.0) - jnp.log(1.0 + jnp.exp2(neg_abs)) * LOG2E
            if masked and t0 - r0 < tk:
                col = lax.broadcasted_iota(jnp.int32, nz.shape, 1)
                row = lax.broadcasted_iota(jnp.int32, nz.shape, 0)
                keep = col < row + (t0 - r0)
                log_stay = jnp.where(keep, log_stay, 0.0)
                nz_view(slot, rows)[...] = jnp.where(keep, nz, jnp.inf)
            ls_b[slot][rows, :] = log_stay.astype(BF16)
            tot_b[slot][rows, :] = jnp.broadcast_to(jnp.sum(log_stay, axis=-1, keepdims=True), (rt, LANE))

    def stage_c(start, r0, slot):
        vb = v_ref[pl.ds(start, tk), :]
        for t0 in range(r0, tq, rt):
            rows = slice(t0, t0 + rt)
            upto = _dot(ls_b[slot][rows, :], from_s)
            w = jnp.exp2(upto - nz_view(slot, rows)[...])
            acc_ref[rows, :] = acc_ref[rows, :] * jnp.exp2(tot_b[slot][rows, :]) + _dot(w.astype(BF16), vb)

    def trip(t, fill):
        for u in range(nd):
            g = t * nd + u
            if u % 2 == 0:
                stage_a(pl.multiple_of(g * tk, tk), 0, u % ns)
            if not (fill and u < 1):
                stage_b(0, (u - 1) % ns, False)
            if not (fill and u < 2):
                stage_c(pl.multiple_of((g - 2) * tk, tk), 0, (u - 2) % ns)

    @pl.when(qi > 0)
    def _():
        trip(0, True)

    def body(t, carry):
        trip(t, False)
        return carry

    lax.fori_loop(1, qi, body, 0)
    for e in range(nd + 2):
        if e < nd and e % 2 == 0:
            stage_a(pl.multiple_of((nf + e) * tk, tk), e * tk, e % ns)
        if 1 <= e <= nd:
            stage_b((e - 1) * tk, (e - 1) % ns, True)
        elif e == 0:
            stage_b(0, ns - 1, False)
        if e >= 2:
            stage_c(pl.multiple_of((nf + e - 2) * tk, tk), (e - 2) * tk, (e - 2) % ns)
        else:
            stage_c(pl.multiple_of(jnp.maximum(nf + e - 2, 0) * tk, tk), 0, (e - 2) % ns)
    o_ref[...] = (acc_ref[...] * _silu(g_ref[...].astype(F32))).astype(o_ref.dtype)


def sb_attention(z, batch, seq, tq=2048, tk=256, rt=256):
    T = z.shape[0]
    tq = min(tq, seq)
    tk = min(tk, tq)
    rt = min(rt, tk)
    nq = seq // tq
    ns = SB_SLOTS
    assert (tq // tk) % ns == 0, "slot of a block must not depend on the loop trip"
    hw = SB_W // SB_DIM
    scratch = ([pltpu.VMEM((tq, 2 * tk), F32)] * (ns // 2) + [pltpu.VMEM((tq, tk), BF16)] * ns
               + [pltpu.VMEM((tq, LANE), F32)] * ns + [pltpu.VMEM((tq, SB_DIM), F32)])
    return pl.pallas_call(
        functools.partial(_sb_kernel, tq=tq, tk=tk, rt=rt),
        grid=(batch, SB_HEADS, nq),
        in_specs=[
            pl.BlockSpec((tq, SB_DIM), lambda b, h, i: (b * nq + i, h)),
            pl.BlockSpec((seq, SB_DIM), lambda b, h, i: (b, hw + h)),
            pl.BlockSpec((seq, SB_DIM), lambda b, h, i: (b, 2 * hw + h)),
            pl.BlockSpec((tq, SB_DIM), lambda b, h, i: (b * nq + i, 3 * hw + h)),
        ],
        out_specs=pl.BlockSpec((tq, SB_DIM), lambda b, h, i: (b * nq + i, h)),
        out_shape=jax.ShapeDtypeStruct((T, SB_W), BF16),
        scratch_shapes=scratch,
        compiler_params=_cparams(("parallel", "parallel", "arbitrary")),
        name="sb_attention",
    )(z, z, z, z)


def _gla_kernel(q_ref, k_ref, v_ref, gg_ref, ga_ref, au_ref, ab_ref, ng_ref, o_ref, st_ref, incl_ref, *, nchunk):
    C, SUB, G = GLA_CHUNK, GLA_SUB, SUBLANE
    tb = nchunk * C

    @pl.when(pl.program_id(2) == 0)
    def _():
        st_ref[...] = jnp.zeros_like(st_ref)
        ri = lax.broadcasted_iota(jnp.int32, (tb, tb), 0)
        ci = lax.broadcasted_iota(jnp.int32, (tb, tb), 1)
        incl_ref[...] = jnp.where(ci <= ri, jnp.where(ci >= (ri & ~(C - 1)), 1.0, 0.0), 0.0).astype(BF16)

    grp_ri = lax.broadcasted_iota(jnp.int32, (G, C), 0)
    grp_ci = lax.broadcasted_iota(jnp.int32, (G, C), 1)

    g_hi, g_lo = _hi_lo(ga_ref[...])
    a_hi, a_lo = _hi_lo(au_ref[...])
    pre = _dot(g_hi, a_hi) + _dot(g_hi, a_lo) + _dot(g_lo, a_hi) + ab_ref[...]
    log_f = -(jnp.maximum(-pre, 0.0) + jnp.log1p(jnp.exp(-jnp.abs(pre)))) / GLA_GATE_NORM
    f_hi = log_f.astype(BF16)
    f_mid, f_lo = _hi_lo(log_f - f_hi.astype(F32))
    incl = incl_ref[...]
    b_all = (_dot(incl, f_hi) + _dot(incl, f_mid) + _dot(incl, f_lo)) * LOG2E

    st = st_ref[...]
    for c in range(nchunk):
        r0 = c * C
        q = q_ref[r0:r0 + C, :].astype(F32) * (GLA_DK ** -0.5)
        k = k_ref[r0:r0 + C, :].astype(F32)
        v = v_ref[r0:r0 + C, :]
        b = b_all[r0:r0 + C, :]
        b_last = b[C - 1:C, :]

        o_inter = _dot_nt((q * jnp.exp2(b)).astype(BF16), st.astype(BF16))

        att_rows = []
        for s in range(C // SUB):
            i0 = s * SUB
            if s > 0:
                b0 = b[i0:i0 + 1, :]
                q_dec = (q[i0:i0 + SUB, :] * jnp.exp2(b[i0:i0 + SUB, :] - b0)).astype(BF16)
                k_dec = (k * jnp.exp2(jnp.minimum(b0 - b, 0.0))).astype(BF16)
                att_far = _dot_nt(q_dec, k_dec)
            for g0 in range(i0, i0 + SUB, G):
                bg = b[g0:g0 + G, :]
                qg = q[g0:g0 + G, :]
                att = jnp.where(grp_ci < i0, att_far[g0 - i0:g0 - i0 + G, :], 0.0) if s > 0 else jnp.zeros((G, C), F32)
                for jj in range(i0, g0 + G):
                    a = jnp.sum(qg * k[jj:jj + 1, :] * jnp.exp2(bg - b[jj:jj + 1, :]), axis=-1, keepdims=True)
                    att = jnp.where(grp_ci == jj, a, att)
                att_rows.append(jnp.where(grp_ci <= grp_ri + g0, att, 0.0))
        att = jnp.concatenate(att_rows, axis=0)
        o = o_inter + _dot(att.astype(BF16), v)

        k_dec = (k * jnp.exp2(b_last - b)).astype(BF16)
        st = st * jnp.exp2(b_last) + _dot_tn(v, k_dec)

        y = _rms(o, ng_ref[...]) * _silu(gg_ref[r0:r0 + C, :].astype(F32))
        o_ref[r0:r0 + C, :] = y.astype(o_ref.dtype)
    st_ref[...] = st


def gla(z, ga, au, ab, ng, batch, seq, tb=512):
    T = z.shape[0]
    tb = min(tb, seq)
    nb = seq // tb
    q0 = 4 * SB_W // GLA_DK
    k0 = q0 + GLA_HEADS
    v0 = (4 * SB_W + 2 * GLA_KW) // GLA_DV
    g0 = v0 + GLA_HEADS
    return pl.pallas_call(
        functools.partial(_gla_kernel, nchunk=tb // GLA_CHUNK),
        grid=(batch, GLA_HEADS, nb),
        in_specs=[
            pl.BlockSpec((tb, GLA_DK), lambda b, h, i: (b * nb + i, q0 + h)),
            pl.BlockSpec((tb, GLA_DK), lambda b, h, i: (b * nb + i, k0 + h)),
            pl.BlockSpec((tb, GLA_DV), lambda b, h, i: (b * nb + i, v0 + h)),
            pl.BlockSpec((tb, GLA_DV), lambda b, h, i: (b * nb + i, g0 + h)),
            pl.BlockSpec((tb, LANE), lambda b, h, i: (b * nb + i, 0)),
            pl.BlockSpec((LANE, GLA_DK), lambda b, h, i: (0, h)),
            pl.BlockSpec((1, GLA_DK), lambda b, h, i: (0, h)),
            pl.BlockSpec((1, GLA_DV), lambda b, h, i: (0, 0)),
        ],
        out_specs=pl.BlockSpec((tb, GLA_DV), lambda b, h, i: (b * nb + i, h)),
        out_shape=jax.ShapeDtypeStruct((T, GLA_VW), BF16),
        scratch_shapes=[pltpu.VMEM((GLA_DV, GLA_DK), F32), pltpu.VMEM((tb, tb), BF16)],
        compiler_params=_cparams(("parallel", "parallel", "arbitrary")),
        name="gla",
    )(z, z, z, z, ga, au, ab, ng)


def _out_proj_kernel(y1_ref, y2_ref, w1_ref, w2_ref, x_ref, gf_ref, o_ref, *, final):
    acc = x_ref[...] + _dot(y1_ref[...], w1_ref[...]) + _dot(y2_ref[...], w2_ref[...])
    if final:
        acc = _rms(acc, gf_ref[...])
    o_ref[...] = acc


def out_proj(y1, c1, y2, c2, w, x, gf, final, tm=512):
    T, D = x.shape
    kh = w.shape[0] // 2
    tm = min(tm, T)
    return pl.pallas_call(
        functools.partial(_out_proj_kernel, final=final),
        grid=(T // tm,),
        in_specs=[
            pl.BlockSpec((tm, kh), lambda i: (i, c1)),
            pl.BlockSpec((tm, kh), lambda i: (i, c2)),
            pl.BlockSpec((kh, D), lambda i: (0, 0)),
            pl.BlockSpec((kh, D), lambda i: (1, 0)),
            pl.BlockSpec((tm, D), lambda i: (i, 0)),
            pl.BlockSpec((1, D), lambda i: (0, 0)),
        ],
        out_specs=pl.BlockSpec((tm, D), lambda i: (i, 0)),
        out_shape=jax.ShapeDtypeStruct((T, D), F32),
        compiler_params=_cparams(("parallel",)),
        name="out_proj",
    )(y1, y2, w, w, x, gf.reshape(1, D))


def _mla_up_kernel(lat_ref, pos_ref, inv_ref, qg_ref, kvg_ref, wq_ref, wkv_ref, q_ref, k_ref, v_ref, *, scale):
    hq = _rms(lat_ref[:, 0:MLA_QR], qg_ref[...]).astype(BF16)
    hkv = _rms(lat_ref[:, MLA_QR:MLA_QR + MLA_KVR], kvg_ref[...]).astype(BF16)
    ang = pos_ref[...].astype(F32) * inv_ref[...]
    lane = lax.broadcasted_iota(jnp.int32, ang.shape, 1)
    half = MLA_ROPE // 2
    cos2 = jnp.where(lane < MLA_ROPE, jnp.cos(ang), 0.0)
    sin2 = jnp.where(lane < half, -jnp.sin(ang), jnp.where(lane < MLA_ROPE, jnp.sin(ang), 0.0))

    def rope(r):
        return r * cos2 + pltpu.roll(r, MLA_ROPE, 1) * sin2

    kr = rope(lat_ref[:, MLA_QR + MLA_KVR:MLA_QR + MLA_KVR + LANE]).astype(BF16)
    kn = _dot(hkv, wkv_ref[:, 0:MLA_HEADS * MLA_NOPE]).astype(BF16)
    v_ref[...] = _dot(hkv, wkv_ref[:, MLA_HEADS * MLA_NOPE:]).astype(BF16)
    for h in range(MLA_HEADS):
        c0 = h * MLA_QK_PAD
        y = _dot(hq, wq_ref[:, c0:c0 + MLA_QK_PAD])
        q_ref[:, c0:c0 + MLA_NOPE] = (y[:, 0:MLA_NOPE] * scale).astype(BF16)
        q_ref[:, c0 + MLA_NOPE:c0 + MLA_QK_PAD] = (rope(y[:, MLA_NOPE:]) * scale).astype(BF16)
        k_ref[:, c0:c0 + MLA_NOPE] = kn[:, h * MLA_NOPE:(h + 1) * MLA_NOPE]
        k_ref[:, c0 + MLA_NOPE:c0 + MLA_QK_PAD] = kr


def mla_up(lat, pos, inv, qg, kvg, wq, wkv, tm=512):
    T = lat.shape[0]
    tm = min(tm, T)
    qk_w = MLA_HEADS * MLA_QK_PAD
    v_w = MLA_HEADS * MLA_V
    return pl.pallas_call(
        functools.partial(_mla_up_kernel, scale=(MLA_NOPE + MLA_ROPE) ** -0.5 * LOG2E),
        grid=(T // tm,),
        in_specs=[
            pl.BlockSpec((tm, lat.shape[1]), lambda i: (i, 0)),
            pl.BlockSpec((tm, 1), lambda i: (i, 0)),
            pl.BlockSpec((1, LANE), lambda i: (0, 0)),
            pl.BlockSpec((1, MLA_QR), lambda i: (0, 0)),
            pl.BlockSpec((1, MLA_KVR), lambda i: (0, 0)),
            pl.BlockSpec(wq.shape, lambda i: (0, 0)),
            pl.BlockSpec(wkv.shape, lambda i: (0, 0)),
        ],
        out_specs=[
            pl.BlockSpec((tm, qk_w), lambda i: (i, 0)),
            pl.BlockSpec((tm, qk_w), lambda i: (i, 0)),
            pl.BlockSpec((tm, v_w), lambda i: (i, 0)),
        ],
        out_shape=[
            jax.ShapeDtypeStruct((T, qk_w), BF16),
            jax.ShapeDtypeStruct((T, qk_w), BF16),
            jax.ShapeDtypeStruct((T, v_w), BF16),
        ],
        compiler_params=_cparams(("parallel",)),
        name="mla_up",
    )(lat, pos, inv, qg.reshape(1, -1), kvg.reshape(1, -1), wq, wkv)


def _mla_attn_kernel(q_ref, k_ref, v_ref, g_ref, o_ref, s0_ref, s1_ref, p0_ref, p1_ref, m_ref, l_ref, acc_ref,
                     *, tq, tk, rt, st):
    qi = pl.program_id(2)
    nd = tq // tk
    nf = qi * nd
    base = qi * tq
    reps = tk // LANE
    s_refs = (s0_ref, s1_ref)
    p_refs = (p0_ref, p1_ref)
    m_ref[...] = jnp.full_like(m_ref, -jnp.inf)
    l_ref[...] = jnp.zeros_like(l_ref)
    acc_ref[...] = jnp.zeros_like(acc_ref)

    def scores(start, r0, slot, r1=tq):
        s_refs[slot][r0:r1, :] = _dot_nt(q_ref[r0:r1, :], k_ref[pl.ds(start, tk), :])

    def update(start, r0, slot, masked):
        vb = v_ref[pl.ds(start, tk), :]
        for g0 in range(r0, tq, rt):
            alphas = []
            for t0 in range(g0, g0 + rt, st):
                rows = slice(t0, t0 + st)
                s = s_refs[slot][rows, :]
                if masked and t0 - r0 < tk:
                    col = lax.broadcasted_iota(jnp.int32, s.shape, 1)
                    row = lax.broadcasted_iota(jnp.int32, s.shape, 0)
                    s = jnp.where(col <= row + (t0 - r0), s, -jnp.inf)
                m_old = m_ref[rows, :]
                m_new = jnp.maximum(m_old, jnp.max(s, axis=-1, keepdims=True))
                alpha = jnp.exp2(m_old - m_new)
                p = jnp.exp2(s - jnp.tile(m_new, (1, reps)))
                l_ref[rows, :] = alpha * l_ref[rows, :] + jnp.sum(p, axis=-1, keepdims=True)
                p_refs[slot][rows, :] = p.astype(BF16)
                m_ref[rows, :] = m_new
                alphas.append(alpha)
            grp = slice(g0, g0 + rt)
            acc_ref[grp, :] = jnp.concatenate(alphas, axis=0) * acc_ref[grp, :] + _dot(p_refs[slot][grp, :], vb)

    scores(0, 0, 0, rt)

    def body(i, carry):
        for u in range(nd):
            j = i * nd + u
            if u == 0:
                scores(pl.multiple_of(j * tk, tk), rt, 0)
            scores(pl.multiple_of((j + 1) * tk, tk), 0, (u + 1) % 2, tq if u + 1 < nd else rt)
            update(pl.multiple_of(j * tk, tk), 0, u % 2, False)
        return carry

    lax.fori_loop(0, qi, body, 0)
    scores(pl.multiple_of(base, tk), rt, 0)
    for m in range(nd):
        if m + 1 < nd:
            scores(pl.multiple_of(base + (m + 1) * tk, tk), (m + 1) * tk, (m + 1) % 2)
        update(pl.multiple_of(base + m * tk, tk), m * tk, m % 2, True)
    o_ref[...] = (acc_ref[...] / l_ref[...] * _silu(g_ref[...].astype(F32))).astype(o_ref.dtype)


def mla_attention(qf, kf, v, gate, batch, seq, tq=2048, tk=512, rt=256):
    T = qf.shape[0]
    tq = min(tq, seq)
    tk = min(tk, tq)
    rt = min(rt, tk)
    nq = seq // tq
    assert (tq // tk) % 2 == 0 or nq == 1, "score-buffer parity is static only for an even block count per tile"
    return pl.pallas_call(
        functools.partial(_mla_attn_kernel, tq=tq, tk=tk, rt=rt, st=min(MLA_SOFTMAX_ROWS, rt)),
        grid=(batch, MLA_HEADS, nq),
        in_specs=[
            pl.BlockSpec((tq, MLA_QK_PAD), lambda b, h, i: (b * nq + i, h)),
            pl.BlockSpec((seq, MLA_QK_PAD), lambda b, h, i: (b, h)),
            pl.BlockSpec((seq, MLA_V), lambda b, h, i: (b, h)),
            pl.BlockSpec((tq, MLA_V), lambda b, h, i: (b * nq + i, h)),
        ],
        out_specs=pl.BlockSpec((tq, MLA_V), lambda b, h, i: (b * nq + i, h)),
        out_shape=jax.ShapeDtypeStruct((T, MLA_HEADS * MLA_V), BF16),
        scratch_shapes=[pltpu.VMEM((tq, tk), F32), pltpu.VMEM((tq, tk), F32),
                        pltpu.VMEM((tq, tk), BF16), pltpu.VMEM((tq, tk), BF16), pltpu.VMEM((tq, LANE), F32),
                        pltpu.VMEM((tq, LANE), F32), pltpu.VMEM((tq, MLA_V), F32)],
        compiler_params=_cparams(("parallel", "parallel", "arbitrary")),
        name="mla_attention",
    )(qf, kf, v, gate)


def _swap_halves(w):
    half = w.shape[-1] // 2
    return jnp.concatenate([w[..., half:], w[..., :half]], axis=-1)


def _prep_even(w_in, alpha_up, alpha_bias):
    col_scale = jnp.concatenate([jnp.full((SB_W,), -(SB_DIM ** -0.5) * LOG2E, F32),
                                 jnp.ones((w_in.shape[1] - SB_W,), F32)])
    w_main = (w_in * col_scale).astype(BF16)
    w_ga = jnp.pad(w_main[:, EVEN_MAIN:], ((0, 0), (0, LANE - GLA_RANK)))
    au = jnp.pad(alpha_up, ((0, LANE - GLA_RANK), (0, 0)))
    return w_main, w_ga, au, alpha_bias.reshape(1, -1)


def _prep_odd(w_in, w_q_up, w_kv_up):
    w_in, w_q_up, w_kv_up = (w.astype(BF16) for w in (w_in, w_q_up, w_kv_up))
    o1 = MLA_QR + MLA_KVR
    w_kr = w_in[:, o1:o1 + MLA_ROPE]
    w_lat = jnp.concatenate([w_in[:, :o1], w_kr, _swap_halves(w_kr)], axis=1)
    w_gate = w_in[:, o1 + MLA_ROPE:]
    wq = w_q_up.reshape(MLA_QR, MLA_HEADS, MLA_NOPE + MLA_ROPE)
    wq_r = wq[..., MLA_NOPE:]
    wq = jnp.concatenate([wq[..., :MLA_NOPE], wq_r, _swap_halves(wq_r)], axis=-1)
    wq = wq.reshape(MLA_QR, MLA_HEADS * MLA_QK_PAD)
    wkv = w_kv_up.reshape(MLA_KVR, MLA_HEADS, MLA_NOPE + MLA_V)
    wkv = jnp.concatenate([wkv[..., :MLA_NOPE].reshape(MLA_KVR, -1), wkv[..., MLA_NOPE:].reshape(MLA_KVR, -1)], axis=1)
    return w_lat, w_gate, wq, wkv


def _even_layer(x, batch, seq, norm_g, prepped, gla_norm_g, w_out, final_g, final):
    w_main, w_ga, au, ab = prepped
    z, ga = norm_matmul(x, norm_g, w_main, BF16, w_ga, F32, tm=1024, tn_cap=1792, n_cols=EVEN_MAIN)
    y_a = sb_attention(z, batch, seq)
    y_b = gla(z, ga, au, ab, gla_norm_g.reshape(1, -1), batch, seq)
    return out_proj(y_a, 0, y_b, 0, w_out, x, final_g, final)


def _odd_layer(x, pos, inv, batch, seq, norm_g, prepped, q_norm_g, kv_norm_g, w_out, final_g, final):
    w_lat, w_gate, wq, wkv = prepped
    gate, lat = norm_matmul(x, norm_g, w_gate, BF16, w_lat, F32, tm=1024, tn_cap=512)
    qf, kf, v = mla_up(lat, pos, inv, q_norm_g, kv_norm_g, wq, wkv)
    y = mla_attention(qf, kf, v, gate, batch, seq)
    return out_proj(y, 0, y, 1, w_out, x, final_g, final)


def kernel(x, positions, ln_even, w_in_even, gla_alpha_up, gla_alpha_bias, gla_norm, w_out_even,
           ln_odd, w_in_odd, q_norm, w_q_up, kv_norm, w_kv_up, w_out_odd, final_norm):
    batch, seq, d = x.shape
    depth = ln_even.shape[0] + ln_odd.shape[0]
    h = x.reshape(batch * seq, d)
    pos = positions.reshape(batch * seq, 1)
    half = MLA_ROPE // 2
    inv = ROPE_THETA ** (-jnp.arange(half, dtype=F32) / half)
    inv = jnp.concatenate([inv, inv, jnp.zeros((LANE - MLA_ROPE,), F32)]).reshape(1, LANE)
    even_w = jax.vmap(_prep_even)(w_in_even, gla_alpha_up, gla_alpha_bias)
    odd_w = jax.vmap(_prep_odd)(w_in_odd, w_q_up, w_kv_up)
    wo_even = w_out_even.astype(BF16)
    wo_odd = w_out_odd.astype(BF16)
    for layer in range(depth):
        i = layer // 2
        final = layer == depth - 1
        if layer % 2 == 0:
            h = _even_layer(h, batch, seq, ln_even[i], tuple(w[i] for w in even_w), gla_norm[i], wo_even[i],
                            final_norm, final)
        else:
            h = _odd_layer(h, pos, inv, batch, seq, ln_odd[i], tuple(w[i] for w in odd_w), q_norm[i], kv_norm[i],
                           wo_odd[i], final_norm, final)
    return h.reshape(batch, seq, d)
```

```python
import functools

import jax
import jax.numpy as jnp
from jax import lax
from jax.experimental import pallas as pl
from jax.experimental.pallas import tpu as pltpu

F32 = jnp.float32
BF16 = jnp.bfloat16

SB_HEADS = 8
SB_DIM = 128
GLA_HEADS = 4
GLA_DK = 128
GLA_DV = 256
GLA_RANK = 16
GLA_GATE_NORM = 16.0
GLA_CHUNK = 64
GLA_SUB = 16
SB_SLOTS = 4
MLA_SOFTMAX_ROWS = 64
MLA_HEADS = 16
MLA_QR = 512
MLA_KVR = 512
MLA_NOPE = 128
MLA_ROPE = 64
MLA_V = 128
MLA_QK_PAD = 256
ROPE_THETA = 10000.0
EPS = 1e-6
LOG2E = 1.4426950408889634

LANE = 128
SUBLANE = 8
VMEM_LIMIT = 48 * 1024 * 1024

SB_W = SB_HEADS * SB_DIM
GLA_KW = GLA_HEADS * GLA_DK
GLA_VW = GLA_HEADS * GLA_DV
EVEN_MAIN = 4 * SB_W + 2 * GLA_KW + 2 * GLA_VW


def _cparams(sem):
    return pltpu.CompilerParams(dimension_semantics=sem, vmem_limit_bytes=VMEM_LIMIT)


def _rms(x, g):
    return x * lax.rsqrt(jnp.mean(x * x, axis=-1, keepdims=True) + EPS) * g


def _silu(g):
    return g * (1.0 / (1.0 + jnp.exp(-g)))


def _dot(a, b):
    return jnp.dot(a, b, preferred_element_type=F32)


def _hi_lo(x):
    hi = x.astype(BF16)
    return hi, (x - hi.astype(F32)).astype(BF16)


def _dot_nt(a, b):
    return lax.dot_general(a, b, (((1,), (1,)), ((), ())), preferred_element_type=F32)


def _dot_tn(a, b):
    return lax.dot_general(a, b, (((0,), (0,)), ((), ())), preferred_element_type=F32)


def _norm_matmul_kernel(x_ref, g_ref, w_ref, ws_ref, o_ref, os_ref, h_ref):
    @pl.when(pl.program_id(1) == 0)
    def _():
        h = _rms(x_ref[...], g_ref[...]).astype(BF16)
        h_ref[...] = h
        os_ref[...] = _dot(h, ws_ref[...]).astype(os_ref.dtype)

    o_ref[...] = _dot(h_ref[...], w_ref[...]).astype(o_ref.dtype)


def _pick_tile(n, cap):
    best = LANE
    for t in range(LANE, cap + 1, LANE):
        if n % t == 0:
            best = t
    return best


def norm_matmul(x, g, w, out_dtype, w_side, side_dtype, tm, tn_cap=1024):
    T, K = x.shape
    N = w.shape[1]
    ns = w_side.shape[1]
    tn = _pick_tile(N, tn_cap)
    tm = min(tm, T)
    return pl.pallas_call(
        _norm_matmul_kernel,
        grid=(T // tm, N // tn),
        in_specs=[
            pl.BlockSpec((tm, K), lambda i, j: (i, 0)),
            pl.BlockSpec((1, K), lambda i, j: (0, 0)),
            pl.BlockSpec((K, tn), lambda i, j: (0, j)),
            pl.BlockSpec((K, ns), lambda i, j: (0, 0), pipeline_mode=pl.Buffered(1)),
        ],
        out_specs=[
            pl.BlockSpec((tm, tn), lambda i, j: (i, j)),
            pl.BlockSpec((tm, ns), lambda i, j: (i, 0)),
        ],
        out_shape=[jax.ShapeDtypeStruct((T, N), out_dtype), jax.ShapeDtypeStruct((T, ns), side_dtype)],
        scratch_shapes=[pltpu.VMEM((tm, K), BF16)],
        compiler_params=_cparams(("parallel", "arbitrary")),
        name="norm_matmul",
    )(x, g.reshape(1, K), w, w_side)


def _sb_kernel(q_ref, k_ref, v_ref, g_ref, o_ref, *scratch, tq, tk, rt):
    qi = pl.program_id(2)
    nd = tq // tk
    ns = SB_SLOTS
    nf = qi * nd
    nzw_b = scratch[0:ns // 2]
    ls_b, tot_b = (scratch[ns // 2 + i * ns:ns // 2 + (i + 1) * ns] for i in range(2))
    acc_ref = scratch[ns // 2 + 2 * ns]
    krow = lax.broadcasted_iota(jnp.int32, (tk, tk), 0)
    kcol = lax.broadcasted_iota(jnp.int32, (tk, tk), 1)
    from_s = (krow >= kcol).astype(BF16)
    acc_ref[...] = jnp.zeros_like(acc_ref)
    @pl.when(qi == 0)
    def _():
        nzw_b[ns // 2 - 1][...] = jnp.full((tq, 2 * tk), jnp.inf, F32)
        ls_b[ns - 2][...] = jnp.zeros((tq, tk), BF16)
        tot_b[ns - 2][...] = jnp.zeros((tq, LANE), F32)

    def nz_view(slot, rows):
        return nzw_b[slot // 2].at[rows, (slot % 2) * tk:(slot % 2 + 1) * tk]

    def stage_a(start, r0, slot):
        nzw_b[slot // 2][r0:, :] = _dot_nt(q_ref[r0:, :], k_ref[pl.ds(start, 2 * tk), :])

    def stage_b(r0, slot, masked):
        for t0 in range(r0, tq, rt):
            rows = slice(t0, t0 + rt)
            nz = nz_view(slot, rows)[...]
            neg_abs = lax.bitcast_convert_type(lax.bitcast_convert_type(nz, jnp.uint32) | jnp.uint32(0x80000000), F32)
            log_stay = jnp.minimum(nz, 0.0) - jnp.log(1.0 + jnp.exp2(neg_abs)) * LOG2E
            if masked and t0 - r0 < tk:
                col = lax.broadcasted_iota(jnp.int32, nz.shape, 1)
                row = lax.broadcasted_iota(jnp.int32, nz.shape, 0)
                keep = col < row + (t0 - r0)
                log_stay = jnp.where(keep, log_stay, 0.0)
                nz_view(slot, rows)[...] = jnp.where(keep, nz, jnp.inf)
            ls_b[slot][rows, :] = log_stay.astype(BF16)
            tot_b[slot][rows, :] = jnp.broadcast_to(jnp.sum(log_stay, axis=-1, keepdims=True), (rt, LANE))

    def stage_c(start, r0, slot):
        vb = v_ref[pl.ds(start, tk), :]
        for t0 in range(r0, tq, rt):
            rows = slice(t0, t0 + rt)
            upto = _dot(ls_b[slot][rows, :], from_s)
            w = jnp.exp2(upto - nz_view(slot, rows)[...])
            acc_ref[rows, :] = acc_ref[rows, :] * jnp.exp2(tot_b[slot][rows, :]) + _dot(w.astype(BF16), vb)

    def trip(t, fill):
        for u in range(nd):
            g = t * nd + u
            if u % 2 == 0:
                stage_a(pl.multiple_of(g * tk, tk), 0, u % ns)
            if not (fill and u < 1):
                stage_b(0, (u - 1) % ns, False)
            if not (fill and u < 2):
                stage_c(pl.multiple_of((g - 2) * tk, tk), 0, (u - 2) % ns)

    @pl.when(qi > 0)
    def _():
        trip(0, True)

    def body(t, carry):
        trip(t, False)
        return carry

    lax.fori_loop(1, qi, body, 0)
    for e in range(nd + 2):
        if e < nd and e % 2 == 0:
            stage_a(pl.multiple_of((nf + e) * tk, tk), e * tk, e % ns)
        if 1 <= e <= nd:
            stage_b((e - 1) * tk, (e - 1) % ns, True)
        elif e == 0:
            stage_b(0, ns - 1, False)
        if e >= 2:
            stage_c(pl.multiple_of((nf + e - 2) * tk, tk), (e - 2) * tk, (e - 2) % ns)
        else:
            stage_c(pl.multiple_of(jnp.maximum(nf + e - 2, 0) * tk, tk), 0, (e - 2) % ns)
    o_ref[...] = (acc_ref[...] * _silu(g_ref[...].astype(F32))).astype(o_ref.dtype)


def sb_attention(z, batch, seq, tq=2048, tk=256, rt=256):
    T = z.shape[0]
    tq = min(tq, seq)
    tk = min(tk, tq)
    rt = min(rt, tk)
    nq = seq // tq
    ns = SB_SLOTS
    assert (tq // tk) % ns == 0, "slot of a block must not depend on the loop trip"
    hw = SB_W // SB_DIM
    scratch = ([pltpu.VMEM((tq, 2 * tk), F32)] * (ns // 2) + [pltpu.VMEM((tq, tk), BF16)] * ns
               + [pltpu.VMEM((tq, LANE), F32)] * ns + [pltpu.VMEM((tq, SB_DIM), F32)])
    return pl.pallas_call(
        functools.partial(_sb_kernel, tq=tq, tk=tk, rt=rt),
        grid=(batch, SB_HEADS, nq),
        in_specs=[
            pl.BlockSpec((tq, SB_DIM), lambda b, h, i: (b * nq + i, h)),
            pl.BlockSpec((seq, SB_DIM), lambda b, h, i: (b, hw + h)),
            pl.BlockSpec((seq, SB_DIM), lambda b, h, i: (b, 2 * hw + h)),
            pl.BlockSpec((tq, SB_DIM), lambda b, h, i: (b * nq + i, 3 * hw + h)),
        ],
        out_specs=pl.BlockSpec((tq, SB_DIM), lambda b, h, i: (b * nq + i, h)),
        out_shape=jax.ShapeDtypeStruct((T, SB_W), BF16),
        scratch_shapes=scratch,
        compiler_params=_cparams(("parallel", "parallel", "arbitrary")),
        name="sb_attention",
    )(z, z, z, z)


def _gla_kernel(q_ref, k_ref, v_ref, gg_ref, ga_ref, au_ref, ab_ref, ng_ref, o_ref, st_ref, incl_ref, *, nchunk):
    C, SUB, G = GLA_CHUNK, GLA_SUB, SUBLANE
    tb = nchunk * C

    @pl.when(pl.program_id(2) == 0)
    def _():
        st_ref[...] = jnp.zeros_like(st_ref)
        ri = lax.broadcasted_iota(jnp.int32, (tb, tb), 0)
        ci = lax.broadcasted_iota(jnp.int32, (tb, tb), 1)
        incl_ref[...] = jnp.where(ci <= ri, jnp.where(ci >= (ri & ~(C - 1)), 1.0, 0.0), 0.0).astype(BF16)

    grp_ri = lax.broadcasted_iota(jnp.int32, (G, C), 0)
    grp_ci = lax.broadcasted_iota(jnp.int32, (G, C), 1)

    g_hi, g_lo = _hi_lo(ga_ref[...])
    a_hi, a_lo = _hi_lo(au_ref[...])
    pre = _dot(g_hi, a_hi) + _dot(g_hi, a_lo) + _dot(g_lo, a_hi) + ab_ref[...]
    log_f = -(jnp.maximum(-pre, 0.0) + jnp.log1p(jnp.exp(-jnp.abs(pre)))) / GLA_GATE_NORM
    f_hi = log_f.astype(BF16)
    f_mid, f_lo = _hi_lo(log_f - f_hi.astype(F32))
    incl = incl_ref[...]
    b_all = (_dot(incl, f_hi) + _dot(incl, f_mid) + _dot(incl, f_lo)) * LOG2E

    st = st_ref[...]
    for c in range(nchunk):
        r0 = c * C
        q = q_ref[r0:r0 + C, :].astype(F32) * (GLA_DK ** -0.5)
        k = k_ref[r0:r0 + C, :].astype(F32)
        v = v_ref[r0:r0 + C, :]
        b = b_all[r0:r0 + C, :]
        b_last = b[C - 1:C, :]

        o_inter = _dot_nt((q * jnp.exp2(b)).astype(BF16), st.astype(BF16))

        att_rows = []
        for s in range(C // SUB):
            i0 = s * SUB
            if s > 0:
                b0 = b[i0:i0 + 1, :]
                q_dec = (q[i0:i0 + SUB, :] * jnp.exp2(b[i0:i0 + SUB, :] - b0)).astype(BF16)
                k_dec = (k * jnp.exp2(jnp.minimum(b0 - b, 0.0))).astype(BF16)
                att_far = _dot_nt(q_dec, k_dec)
            for g0 in range(i0, i0 + SUB, G):
                bg = b[g0:g0 + G, :]
                qg = q[g0:g0 + G, :]
                att = jnp.where(grp_ci < i0, att_far[g0 - i0:g0 - i0 + G, :], 0.0) if s > 0 else jnp.zeros((G, C), F32)
                for jj in range(i0, g0 + G):
                    a = jnp.sum(qg * k[jj:jj + 1, :] * jnp.exp2(bg - b[jj:jj + 1, :]), axis=-1, keepdims=True)
                    att = jnp.where(grp_ci == jj, a, att)
                att_rows.append(jnp.where(grp_ci <= grp_ri + g0, att, 0.0))
        att = jnp.concatenate(att_rows, axis=0)
        o = o_inter + _dot(att.astype(BF16), v)

        k_dec = (k * jnp.exp2(b_last - b)).astype(BF16)
        st = st * jnp.exp2(b_last) + _dot_tn(v, k_dec)

        y = _rms(o, ng_ref[...]) * _silu(gg_ref[r0:r0 + C, :].astype(F32))
        o_ref[r0:r0 + C, :] = y.astype(o_ref.dtype)
    st_ref[...] = st


def gla(z, ga, au, ab, ng, batch, seq, tb=512):
    T = z.shape[0]
    tb = min(tb, seq)
    nb = seq // tb
    q0 = 4 * SB_W // GLA_DK
    k0 = q0 + GLA_HEADS
    v0 = (4 * SB_W + 2 * GLA_KW) // GLA_DV
    g0 = v0 + GLA_HEADS
    return pl.pallas_call(
        functools.partial(_gla_kernel, nchunk=tb // GLA_CHUNK),
        grid=(batch, GLA_HEADS, nb),
        in_specs=[
            pl.BlockSpec((tb, GLA_DK), lambda b, h, i: (b * nb + i, q0 + h)),
            pl.BlockSpec((tb, GLA_DK), lambda b, h, i: (b * nb + i, k0 + h)),
            pl.BlockSpec((tb, GLA_DV), lambda b, h, i: (b * nb + i, v0 + h)),
            pl.BlockSpec((tb, GLA_DV), lambda b, h, i: (b * nb + i, g0 + h)),
            pl.BlockSpec((tb, LANE), lambda b, h, i: (b * nb + i, 0)),
            pl.BlockSpec((LANE, GLA_DK), lambda b, h, i: (0, h)),
            pl.BlockSpec((1, GLA_DK), lambda b, h, i: (0, h)),
            pl.BlockSpec((1, GLA_DV), lambda b, h, i: (0, 0)),
        ],
        out_specs=pl.BlockSpec((tb, GLA_DV), lambda b, h, i: (b * nb + i, h)),
        out_shape=jax.ShapeDtypeStruct((T, GLA_VW), BF16),
        scratch_shapes=[pltpu.VMEM((GLA_DV, GLA_DK), F32), pltpu.VMEM((tb, tb), BF16)],
        compiler_params=_cparams(("parallel", "parallel", "arbitrary")),
        name="gla",
    )(z, z, z, z, ga, au, ab, ng)


def _out_proj_kernel(y1_ref, y2_ref, w1_ref, w2_ref, x_ref, gf_ref, o_ref, *, final):
    acc = x_ref[...] + _dot(y1_ref[...], w1_ref[...]) + _dot(y2_ref[...], w2_ref[...])
    if final:
        acc = _rms(acc, gf_ref[...])
    o_ref[...] = acc


def out_proj(y1, c1, y2, c2, w, x, gf, final, tm=512):
    T, D = x.shape
    kh = w.shape[0] // 2
    tm = min(tm, T)
    return pl.pallas_call(
        functools.partial(_out_proj_kernel, final=final),
        grid=(T // tm,),
        in_specs=[
            pl.BlockSpec((tm, kh), lambda i: (i, c1)),
            pl.BlockSpec((tm, kh), lambda i: (i, c2)),
            pl.BlockSpec((kh, D), lambda i: (0, 0)),
            pl.BlockSpec((kh, D), lambda i: (1, 0)),
            pl.BlockSpec((tm, D), lambda i: (i, 0)),
            pl.BlockSpec((1, D), lambda i: (0, 0)),
        ],
        out_specs=pl.BlockSpec((tm, D), lambda i: (i, 0)),
        out_shape=jax.ShapeDtypeStruct((T, D), F32),
        compiler_params=_cparams(("parallel",)),
        name="out_proj",
    )(y1, y2, w, w, x, gf.reshape(1, D))


def _mla_up_kernel(lat_ref, pos_ref, inv_ref, qg_ref, kvg_ref, wq_ref, wkv_ref, q_ref, k_ref, v_ref, *, scale):
    hq = _rms(lat_ref[:, 0:MLA_QR], qg_ref[...]).astype(BF16)
    hkv = _rms(lat_ref[:, MLA_QR:MLA_QR + MLA_KVR], kvg_ref[...]).astype(BF16)
    ang = pos_ref[...].astype(F32) * inv_ref[...]
    lane = lax.broadcasted_iota(jnp.int32, ang.shape, 1)
    half = MLA_ROPE // 2
    cos2 = jnp.where(lane < MLA_ROPE, jnp.cos(ang), 0.0)
    sin2 = jnp.where(lane < half, -jnp.sin(ang), jnp.where(lane < MLA_ROPE, jnp.sin(ang), 0.0))

    def rope(r):
        return r * cos2 + pltpu.roll(r, MLA_ROPE, 1) * sin2

    kr = rope(lat_ref[:, MLA_QR + MLA_KVR:MLA_QR + MLA_KVR + LANE]).astype(BF16)
    kn = _dot(hkv, wkv_ref[:, 0:MLA_HEADS * MLA_NOPE]).astype(BF16)
    v_ref[...] = _dot(hkv, wkv_ref[:, MLA_HEADS * MLA_NOPE:]).astype(BF16)
    for h in range(MLA_HEADS):
        c0 = h * MLA_QK_PAD
        y = _dot(hq, wq_ref[:, c0:c0 + MLA_QK_PAD])
        q_ref[:, c0:c0 + MLA_NOPE] = (y[:, 0:MLA_NOPE] * scale).astype(BF16)
        q_ref[:, c0 + MLA_NOPE:c0 + MLA_QK_PAD] = (rope(y[:, MLA_NOPE:]) * scale).astype(BF16)
        k_ref[:, c0:c0 + MLA_NOPE] = kn[:, h * MLA_NOPE:(h + 1) * MLA_NOPE]
        k_ref[:, c0 + MLA_NOPE:c0 + MLA_QK_PAD] = kr


def mla_up(lat, pos, inv, qg, kvg, wq, wkv, tm=512):
    T = lat.shape[0]
    tm = min(tm, T)
    qk_w = MLA_HEADS * MLA_QK_PAD
    v_w = MLA_HEADS * MLA_V
    return pl.pallas_call(
        functools.partial(_mla_up_kernel, scale=(MLA_NOPE + MLA_ROPE) ** -0.5 * LOG2E),
        grid=(T // tm,),
        in_specs=[
            pl.BlockSpec((tm, lat.shape[1]), lambda i: (i, 0)),
            pl.BlockSpec((tm, 1), lambda i: (i, 0)),
            pl.BlockSpec((1, LANE), lambda i: (0, 0)),
            pl.BlockSpec((1, MLA_QR), lambda i: (0, 0)),
            pl.BlockSpec((1, MLA_KVR), lambda i: (0, 0)),
            pl.BlockSpec(wq.shape, lambda i: (0, 0)),
            pl.BlockSpec(wkv.shape, lambda i: (0, 0)),
        ],
        out_specs=[
            pl.BlockSpec((tm, qk_w), lambda i: (i, 0)),
            pl.BlockSpec((tm, qk_w), lambda i: (i, 0)),
            pl.BlockSpec((tm, v_w), lambda i: (i, 0)),
        ],
        out_shape=[
            jax.ShapeDtypeStruct((T, qk_w), BF16),
            jax.ShapeDtypeStruct((T, qk_w), BF16),
            jax.ShapeDtypeStruct((T, v_w), BF16),
        ],
        compiler_params=_cparams(("parallel",)),
        name="mla_up",
    )(lat, pos, inv, qg.reshape(1, -1), kvg.reshape(1, -1), wq, wkv)


def _mla_attn_kernel(q_ref, k_ref, v_ref, g_ref, o_ref, s0_ref, s1_ref, p0_ref, p1_ref, m_ref, l_ref, acc_ref,
                     *, tq, tk, rt, st):
    qi = pl.program_id(2)
    nd = tq // tk
    nf = qi * nd
    base = qi * tq
    reps = tk // LANE
    s_refs = (s0_ref, s1_ref)
    p_refs = (p0_ref, p1_ref)
    m_ref[...] = jnp.full_like(m_ref, -jnp.inf)
    l_ref[...] = jnp.zeros_like(l_ref)
    acc_ref[...] = jnp.zeros_like(acc_ref)

    def scores(start, r0, slot, r1=tq):
        s_refs[slot][r0:r1, :] = _dot_nt(q_ref[r0:r1, :], k_ref[pl.ds(start, tk), :])

    def update(start, r0, slot, masked):
        vb = v_ref[pl.ds(start, tk), :]
        for g0 in range(r0, tq, rt):
            alphas = []
            for t0 in range(g0, g0 + rt, st):
                rows = slice(t0, t0 + st)
                s = s_refs[slot][rows, :]
                if masked and t0 - r0 < tk:
                    col = lax.broadcasted_iota(jnp.int32, s.shape, 1)
                    row = lax.broadcasted_iota(jnp.int32, s.shape, 0)
                    s = jnp.where(col <= row + (t0 - r0), s, -jnp.inf)
                m_old = m_ref[rows, :]
                m_new = jnp.maximum(m_old, jnp.max(s, axis=-1, keepdims=True))
                alpha = jnp.exp2(m_old - m_new)
                p = jnp.exp2(s - jnp.tile(m_new, (1, reps)))
                l_ref[rows, :] = alpha * l_ref[rows, :] + jnp.sum(p, axis=-1, keepdims=True)
                p_refs[slot][rows, :] = p.astype(BF16)
                m_ref[rows, :] = m_new
                alphas.append(alpha)
            grp = slice(g0, g0 + rt)
            acc_ref[grp, :] = jnp.concatenate(alphas, axis=0) * acc_ref[grp, :] + _dot(p_refs[slot][grp, :], vb)

    scores(0, 0, 0, rt)

    def body(i, carry):
        for u in range(nd):
            j = i * nd + u
            if u == 0:
                scores(pl.multiple_of(j * tk, tk), rt, 0)
            scores(pl.multiple_of((j + 1) * tk, tk), 0, (u + 1) % 2, tq if u + 1 < nd else rt)
            update(pl.multiple_of(j * tk, tk), 0, u % 2, False)
        return carry

    lax.fori_loop(0, qi, body, 0)
    scores(pl.multiple_of(base, tk), rt, 0)
    for m in range(nd):
        if m + 1 < nd:
            scores(pl.multiple_of(base + (m + 1) * tk, tk), (m + 1) * tk, (m + 1) % 2)
        update(pl.multiple_of(base + m * tk, tk), m * tk, m % 2, True)
    o_ref[...] = (acc_ref[...] / l_ref[...] * _silu(g_ref[...].astype(F32))).astype(o_ref.dtype)


def mla_attention(qf, kf, v, gate, batch, seq, tq=2048, tk=512, rt=256):
    T = qf.shape[0]
    tq = min(tq, seq)
    tk = min(tk, tq)
    rt = min(rt, tk)
    nq = seq // tq
    assert (tq // tk) % 2 == 0 or nq == 1, "score-buffer parity is static only for an even block count per tile"
    return pl.pallas_call(
        functools.partial(_mla_attn_kernel, tq=tq, tk=tk, rt=rt, st=min(MLA_SOFTMAX_ROWS, rt)),
        grid=(batch, MLA_HEADS, nq),
        in_specs=[
            pl.BlockSpec((tq, MLA_QK_PAD), lambda b, h, i: (b * nq + i, h)),
            pl.BlockSpec((seq, MLA_QK_PAD), lambda b, h, i: (b, h)),
            pl.BlockSpec((seq, MLA_V), lambda b, h, i: (b, h)),
            pl.BlockSpec((tq, MLA_V), lambda b, h, i: (b * nq + i, h)),
        ],
        out_specs=pl.BlockSpec((tq, MLA_V), lambda b, h, i: (b * nq + i, h)),
        out_shape=jax.ShapeDtypeStruct((T, MLA_HEADS * MLA_V), BF16),
        scratch_shapes=[pltpu.VMEM((tq, tk), F32), pltpu.VMEM((tq, tk), F32),
                        pltpu.VMEM((tq, tk), BF16), pltpu.VMEM((tq, tk), BF16), pltpu.VMEM((tq, LANE), F32),
                        pltpu.VMEM((tq, LANE), F32), pltpu.VMEM((tq, MLA_V), F32)],
        compiler_params=_cparams(("parallel", "parallel", "arbitrary")),
        name="mla_attention",
    )(qf, kf, v, gate)


def _swap_halves(w):
    half = w.shape[-1] // 2
    return jnp.concatenate([w[..., half:], w[..., :half]], axis=-1)


def _prep_even(w_in, alpha_up, alpha_bias):
    col_scale = jnp.concatenate([jnp.full((SB_W,), -(SB_DIM ** -0.5) * LOG2E, F32),
                                 jnp.ones((w_in.shape[1] - SB_W,), F32)])
    w_all = (w_in * col_scale).astype(BF16)
    w_main = w_all[:, :EVEN_MAIN]
    w_ga = jnp.pad(w_all[:, EVEN_MAIN:], ((0, 0), (0, LANE - GLA_RANK)))
    au = jnp.pad(alpha_up, ((0, LANE - GLA_RANK), (0, 0)))
    return w_main, w_ga, au, alpha_bias.reshape(1, -1)


def _prep_odd(w_in, w_q_up, w_kv_up):
    w_in, w_q_up, w_kv_up = (w.astype(BF16) for w in (w_in, w_q_up, w_kv_up))
    o1 = MLA_QR + MLA_KVR
    w_kr = w_in[:, o1:o1 + MLA_ROPE]
    w_lat = jnp.concatenate([w_in[:, :o1], w_kr, _swap_halves(w_kr)], axis=1)
    w_gate = w_in[:, o1 + MLA_ROPE:]
    wq = w_q_up.reshape(MLA_QR, MLA_HEADS, MLA_NOPE + MLA_ROPE)
    wq_r = wq[..., MLA_NOPE:]
    wq = jnp.concatenate([wq[..., :MLA_NOPE], wq_r, _swap_halves(wq_r)], axis=-1)
    wq = wq.reshape(MLA_QR, MLA_HEADS * MLA_QK_PAD)
    wkv = w_kv_up.reshape(MLA_KVR, MLA_HEADS, MLA_NOPE + MLA_V)
    wkv = jnp.concatenate([wkv[..., :MLA_NOPE].reshape(MLA_KVR, -1), wkv[..., MLA_NOPE:].reshape(MLA_KVR, -1)], axis=1)
    return w_lat, w_gate, wq, wkv


def _even_layer(x, batch, seq, norm_g, prepped, gla_norm_g, w_out, final_g, final):
    w_main, w_ga, au, ab = prepped
    z, ga = norm_matmul(x, norm_g, w_main, BF16, w_ga, F32, tm=1024, tn_cap=1792)
    y_a = sb_attention(z, batch, seq)
    y_b = gla(z, ga, au, ab, gla_norm_g.reshape(1, -1), batch, seq)
    return out_proj(y_a, 0, y_b, 0, w_out, x, final_g, final)


def _odd_layer(x, pos, inv, batch, seq, norm_g, prepped, q_norm_g, kv_norm_g, w_out, final_g, final):
    w_lat, w_gate, wq, wkv = prepped
    gate, lat = norm_matmul(x, norm_g, w_gate, BF16, w_lat, F32, tm=1024, tn_cap=512)
    qf, kf, v = mla_up(lat, pos, inv, q_norm_g, kv_norm_g, wq, wkv)
    y = mla_attention(qf, kf, v, gate, batch, seq)
    return out_proj(y, 0, y, 1, w_out, x, final_g, final)


def kernel(x, positions, ln_even, w_in_even, gla_alpha_up, gla_alpha_bias, gla_norm, w_out_even,
           ln_odd, w_in_odd, q_norm, w_q_up, kv_norm, w_kv_up, w_out_odd, final_norm):
    batch, seq, d = x.shape
    depth = ln_even.shape[0] + ln_odd.shape[0]
    h = x.reshape(batch * seq, d)
    pos = positions.reshape(batch * seq, 1)
    half = MLA_ROPE // 2
    inv = ROPE_THETA ** (-jnp.arange(half, dtype=F32) / half)
    inv = jnp.concatenate([inv, inv, jnp.zeros((LANE - MLA_ROPE,), F32)]).reshape(1, LANE)
    even_w = jax.vmap(_prep_even)(w_in_even, gla_alpha_up, gla_alpha_bias)
    odd_w = jax.vmap(_prep_odd)(w_in_odd, w_q_up, w_kv_up)
    wo_even = w_out_even.astype(BF16)
    wo_odd = w_out_odd.astype(BF16)
    for layer in range(depth):
        i = layer // 2
        final = layer == depth - 1
        if layer % 2 == 0:
            h = _even_layer(h, batch, seq, ln_even[i], tuple(w[i] for w in even_w), gla_norm[i], wo_even[i],
                            final_norm, final)
        else:
            h = _odd_layer(h, pos, inv, batch, seq, ln_odd[i], tuple(w[i] for w in odd_w), q_norm[i], kv_norm[i],
                           wo_odd[i], final_norm, final)
    return h.reshape(batch, seq, d)
```

```python
import functools

import jax
import jax.numpy as jnp
from jax import lax
from jax.experimental import pallas as pl
from jax.experimental.pallas import tpu as pltpu

F32 = jnp.float32
BF16 = jnp.bfloat16

SB_HEADS = 8
SB_DIM = 128
GLA_HEADS = 4
GLA_DK = 128
GLA_DV = 256
GLA_RANK = 16
GLA_GATE_NORM = 16.0
GLA_CHUNK = 64
GLA_SUB = 16
SB_SLOTS = 4
MLA_SOFTMAX_ROWS = 64
MLA_HEADS = 16
MLA_QR = 512
MLA_KVR = 512
MLA_NOPE = 128
MLA_ROPE = 64
MLA_V = 128
MLA_QK_PAD = 256
ROPE_THETA = 10000.0
EPS = 1e-6
LOG2E = 1.4426950408889634

LANE = 128
SUBLANE = 8
VMEM_LIMIT = 48 * 1024 * 1024

SB_W = SB_HEADS * SB_DIM
GLA_KW = GLA_HEADS * GLA_DK
GLA_VW = GLA_HEADS * GLA_DV
EVEN_MAIN = 4 * SB_W + 2 * GLA_KW + 2 * GLA_VW


def _cparams(sem):
    return pltpu.CompilerParams(dimension_semantics=sem, vmem_limit_bytes=VMEM_LIMIT)


def _rms(x, g):
    return x * lax.rsqrt(jnp.mean(x * x, axis=-1, keepdims=True) + EPS) * g


def _silu(g):
    return g * (1.0 / (1.0 + jnp.exp(-g)))


def _dot(a, b):
    return jnp.dot(a, b, preferred_element_type=F32)


def _hi_lo(x):
    hi = x.astype(BF16)
    return hi, (x - hi.astype(F32)).astype(BF16)


def _dot_nt(a, b):
    return lax.dot_general(a, b, (((1,), (1,)), ((), ())), preferred_element_type=F32)


def _dot_tn(a, b):
    return lax.dot_general(a, b, (((0,), (0,)), ((), ())), preferred_element_type=F32)


def _norm_matmul_kernel(x_ref, g_ref, w_ref, ws_ref, o_ref, os_ref, h_ref):
    @pl.when(pl.program_id(1) == 0)
    def _():
        h = _rms(x_ref[...], g_ref[...]).astype(BF16)
        h_ref[...] = h
        os_ref[...] = _dot(h, ws_ref[...]).astype(os_ref.dtype)

    o_ref[...] = _dot(h_ref[...], w_ref[...]).astype(o_ref.dtype)


def _pick_tile(n, cap):
    best = LANE
    for t in range(LANE, cap + 1, LANE):
        if n % t == 0:
            best = t
    return best


def norm_matmul(x, g, w, out_dtype, w_side, side_dtype, tm, tn_cap=1024):
    T, K = x.shape
    N = w.shape[1]
    ns = w_side.shape[1]
    tn = _pick_tile(N, tn_cap)
    tm = min(tm, T)
    return pl.pallas_call(
        _norm_matmul_kernel,
        grid=(T // tm, N // tn),
        in_specs=[
            pl.BlockSpec((tm, K), lambda i, j: (i, 0)),
            pl.BlockSpec((1, K), lambda i, j: (0, 0)),
            pl.BlockSpec((K, tn), lambda i, j: (0, j)),
            pl.BlockSpec((K, ns), lambda i, j: (0, 0), pipeline_mode=pl.Buffered(1)),
        ],
        out_specs=[
            pl.BlockSpec((tm, tn), lambda i, j: (i, j)),
            pl.BlockSpec((tm, ns), lambda i, j: (i, 0)),
        ],
        out_shape=[jax.ShapeDtypeStruct((T, N), out_dtype), jax.ShapeDtypeStruct((T, ns), side_dtype)],
        scratch_shapes=[pltpu.VMEM((tm, K), BF16)],
        compiler_params=_cparams(("parallel", "arbitrary")),
        name="norm_matmul",
    )(x, g.reshape(1, K), w, w_side)


def _sb_kernel(q_ref, k_ref, v_ref, g_ref, o_ref, *scratch, tq, tk, rt):
    qi = pl.program_id(2)
    nd = tq // tk
    ns = SB_SLOTS
    nf = qi * nd
    nzw_b = scratch[0:ns // 2]
    ls_b, tot_b = (scratch[ns // 2 + i * ns:ns // 2 + (i + 1) * ns] for i in range(2))
    acc_ref = scratch[ns // 2 + 2 * ns]
    krow = lax.broadcasted_iota(jnp.int32, (tk, tk), 0)
    kcol = lax.broadcasted_iota(jnp.int32, (tk, tk), 1)
    from_s = (krow >= kcol).astype(BF16)
    acc_ref[...] = jnp.zeros_like(acc_ref)
    @pl.when(qi == 0)
    def _():
        nzw_b[ns // 2 - 1][...] = jnp.full((tq, 2 * tk), jnp.inf, F32)
        ls_b[ns - 2][...] = jnp.zeros((tq, tk), BF16)
        tot_b[ns - 2][...] = jnp.zeros((tq, LANE), F32)

    def nz_view(slot, rows):
        return nzw_b[slot // 2].at[rows, (slot % 2) * tk:(slot % 2 + 1) * tk]

    def stage_a(start, r0, slot):
        nzw_b[slot // 2][r0:, :] = _dot_nt(q_ref[r0:, :], k_ref[pl.ds(start, 2 * tk), :])

    def stage_b(r0, slot, masked):
        for t0 in range(r0, tq, rt):
            rows = slice(t0, t0 + rt)
            nz = nz_view(slot, rows)[...]
            neg_abs = lax.bitcast_convert_type(lax.bitcast_convert_type(nz, jnp.uint32) | jnp.uint32(0x80000000), F32)
            log_stay = jnp.minimum(nz, 0.0) - jnp.log(1.0 + jnp.exp2(neg_abs)) * LOG2E
            if masked and t0 - r0 < tk:
                col = lax.broadcasted_iota(jnp.int32, nz.shape, 1)
                row = lax.broadcasted_iota(jnp.int32, nz.shape, 0)
                keep = col < row + (t0 - r0)
                log_stay = jnp.where(keep, log_stay, 0.0)
                nz_view(slot, rows)[...] = jnp.where(keep, nz, jnp.inf)
            ls_b[slot][rows, :] = log_stay.astype(BF16)
            tot_b[slot][rows, :] = jnp.broadcast_to(jnp.sum(log_stay, axis=-1, keepdims=True), (rt, LANE))

    def stage_c(start, r0, slot):
        vb = v_ref[pl.ds(start, tk), :]
        for t0 in range(r0, tq, rt):
            rows = slice(t0, t0 + rt)
            upto = _dot(ls_b[slot][rows, :], from_s)
            w = jnp.exp2(upto - nz_view(slot, rows)[...])
            acc_ref[rows, :] = acc_ref[rows, :] * jnp.exp2(tot_b[slot][rows, :]) + _dot(w.astype(BF16), vb)

    def trip(t, fill):
        for u in range(nd):
            g = t * nd + u
            if u % 2 == 0:
                stage_a(pl.multiple_of(g * tk, tk), 0, u % ns)
            if not (fill and u < 1):
                stage_b(0, (u - 1) % ns, False)
            if not (fill and u < 2):
                stage_c(pl.multiple_of((g - 2) * tk, tk), 0, (u - 2) % ns)

    @pl.when(qi > 0)
    def _():
        trip(0, True)

    def body(t, carry):
        trip(t, False)
        return carry

    lax.fori_loop(1, qi, body, 0)
    for e in range(nd + 2):
        if e < nd and e % 2 == 0:
            stage_a(pl.multiple_of((nf + e) * tk, tk), e * tk, e % ns)
        if 1 <= e <= nd:
            stage_b((e - 1) * tk, (e - 1) % ns, True)
        elif e == 0:
            stage_b(0, ns - 1, False)
        if e >= 2:
            stage_c(pl.multiple_of((nf + e - 2) * tk, tk), (e - 2) * tk, (e - 2) % ns)
        else:
            stage_c(pl.multiple_of(jnp.maximum(nf + e - 2, 0) * tk, tk), 0, (e - 2) % ns)
    o_ref[...] = (acc_ref[...] * _silu(g_ref[...].astype(F32))).astype(o_ref.dtype)


def sb_attention(z, batch, seq, tq=2048, tk=256, rt=256):
    T = z.shape[0]
    tq = min(tq, seq)
    tk = min(tk, tq)
    rt = min(rt, tk)
    nq = seq // tq
    ns = SB_SLOTS
    assert (tq // tk) % ns == 0, "slot of a block must not depend on the loop trip"
    hw = SB_W // SB_DIM
    scratch = ([pltpu.VMEM((tq, 2 * tk), F32)] * (ns // 2) + [pltpu.VMEM((tq, tk), BF16)] * ns
               + [pltpu.VMEM((tq, LANE), F32)] * ns + [pltpu.VMEM((tq, SB_DIM), F32)])
    return pl.pallas_call(
        functools.partial(_sb_kernel, tq=tq, tk=tk, rt=rt),
        grid=(batch, SB_HEADS, nq),
        in_specs=[
            pl.BlockSpec((tq, SB_DIM), lambda b, h, i: (b * nq + i, h)),
            pl.BlockSpec((seq, SB_DIM), lambda b, h, i: (b, hw + h)),
            pl.BlockSpec((seq, SB_DIM), lambda b, h, i: (b, 2 * hw + h)),
            pl.BlockSpec((tq, SB_DIM), lambda b, h, i: (b * nq + i, 3 * hw + h)),
        ],
        out_specs=pl.BlockSpec((tq, SB_DIM), lambda b, h, i: (b * nq + i, h)),
        out_shape=jax.ShapeDtypeStruct((T, SB_W), BF16),
        scratch_shapes=scratch,
        compiler_params=_cparams(("parallel", "parallel", "arbitrary")),
        name="sb_attention",
    )(z, z, z, z)


def _gla_kernel(q_ref, k_ref, v_ref, gg_ref, ga_ref, au_ref, ab_ref, ng_ref, o_ref, st_ref, incl_ref, *, nchunk):
    C, SUB, G = GLA_CHUNK, GLA_SUB, SUBLANE
    tb = nchunk * C

    @pl.when(pl.program_id(2) == 0)
    def _():
        st_ref[...] = jnp.zeros_like(st_ref)
        ri = lax.broadcasted_iota(jnp.int32, (tb, tb), 0)
        ci = lax.broadcasted_iota(jnp.int32, (tb, tb), 1)
        incl_ref[...] = jnp.where(ci <= ri, jnp.where(ci >= (ri & ~(C - 1)), 1.0, 0.0), 0.0).astype(BF16)

    grp_ri = lax.broadcasted_iota(jnp.int32, (G, C), 0)
    grp_ci = lax.broadcasted_iota(jnp.int32, (G, C), 1)

    g_hi, g_lo = _hi_lo(ga_ref[...])
    a_hi, a_lo = _hi_lo(au_ref[...])
    pre = _dot(g_hi, a_hi) + _dot(g_hi, a_lo) + _dot(g_lo, a_hi) + ab_ref[...]
    log_f = -(jnp.maximum(-pre, 0.0) + jnp.log1p(jnp.exp(-jnp.abs(pre)))) / GLA_GATE_NORM
    f_hi = log_f.astype(BF16)
    f_mid, f_lo = _hi_lo(log_f - f_hi.astype(F32))
    incl = incl_ref[...]
    b_all = (_dot(incl, f_hi) + _dot(incl, f_mid) + _dot(incl, f_lo)) * LOG2E

    st = st_ref[...]
    for c in range(nchunk):
        r0 = c * C
        q = q_ref[r0:r0 + C, :].astype(F32) * (GLA_DK ** -0.5)
        k = k_ref[r0:r0 + C, :].astype(F32)
        v = v_ref[r0:r0 + C, :]
        b = b_all[r0:r0 + C, :]
        b_last = b[C - 1:C, :]

        o_inter = _dot_nt((q * jnp.exp2(b)).astype(BF16), st.astype(BF16))

        att_rows = []
        for s in range(C // SUB):
            i0 = s * SUB
            if s > 0:
                b0 = b[i0:i0 + 1, :]
                q_dec = (q[i0:i0 + SUB, :] * jnp.exp2(b[i0:i0 + SUB, :] - b0)).astype(BF16)
                k_dec = (k * jnp.exp2(jnp.minimum(b0 - b, 0.0))).astype(BF16)
                att_far = _dot_nt(q_dec, k_dec)
            for g0 in range(i0, i0 + SUB, G):
                bg = b[g0:g0 + G, :]
                qg = q[g0:g0 + G, :]
                att = jnp.where(grp_ci < i0, att_far[g0 - i0:g0 - i0 + G, :], 0.0) if s > 0 else jnp.zeros((G, C), F32)
                for jj in range(i0, g0 + G):
                    a = jnp.sum(qg * k[jj:jj + 1, :] * jnp.exp2(bg - b[jj:jj + 1, :]), axis=-1, keepdims=True)
                    att = jnp.where(grp_ci == jj, a, att)
                att_rows.append(jnp.where(grp_ci <= grp_ri + g0, att, 0.0))
        att = jnp.concatenate(att_rows, axis=0)
        o = o_inter + _dot(att.astype(BF16), v)

        k_dec = (k * jnp.exp2(b_last - b)).astype(BF16)
        st = st * jnp.exp2(b_last) + _dot_tn(v, k_dec)

        y = _rms(o, ng_ref[...]) * _silu(gg_ref[r0:r0 + C, :].astype(F32))
        o_ref[r0:r0 + C, :] = y.astype(o_ref.dtype)
    st_ref[...] = st


def gla(z, ga, au, ab, ng, batch, seq, tb=512):
    T = z.shape[0]
    tb = min(tb, seq)
    nb = seq // tb
    q0 = 4 * SB_W // GLA_DK
    k0 = q0 + GLA_HEADS
    v0 = (4 * SB_W + 2 * GLA_KW) // GLA_DV
    g0 = v0 + GLA_HEADS
    return pl.pallas_call(
        functools.partial(_gla_kernel, nchunk=tb // GLA_CHUNK),
        grid=(batch, GLA_HEADS, nb),
        in_specs=[
            pl.BlockSpec((tb, GLA_DK), lambda b, h, i: (b * nb + i, q0 + h)),
            pl.BlockSpec((tb, GLA_DK), lambda b, h, i: (b * nb + i, k0 + h)),
            pl.BlockSpec((tb, GLA_DV), lambda b, h, i: (b * nb + i, v0 + h)),
            pl.BlockSpec((tb, GLA_DV), lambda b, h, i: (b * nb + i, g0 + h)),
            pl.BlockSpec((tb, LANE), lambda b, h, i: (b * nb + i, 0)),
            pl.BlockSpec((LANE, GLA_DK), lambda b, h, i: (0, h)),
            pl.BlockSpec((1, GLA_DK), lambda b, h, i: (0, h)),
            pl.BlockSpec((1, GLA_DV), lambda b, h, i: (0, 0)),
        ],
        out_specs=pl.BlockSpec((tb, GLA_DV), lambda b, h, i: (b * nb + i, h)),
        out_shape=jax.ShapeDtypeStruct((T, GLA_VW), BF16),
        scratch_shapes=[pltpu.VMEM((GLA_DV, GLA_DK), F32), pltpu.VMEM((tb, tb), BF16)],
        compiler_params=_cparams(("parallel", "parallel", "arbitrary")),
        name="gla",
    )(z, z, z, z, ga, au, ab, ng)


def _out_proj_kernel(y1_ref, y2_ref, w_ref, x_ref, gf_ref, o_ref, *, final):
    y = jnp.concatenate([y1_ref[...], y2_ref[...]], axis=1)
    acc = x_ref[...] + _dot(y, w_ref[...])
    if final:
        acc = _rms(acc, gf_ref[...])
    o_ref[...] = acc


def out_proj(y1, c1, y2, c2, w, x, gf, final, tm=512):
    T, D = x.shape
    kh = w.shape[0] // 2
    tm = min(tm, T)
    return pl.pallas_call(
        functools.partial(_out_proj_kernel, final=final),
        grid=(T // tm,),
        in_specs=[
            pl.BlockSpec((tm, kh), lambda i: (i, c1)),
            pl.BlockSpec((tm, kh), lambda i: (i, c2)),
            pl.BlockSpec((2 * kh, D), lambda i: (0, 0)),
            pl.BlockSpec((tm, D), lambda i: (i, 0)),
            pl.BlockSpec((1, D), lambda i: (0, 0)),
        ],
        out_specs=pl.BlockSpec((tm, D), lambda i: (i, 0)),
        out_shape=jax.ShapeDtypeStruct((T, D), F32),
        compiler_params=_cparams(("parallel",)),
        name="out_proj",
    )(y1, y2, w, x, gf.reshape(1, D))


def _mla_up_kernel(lat_ref, pos_ref, inv_ref, qg_ref, kvg_ref, wq_ref, wkv_ref, q_ref, k_ref, v_ref, *, scale):
    hq = _rms(lat_ref[:, 0:MLA_QR], qg_ref[...]).astype(BF16)
    hkv = _rms(lat_ref[:, MLA_QR:MLA_QR + MLA_KVR], kvg_ref[...]).astype(BF16)
    ang = pos_ref[...].astype(F32) * inv_ref[...]
    lane = lax.broadcasted_iota(jnp.int32, ang.shape, 1)
    half = MLA_ROPE // 2
    cos2 = jnp.where(lane < MLA_ROPE, jnp.cos(ang), 0.0)
    sin2 = jnp.where(lane < half, -jnp.sin(ang), jnp.where(lane < MLA_ROPE, jnp.sin(ang), 0.0))

    def rope(r):
        return r * cos2 + pltpu.roll(r, MLA_ROPE, 1) * sin2

    kr = rope(lat_ref[:, MLA_QR + MLA_KVR:MLA_QR + MLA_KVR + LANE]).astype(BF16)
    kn = _dot(hkv, wkv_ref[:, 0:MLA_HEADS * MLA_NOPE]).astype(BF16)
    v_ref[...] = _dot(hkv, wkv_ref[:, MLA_HEADS * MLA_NOPE:]).astype(BF16)
    for h in range(MLA_HEADS):
        c0 = h * MLA_QK_PAD
        y = _dot(hq, wq_ref[:, c0:c0 + MLA_QK_PAD])
        q_ref[:, c0:c0 + MLA_NOPE] = (y[:, 0:MLA_NOPE] * scale).astype(BF16)
        q_ref[:, c0 + MLA_NOPE:c0 + MLA_QK_PAD] = (rope(y[:, MLA_NOPE:]) * scale).astype(BF16)
        k_ref[:, c0:c0 + MLA_NOPE] = kn[:, h * MLA_NOPE:(h + 1) * MLA_NOPE]
        k_ref[:, c0 + MLA_NOPE:c0 + MLA_QK_PAD] = kr


def mla_up(lat, pos, inv, qg, kvg, wq, wkv, tm=512):
    T = lat.shape[0]
    tm = min(tm, T)
    qk_w = MLA_HEADS * MLA_QK_PAD
    v_w = MLA_HEADS * MLA_V
    return pl.pallas_call(
        functools.partial(_mla_up_kernel, scale=(MLA_NOPE + MLA_ROPE) ** -0.5 * LOG2E),
        grid=(T // tm,),
        in_specs=[
            pl.BlockSpec((tm, lat.shape[1]), lambda i: (i, 0)),
            pl.BlockSpec((tm, 1), lambda i: (i, 0)),
            pl.BlockSpec((1, LANE), lambda i: (0, 0)),
            pl.BlockSpec((1, MLA_QR), lambda i: (0, 0)),
            pl.BlockSpec((1, MLA_KVR), lambda i: (0, 0)),
            pl.BlockSpec(wq.shape, lambda i: (0, 0)),
            pl.BlockSpec(wkv.shape, lambda i: (0, 0)),
        ],
        out_specs=[
            pl.BlockSpec((tm, qk_w), lambda i: (i, 0)),
            pl.BlockSpec((tm, qk_w), lambda i: (i, 0)),
            pl.BlockSpec((tm, v_w), lambda i: (i, 0)),
        ],
        out_shape=[
            jax.ShapeDtypeStruct((T, qk_w), BF16),
            jax.ShapeDtypeStruct((T, qk_w), BF16),
            jax.ShapeDtypeStruct((T, v_w), BF16),
        ],
        compiler_params=_cparams(("parallel",)),
        name="mla_up",
    )(lat, pos, inv, qg.reshape(1, -1), kvg.reshape(1, -1), wq, wkv)


def _mla_attn_kernel(q_ref, k_ref, v_ref, g_ref, o_ref, s0_ref, s1_ref, p0_ref, p1_ref, m_ref, l_ref, acc_ref,
                     *, tq, tk, rt, st):
    qi = pl.program_id(2)
    nd = tq // tk
    nf = qi * nd
    base = qi * tq
    reps = tk // LANE
    s_refs = (s0_ref, s1_ref)
    p_refs = (p0_ref, p1_ref)
    m_ref[...] = jnp.full_like(m_ref, -jnp.inf)
    l_ref[...] = jnp.zeros_like(l_ref)
    acc_ref[...] = jnp.zeros_like(acc_ref)

    def scores(start, r0, slot, r1=tq):
        s_refs[slot][r0:r1, :] = _dot_nt(q_ref[r0:r1, :], k_ref[pl.ds(start, tk), :])

    def update(start, r0, slot, masked):
        vb = v_ref[pl.ds(start, tk), :]
        for g0 in range(r0, tq, rt):
            alphas = []
            for t0 in range(g0, g0 + rt, st):
                rows = slice(t0, t0 + st)
                s = s_refs[slot][rows, :]
                if masked and t0 - r0 < tk:
                    col = lax.broadcasted_iota(jnp.int32, s.shape, 1)
                    row = lax.broadcasted_iota(jnp.int32, s.shape, 0)
                    s = jnp.where(col <= row + (t0 - r0), s, -jnp.inf)
                m_old = m_ref[rows, :]
                m_new = jnp.maximum(m_old, jnp.max(s, axis=-1, keepdims=True))
                alpha = jnp.exp2(m_old - m_new)
                p = jnp.exp2(s - jnp.tile(m_new, (1, reps)))
                l_ref[rows, :] = alpha * l_ref[rows, :] + jnp.sum(p, axis=-1, keepdims=True)
                p_refs[slot][rows, :] = p.astype(BF16)
                m_ref[rows, :] = m_new
                alphas.append(alpha)
            grp = slice(g0, g0 + rt)
            acc_ref[grp, :] = jnp.concatenate(alphas, axis=0) * acc_ref[grp, :] + _dot(p_refs[slot][grp, :], vb)

    scores(0, 0, 0, rt)

    def body(i, carry):
        for u in range(nd):
            j = i * nd + u
            if u == 0:
                scores(pl.multiple_of(j * tk, tk), rt, 0)
            scores(pl.multiple_of((j + 1) * tk, tk), 0, (u + 1) % 2, tq if u + 1 < nd else rt)
            update(pl.multiple_of(j * tk, tk), 0, u % 2, False)
        return carry

    lax.fori_loop(0, qi, body, 0)
    scores(pl.multiple_of(base, tk), rt, 0)
    for m in range(nd):
        if m + 1 < nd:
            scores(pl.multiple_of(base + (m + 1) * tk, tk), (m + 1) * tk, (m + 1) % 2)
        update(pl.multiple_of(base + m * tk, tk), m * tk, m % 2, True)
    o_ref[...] = (acc_ref[...] / l_ref[...] * _silu(g_ref[...].astype(F32))).astype(o_ref.dtype)


def mla_attention(qf, kf, v, gate, batch, seq, tq=2048, tk=512, rt=256):
    T = qf.shape[0]
    tq = min(tq, seq)
    tk = min(tk, tq)
    rt = min(rt, tk)
    nq = seq // tq
    assert (tq // tk) % 2 == 0 or nq == 1, "score-buffer parity is static only for an even block count per tile"
    return pl.pallas_call(
        functools.partial(_mla_attn_kernel, tq=tq, tk=tk, rt=rt, st=min(MLA_SOFTMAX_ROWS, rt)),
        grid=(batch, MLA_HEADS, nq),
        in_specs=[
            pl.BlockSpec((tq, MLA_QK_PAD), lambda b, h, i: (b * nq + i, h)),
            pl.BlockSpec((seq, MLA_QK_PAD), lambda b, h, i: (b, h)),
            pl.BlockSpec((seq, MLA_V), lambda b, h, i: (b, h)),
            pl.BlockSpec((tq, MLA_V), lambda b, h, i: (b * nq + i, h)),
        ],
        out_specs=pl.BlockSpec((tq, MLA_V), lambda b, h, i: (b * nq + i, h)),
        out_shape=jax.ShapeDtypeStruct((T, MLA_HEADS * MLA_V), BF16),
        scratch_shapes=[pltpu.VMEM((tq, tk), F32), pltpu.VMEM((tq, tk), F32),
                        pltpu.VMEM((tq, tk), BF16), pltpu.VMEM((tq, tk), BF16), pltpu.VMEM((tq, LANE), F32),
                        pltpu.VMEM((tq, LANE), F32), pltpu.VMEM((tq, MLA_V), F32)],
        compiler_params=_cparams(("parallel", "parallel", "arbitrary")),
        name="mla_attention",
    )(qf, kf, v, gate)


def _swap_halves(w):
    half = w.shape[-1] // 2
    return jnp.concatenate([w[..., half:], w[..., :half]], axis=-1)


def _prep_even(w_in, alpha_up, alpha_bias):
    col_scale = jnp.concatenate([jnp.full((SB_W,), -(SB_DIM ** -0.5) * LOG2E, F32),
                                 jnp.ones((w_in.shape[1] - SB_W,), F32)])
    w_all = (w_in * col_scale).astype(BF16)
    w_main = w_all[:, :EVEN_MAIN]
    w_ga = jnp.pad(w_all[:, EVEN_MAIN:], ((0, 0), (0, LANE - GLA_RANK)))
    au = jnp.pad(alpha_up, ((0, LANE - GLA_RANK), (0, 0)))
    return w_main, w_ga, au, alpha_bias.reshape(1, -1)


def _prep_odd(w_in, w_q_up, w_kv_up):
    w_in, w_q_up, w_kv_up = (w.astype(BF16) for w in (w_in, w_q_up, w_kv_up))
    o1 = MLA_QR + MLA_KVR
    w_kr = w_in[:, o1:o1 + MLA_ROPE]
    w_lat = jnp.concatenate([w_in[:, :o1], w_kr, _swap_halves(w_kr)], axis=1)
    w_gate = w_in[:, o1 + MLA_ROPE:]
    wq = w_q_up.reshape(MLA_QR, MLA_HEADS, MLA_NOPE + MLA_ROPE)
    wq_r = wq[..., MLA_NOPE:]
    wq = jnp.concatenate([wq[..., :MLA_NOPE], wq_r, _swap_halves(wq_r)], axis=-1)
    wq = wq.reshape(MLA_QR, MLA_HEADS * MLA_QK_PAD)
    wkv = w_kv_up.reshape(MLA_KVR, MLA_HEADS, MLA_NOPE + MLA_V)
    wkv = jnp.concatenate([wkv[..., :MLA_NOPE].reshape(MLA_KVR, -1), wkv[..., MLA_NOPE:].reshape(MLA_KVR, -1)], axis=1)
    return w_lat, w_gate, wq, wkv


def _even_layer(x, batch, seq, norm_g, prepped, gla_norm_g, w_out, final_g, final):
    w_main, w_ga, au, ab = prepped
    z, ga = norm_matmul(x, norm_g, w_main, BF16, w_ga, F32, tm=1024, tn_cap=1792)
    y_a = sb_attention(z, batch, seq)
    y_b = gla(z, ga, au, ab, gla_norm_g.reshape(1, -1), batch, seq)
    return out_proj(y_a, 0, y_b, 0, w_out, x, final_g, final)


def _odd_layer(x, pos, inv, batch, seq, norm_g, prepped, q_norm_g, kv_norm_g, w_out, final_g, final):
    w_lat, w_gate, wq, wkv = prepped
    gate, lat = norm_matmul(x, norm_g, w_gate, BF16, w_lat, F32, tm=1024, tn_cap=512)
    qf, kf, v = mla_up(lat, pos, inv, q_norm_g, kv_norm_g, wq, wkv)
    y = mla_attention(qf, kf, v, gate, batch, seq)
    return out_proj(y, 0, y, 1, w_out, x, final_g, final)


def kernel(x, positions, ln_even, w_in_even, gla_alpha_up, gla_alpha_bias, gla_norm, w_out_even,
           ln_odd, w_in_odd, q_norm, w_q_up, kv_norm, w_kv_up, w_out_odd, final_norm):
    batch, seq, d = x.shape
    depth = ln_even.shape[0] + ln_odd.shape[0]
    h = x.reshape(batch * seq, d)
    pos = positions.reshape(batch * seq, 1)
    half = MLA_ROPE // 2
    inv = ROPE_THETA ** (-jnp.arange(half, dtype=F32) / half)
    inv = jnp.concatenate([inv, inv, jnp.zeros((LANE - MLA_ROPE,), F32)]).reshape(1, LANE)
    even_w = jax.vmap(_prep_even)(w_in_even, gla_alpha_up, gla_alpha_bias)
    odd_w = jax.vmap(_prep_odd)(w_in_odd, w_q_up, w_kv_up)
    wo_even = w_out_even.astype(BF16)
    wo_odd = w_out_odd.astype(BF16)
    for layer in range(depth):
        i = layer // 2
        final = layer == depth - 1
        if layer % 2 == 0:
            h = _even_layer(h, batch, seq, ln_even[i], tuple(w[i] for w in even_w), gla_norm[i], wo_even[i],
                            final_norm, final)
        else:
            h = _odd_layer(h, pos, inv, batch, seq, ln_odd[i], tuple(w[i] for w in odd_w), q_norm[i], kv_norm[i],
                           wo_odd[i], final_norm, final)
    return h.reshape(batch, seq, d)
```
